```python
import math
import jax, jax.numpy as jnp
from jax import lax
import numpy as np

D_MODEL = 4096
BATCH = 1
SEQ = 16384
DEPTH = 2

CTX_LEN = 256
GRID_W = 64
HEAD_DIM = 128
N_BRANCH = 4
BRANCH_W = D_MODEL // N_BRANCH
CONV_W = BRANCH_W
CONV_K = 3
DIFF_HEADS = BRANCH_W // HEAD_DIM
DIFF_QK = HEAD_DIM // 2
Q_BLOCK = 128
SWA_HEADS = BRANCH_W // HEAD_DIM
SWA_KV_HEADS = 2
SWA_WINDOW = 128
SWA_BLOCK = 128
NA_HEADS = BRANCH_W // HEAD_DIM
NA_ROWS = 8
NA_COLS = 16
ROPE_BASE = 10000.0
N_EXPERTS = 64
N_GROUPS = 8
TOPK_GROUPS = 4
TOP_K = 8
EXPERT_DIM = 256
SHARED_DIM = 256
ROUTED_SCALE = 2.5
MOE_BLOCK = 128
EPS = 1e-6
NEG = -1e30

SPLIT_SIZES = (CONV_W, CONV_W, CONV_W,
               DIFF_HEADS * 2 * DIFF_QK, DIFF_HEADS * 2 * DIFF_QK, DIFF_HEADS * HEAD_DIM,
               SWA_HEADS * HEAD_DIM, SWA_KV_HEADS * HEAD_DIM, SWA_KV_HEADS * HEAD_DIM,
               NA_HEADS * HEAD_DIM, NA_HEADS * HEAD_DIM, NA_HEADS * HEAD_DIM)
PROJ_W = sum(SPLIT_SIZES)

kernel_name = 'hybrid_diffusion_prefix_block'


def _rms(x, g):
    xf = x.astype(jnp.float32)
    y = xf * lax.rsqrt(jnp.mean(xf * xf, axis=-1, keepdims=True) + EPS)
    return (y * g.astype(jnp.float32)).astype(x.dtype)


def _modulate(h, shift, scale):
    return h * (1 + scale) + shift


def _rope_half(x, pos):
    m = x.shape[-1]
    freqs = ROPE_BASE ** (-jnp.arange(0, m, 2, dtype=jnp.float32) / m)
    ang = pos.astype(jnp.float32)[:, None] * freqs[None, :]
    shape = (1, pos.shape[0]) + (1,) * (x.ndim - 3) + (m,)
    cos = jnp.concatenate([jnp.cos(ang), jnp.cos(ang)], -1).reshape(shape)
    sin = jnp.concatenate([jnp.sin(ang), jnp.sin(ang)], -1).reshape(shape)
    xf = x.astype(jnp.float32)
    x1, x2 = jnp.split(xf, 2, axis=-1)
    rot = jnp.concatenate([-x2, x1], -1)
    return (xf * cos + rot * sin).astype(x.dtype)


def _rope_2d(x, rows, cols):
    xr, xc = jnp.split(x, 2, axis=-1)
    return jnp.concatenate([_rope_half(xr, rows), _rope_half(xc, cols)], -1)


def _split_proj(P):
    idx, acc = [], 0
    for s in SPLIT_SIZES[:-1]:
        acc += s
        idx.append(acc)
    return jnp.split(P, idx, axis=-1)


def _short_conv(b, cg, xv, w):
    u = cg * xv
    n = u.shape[1]
    up = jnp.pad(u, ((0, 0), (1, 1), (0, 0)))
    y = w[0] * up[:, :n] + w[1] * up[:, 1:n + 1] + w[2] * up[:, 2:]
    return b * y


def _diff_core(q, k, v, lam):
    s = jnp.einsum('bhmqd,bhmkd->bhmqk', q, k).astype(jnp.float32) * (DIFF_QK ** -0.5)
    p = jax.nn.softmax(s, axis=-1)
    a = p[:, :, 0] - lam * p[:, :, 1]
    return jnp.einsum('bhqk,bhkd->bhqd', a.astype(v.dtype), v)


def _diff_attention(q, k, v, qc, kc, vc, p, lam_init, rows, cols, update_ctx):
    bsz, n = q.shape[:2]
    L = kc.shape[1]
    H, dq, dv = DIFF_HEADS, DIFF_QK, HEAD_DIM

    def qk(t, g, m):
        return _rms(t.reshape(bsz, m, H, 2, dq), g)

    ql = _rope_2d(qk(q, p['diff_qnorm'], n), rows, cols).transpose(0, 2, 3, 1, 4)
    kl = _rope_2d(qk(k, p['diff_knorm'], n), rows, cols).transpose(0, 2, 3, 1, 4)
    vl = v.reshape(bsz, n, H, dv).transpose(0, 2, 1, 3)
    kcx = qk(kc, p['diff_knorm'], L).transpose(0, 2, 3, 1, 4)
    vcx = vc.reshape(bsz, L, H, dv).transpose(0, 2, 1, 3)
    lv = p['diff_lambda'].astype(jnp.float32)
    lam = jnp.exp(jnp.sum(lv[0] * lv[1])) - jnp.exp(jnp.sum(lv[2] * lv[3])) + lam_init
    k_all = jnp.concatenate([kl, kcx], axis=3)
    v_all = jnp.concatenate([vl, vcx], axis=2)
    nb = n // Q_BLOCK
    qb = ql.reshape(bsz, H, 2, nb, Q_BLOCK, dq).transpose(3, 0, 1, 2, 4, 5)
    o = lax.map(lambda t: _diff_core(t, k_all, v_all, lam), qb)
    o = o.transpose(1, 2, 0, 3, 4).reshape(bsz, H, n, dv)

    def finish(t):
        t = _rms(t, p['diff_subln']) * (1.0 - lam_init)
        return t.transpose(0, 2, 1, 3).reshape(bsz, t.shape[2], H * dv)

    y = finish(o)
    if update_ctx:
        qcx = qk(qc, p['diff_qnorm'], L).transpose(0, 2, 3, 1, 4)
        return y, finish(_diff_core(qcx, kcx, vcx, lam))
    return y, None


def _swa_attention(q, k, v, qc, kc, vc, p, rows, cols, update_ctx):
    bsz, n = q.shape[:2]
    L = kc.shape[1]
    H, KV, d, W = SWA_HEADS, SWA_KV_HEADS, HEAD_DIM, SWA_BLOCK
    G = H // KV
    nb = n // W
    scale = d ** -0.5
    ql = _rope_2d(_rms(q.reshape(bsz, n, H, d), p['swa_qnorm']), rows, cols)
    ql = ql.reshape(bsz, n, KV, G, d).transpose(0, 2, 3, 1, 4).reshape(bsz, KV, G, nb, W, d)
    kl = _rope_2d(_rms(k.reshape(bsz, n, KV, d), p['swa_knorm']), rows, cols).transpose(0, 2, 1, 3)
    vl = v.reshape(bsz, n, KV, d).transpose(0, 2, 1, 3)
    kcx = _rms(kc.reshape(bsz, L, KV, d), p['swa_knorm']).transpose(0, 2, 1, 3)
    vcx = vc.reshape(bsz, L, KV, d).transpose(0, 2, 1, 3)

    def band(t):
        tp = jnp.pad(t, ((0, 0), (0, 0), (W, W), (0, 0))).reshape(bsz, KV, nb + 2, W, d)
        return jnp.concatenate([tp[:, :, :-2], tp[:, :, 1:-1], tp[:, :, 2:]], axis=3)

    kw, vw = band(kl), band(vl)
    qpos = jnp.arange(nb)[:, None] * W + jnp.arange(W)[None, :]
    kpos = (jnp.arange(nb)[:, None] - 1) * W + jnp.arange(3 * W)[None, :]
    mask = ((jnp.abs(qpos[:, :, None] - kpos[:, None, :]) <= SWA_WINDOW)
            & (kpos[:, None, :] >= 0) & (kpos[:, None, :] < n))
    sink = p['swa_sink'].astype(jnp.float32).reshape(KV, G)
    s_w = jnp.einsum('bkgnqd,bknjd->bkgnqj', ql, kw).astype(jnp.float32) * scale
    s_w = jnp.where(mask, s_w, NEG)
    s_c = jnp.einsum('bkgnqd,bkjd->bkgnqj', ql, kcx).astype(jnp.float32) * scale
    sink_b = jnp.broadcast_to(sink[None, :, :, None, None, None], s_c.shape[:-1] + (1,))
    pr = jax.nn.softmax(jnp.concatenate([s_w, s_c, sink_b], -1), axis=-1)
    pw = pr[..., :3 * W].astype(v.dtype)
    pc = pr[..., 3 * W:3 * W + L].astype(v.dtype)
    o = jnp.einsum('bkgnqj,bknjd->bkgnqd', pw, vw) + jnp.einsum('bkgnqj,bkjd->bkgnqd', pc, vcx)
    y = o.reshape(bsz, KV, G, n, d).transpose(0, 3, 1, 2, 4).reshape(bsz, n, H * d)
    if update_ctx:
        qcx = _rms(qc.reshape(bsz, L, H, d), p['swa_qnorm']).reshape(bsz, L, KV, G, d).transpose(0, 2, 3, 1, 4)
        s = jnp.einsum('bkgqd,bkjd->bkgqj', qcx, kcx).astype(jnp.float32) * scale
        sb = jnp.broadcast_to(sink[None, :, :, None, None], s.shape[:-1] + (1,))
        pcc = jax.nn.softmax(jnp.concatenate([s, sb], -1), axis=-1)[..., :L].astype(v.dtype)
        oc = jnp.einsum('bkgqj,bkjd->bkgqd', pcc, vcx)
        return y, oc.transpose(0, 3, 1, 2, 4).reshape(bsz, L, H * d)
    return y, None


def _na_attention(q, k, v, qc, kc, vc, p, update_ctx):
    bsz, n = q.shape[:2]
    L = kc.shape[1]
    H, d = NA_HEADS, HEAD_DIM
    n_rows = n // GRID_W
    kr = min(NA_ROWS, n_rows)
    scale = d ** -0.5
    qg = _rms(q.reshape(bsz, n, H, d), p['na_qnorm']).transpose(0, 2, 1, 3).reshape(bsz, H, n_rows, GRID_W, d)
    kg = _rms(k.reshape(bsz, n, H, d), p['na_knorm']).transpose(0, 2, 1, 3).reshape(bsz, H, n_rows, GRID_W, d)
    vg = v.reshape(bsz, n, H, d).transpose(0, 2, 1, 3).reshape(bsz, H, n_rows, GRID_W, d)
    kcx = _rms(kc.reshape(bsz, L, H, d), p['na_knorm']).transpose(0, 2, 1, 3)
    vcx = vc.reshape(bsz, L, H, d).transpose(0, 2, 1, 3)
    cidx = jnp.arange(GRID_W)
    c0 = jnp.clip(cidx - NA_COLS // 2, 0, GRID_W - NA_COLS)
    col_idx = c0[:, None] + jnp.arange(NA_COLS)[None, :]
    col_off = col_idx - cidx[:, None] + (NA_COLS - 1)
    rpb_cols = p['na_rpb'][:, :, col_off]

    def row_fn(r):
        r0 = jnp.clip(r - kr // 2, 0, n_rows - kr)
        kn_ = lax.dynamic_slice_in_dim(kg, r0, kr, axis=2)[:, :, :, col_idx]
        vn_ = lax.dynamic_slice_in_dim(vg, r0, kr, axis=2)[:, :, :, col_idx]
        qr = lax.dynamic_index_in_dim(qg, r, axis=2, keepdims=False)
        row_off = r0 + jnp.arange(kr) - r + (NA_ROWS - 1)
        bias = rpb_cols[:, row_off].transpose(0, 2, 1, 3).astype(jnp.float32)
        s_n = jnp.einsum('bhqd,bhaqjd->bhqaj', qr, kn_).astype(jnp.float32) * scale + bias[None]
        s_n = s_n.reshape(bsz, H, GRID_W, kr * NA_COLS)
        s_c = jnp.einsum('bhqd,bhjd->bhqj', qr, kcx).astype(jnp.float32) * scale
        pr = jax.nn.softmax(jnp.concatenate([s_n, s_c], -1), axis=-1)
        pn = pr[..., :kr * NA_COLS].reshape(bsz, H, GRID_W, kr, NA_COLS).astype(v.dtype)
        pc = pr[..., kr * NA_COLS:].astype(v.dtype)
        return jnp.einsum('bhqaj,bhaqjd->bhqd', pn, vn_) + jnp.einsum('bhqj,bhjd->bhqd', pc, vcx)

    o = lax.map(row_fn, jnp.arange(n_rows))
    y = o.transpose(1, 0, 3, 2, 4).reshape(bsz, n, H * d)
    if update_ctx:
        qcx = _rms(qc.reshape(bsz, L, H, d), p['na_qnorm']).transpose(0, 2, 1, 3)
        s = jnp.einsum('bhqd,bhjd->bhqj', qcx, kcx).astype(jnp.float32) * scale
        oc = jnp.einsum('bhqj,bhjd->bhqd', jax.nn.softmax(s, axis=-1).astype(v.dtype), vcx)
        return y, oc.transpose(0, 2, 1, 3).reshape(bsz, L, H * d)
    return y, None


def _merge(h, ys, p):
    acc = None
    for i, y in enumerate(ys):
        g = jax.nn.sigmoid(h @ p['w_gate'][i] + p['b_gate'][i])
        term = g * (y @ p['w_branch'][i])
        acc = term if acc is None else acc + term
    return acc @ p['w_out']


def _hybrid_mixer(h, hc, p, lam_init, rows, cols, update_ctx):
    pl = _split_proj(h @ p['w_in'])
    pc = _split_proj(hc @ p['w_in'])
    a = _short_conv(pl[0], pl[1], pl[2], p['conv_w'])
    b, bc = _diff_attention(pl[3], pl[4], pl[5], pc[3], pc[4], pc[5], p, lam_init, rows, cols, update_ctx)
    s, sc = _swa_attention(pl[6], pl[7], pl[8], pc[6], pc[7], pc[8], p, rows, cols, update_ctx)
    nn, nc = _na_attention(pl[9], pl[10], pl[11], pc[9], pc[10], pc[11], p, update_ctx)
    y = _merge(h, (a, b, s, nn), p)
    if update_ctx:
        ac = _short_conv(pc[0], pc[1], pc[2], p['conv_w'])
        return y, _merge(hc, (ac, bc, sc, nc), p)
    return y, None


def _swiglu(x, wg, wu, wd):
    return (jax.nn.silu(x @ wg) * (x @ wu)) @ wd


def _dispatch(h, idx, w, p):
    T, D = h.shape
    TK = T * TOP_K
    flat_e = idx.reshape(-1)
    flat_tok = jnp.arange(TK, dtype=jnp.int32) // TOP_K
    flat_w = w.reshape(-1)
    order = jnp.argsort(flat_e)
    e_s, tok_s, w_s = flat_e[order], flat_tok[order], flat_w[order]
    counts = jax.ops.segment_sum(jnp.ones((TK,), jnp.int32), flat_e, num_segments=N_EXPERTS)
    padded = ((counts + MOE_BLOCK - 1) // MOE_BLOCK) * MOE_BLOCK
    start = jnp.cumsum(counts) - counts
    pend = jnp.cumsum(padded)
    pstart = pend - padded
    dest = pstart[e_s] + (jnp.arange(TK, dtype=jnp.int32) - start[e_s])
    n_blk = (TK + MOE_BLOCK - 1) // MOE_BLOCK + N_EXPERTS
    n_slot = n_blk * MOE_BLOCK
    slot_tok = jnp.full((n_slot,), T, jnp.int32).at[dest].set(tok_s)
    slot_w = jnp.zeros((n_slot,), jnp.float32).at[dest].set(w_s)
    blk_e = jnp.clip(jnp.searchsorted(pend, jnp.arange(n_blk) * MOE_BLOCK, side='right'), 0, N_EXPERTS - 1)
    h_pad = jnp.concatenate([h, jnp.zeros((1, D), h.dtype)], axis=0)

    def block(args):
        toks, e = args
        xb = h_pad[toks]
        return _swiglu(xb, p['w_exp_gate'][e], p['w_exp_up'][e], p['w_exp_down'][e])

    y = lax.map(block, (slot_tok.reshape(n_blk, MOE_BLOCK), blk_e))
    y = y.reshape(n_slot, D) * slot_w[:, None].astype(y.dtype)
    return jax.ops.segment_sum(y, slot_tok, num_segments=T + 1)[:T]


def _moe(h, p):
    T = h.shape[0]
    per = N_EXPERTS // N_GROUPS
    scores = jax.nn.sigmoid((h @ p['w_router']).astype(jnp.float32))
    choice = scores + p['b_router'].astype(jnp.float32)
    grp = lax.top_k(choice.reshape(T, N_GROUPS, per), 2)[0].sum(-1)
    _, top_g = lax.top_k(grp, TOPK_GROUPS)
    gmask = jax.nn.one_hot(top_g, N_GROUPS, dtype=jnp.float32).sum(1) > 0
    emask = jnp.repeat(gmask, per, axis=1)
    _, idx = lax.top_k(jnp.where(emask, choice, -jnp.inf), TOP_K)
    w = jnp.take_along_axis(scores, idx, axis=1)
    w = w / jnp.sum(w, axis=-1, keepdims=True) * ROUTED_SCALE
    return _dispatch(h, idx, w, p) + _swiglu(h, p['w_sh_gate'], p['w_sh_up'], p['w_sh_down'])


def _normal(k, shape, scale):
    return jax.random.normal(k, shape, jnp.float32) * scale


def setup_inputs(seed: int = 0) -> dict:
    key = jax.random.key(seed)
    ks = jax.random.split(key, 32)
    D = D_MODEL
    return {
        'x': _normal(ks[0], (BATCH, SEQ, D), 1.0),
        'c': _normal(ks[1], (BATCH, D), 1.0),
        'ctx': _normal(ks[2], (BATCH, CTX_LEN, D), 1.0),
        'c_ctx': _normal(ks[3], (D,), 1.0),
        'w_mod': _normal(ks[4], (DEPTH, D, 6 * D), 0.5 * D ** -0.5),
        'b_mod': _normal(ks[5], (DEPTH, 6 * D), 0.01),
        'g_norm1': 1.0 + _normal(ks[6], (DEPTH, D), 0.02),
        'g_norm2': 1.0 + _normal(ks[7], (DEPTH, D), 0.02),
        'w_in': _normal(ks[8], (DEPTH, D, PROJ_W), D ** -0.5),
        'conv_w': _normal(ks[9], (DEPTH, CONV_K, CONV_W), CONV_K ** -0.5),
        'diff_qnorm': 1.0 + _normal(ks[10], (DEPTH, DIFF_QK), 0.02),
        'diff_knorm': 1.0 + _normal(ks[11], (DEPTH, DIFF_QK), 0.02),
        'diff_lambda': _normal(ks[12], (DEPTH, 4, DIFF_QK), 0.1),
        'diff_subln': 1.0 + _normal(ks[13], (DEPTH, HEAD_DIM), 0.02),
        'swa_qnorm': 1.0 + _normal(ks[14], (DEPTH, HEAD_DIM), 0.02),
        'swa_knorm': 1.0 + _normal(ks[15], (DEPTH, HEAD_DIM), 0.02),
        'swa_sink': _normal(ks[16], (DEPTH, SWA_HEADS), 0.5),
        'na_qnorm': 1.0 + _normal(ks[17], (DEPTH, HEAD_DIM), 0.02),
        'na_knorm': 1.0 + _normal(ks[18], (DEPTH, HEAD_DIM), 0.02),
        'na_rpb': _normal(ks[19], (DEPTH, NA_HEADS, 2 * NA_ROWS - 1, 2 * NA_COLS - 1), 0.1),
        'w_gate': _normal(ks[20], (DEPTH, N_BRANCH, D, D), D ** -0.5),
        'b_gate': _normal(ks[21], (DEPTH, N_BRANCH, D), 0.01),
        'w_branch': _normal(ks[22], (DEPTH, N_BRANCH, BRANCH_W, D), BRANCH_W ** -0.5),
        'w_out': _normal(ks[23], (DEPTH, D, D), D ** -0.5),
        'w_router': _normal(ks[24], (DEPTH, D, N_EXPERTS), D ** -0.5),
        'b_router': _normal(ks[25], (DEPTH, N_EXPERTS), 0.01),
        'w_exp_gate': _normal(ks[26], (DEPTH, N_EXPERTS, D, EXPERT_DIM), D ** -0.5),
        'w_exp_up': _normal(ks[27], (DEPTH, N_EXPERTS, D, EXPERT_DIM), D ** -0.5),
        'w_exp_down': _normal(ks[28], (DEPTH, N_EXPERTS, EXPERT_DIM, D), EXPERT_DIM ** -0.5),
        'w_sh_gate': _normal(ks[29], (DEPTH, D, SHARED_DIM), D ** -0.5),
        'w_sh_up': _normal(ks[30], (DEPTH, D, SHARED_DIM), D ** -0.5),
        'w_sh_down': _normal(ks[31], (DEPTH, SHARED_DIM, D), SHARED_DIM ** -0.5),
    }


def reference(x, c, ctx, c_ctx, w_mod, b_mod, g_norm1, g_norm2, w_in, conv_w, diff_qnorm, diff_knorm,
              diff_lambda, diff_subln, swa_qnorm, swa_knorm, swa_sink, na_qnorm, na_knorm, na_rpb,
              w_gate, b_gate, w_branch, w_out, w_router, b_router, w_exp_gate, w_exp_up, w_exp_down,
              w_sh_gate, w_sh_up, w_sh_down):
    bsz, n, D = x.shape
    L = ctx.shape[1]
    t = jnp.arange(n, dtype=jnp.int32)
    rows = t // GRID_W
    cols = t % GRID_W
    xc = ctx
    for l in range(DEPTH):
        update_ctx = l < DEPTH - 1
        p = dict(w_in=w_in[l], conv_w=conv_w[l], diff_qnorm=diff_qnorm[l], diff_knorm=diff_knorm[l],
                 diff_lambda=diff_lambda[l], diff_subln=diff_subln[l], swa_qnorm=swa_qnorm[l],
                 swa_knorm=swa_knorm[l], swa_sink=swa_sink[l], na_qnorm=na_qnorm[l], na_knorm=na_knorm[l],
                 na_rpb=na_rpb[l], w_gate=w_gate[l], b_gate=b_gate[l], w_branch=w_branch[l], w_out=w_out[l],
                 w_router=w_router[l], b_router=b_router[l], w_exp_gate=w_exp_gate[l], w_exp_up=w_exp_up[l],
                 w_exp_down=w_exp_down[l], w_sh_gate=w_sh_gate[l], w_sh_up=w_sh_up[l], w_sh_down=w_sh_down[l])
        lam_init = 0.8 - 0.6 * math.exp(-0.3 * l)
        mod = jax.nn.silu(c) @ w_mod[l] + b_mod[l]
        sh1, sc1, g1, sh2, sc2, g2 = [m[:, None, :] for m in jnp.split(mod, 6, axis=-1)]
        modc = jax.nn.silu(c_ctx) @ w_mod[l] + b_mod[l]
        sh1c, sc1c, g1c, sh2c, sc2c, g2c = jnp.split(modc, 6, axis=-1)
        h = _modulate(_rms(x, g_norm1[l]), sh1, sc1)
        hc = _modulate(_rms(xc, g_norm1[l]), sh1c, sc1c)
        y, yc = _hybrid_mixer(h, hc, p, lam_init, rows, cols, update_ctx)
        x = x + g1 * y
        h2 = _modulate(_rms(x, g_norm2[l]), sh2, sc2)
        if update_ctx:
            xc = xc + g1c * yc
            h2c = _modulate(_rms(xc, g_norm2[l]), sh2c, sc2c)
            tokens = jnp.concatenate([h2.reshape(bsz * n, D), h2c.reshape(bsz * L, D)], axis=0)
            f = _moe(tokens, p)
            x = x + g2 * f[:bsz * n].reshape(bsz, n, D)
            xc = xc + g2c * f[bsz * n:].reshape(bsz, L, D)
        else:
            x = x + g2 * _moe(h2.reshape(bsz * n, D), p).reshape(bsz, n, D)
    return x
```

```python
import functools
import math

import jax
import jax.numpy as jnp
from jax import lax
from jax.experimental import pallas as pl
from jax.experimental.pallas import tpu as pltpu

GRID_W = 64
HEAD_DIM = 128
DIFF_QK = 64
SWA_WINDOW = 128
SWA_BLOCK = 128
SWA_KV_HEADS = 2
N_HEADS = 8
NA_ROWS = 8
NA_COLS = 16
ROPE_BASE = 10000.0
N_EXPERTS = 64
N_GROUPS = 8
TOPK_GROUPS = 4
TOP_K = 8
EXPERT_DIM = 256
ROUTED_SCALE = 2.5
EPS = 1e-6
NEG = -1e30

LANES = 128
SUBLANES = 8
V7X_VMEM_BYTES = 64 * 1024 * 1024

MXU_DTYPE = jnp.bfloat16
MOE_TILE = 256
COMBINE_TILE = 64

F32 = jnp.float32
I32 = jnp.int32
U32 = jnp.uint32


def _params(sem, vmem_mb=48):
    return pltpu.CompilerParams(dimension_semantics=sem, vmem_limit_bytes=vmem_mb * 1024 * 1024)


def _pick(m, cands):
    for c in cands:
        if m % c == 0:
            return c
    raise ValueError(f"no tile for {m} in {cands}")


def _nt_dot(a, b):
    return lax.dot_general(a, b, (((1,), (1,)), ((), ())), preferred_element_type=F32)


def _dot(a, b):
    return jnp.dot(a, b, preferred_element_type=F32)


def _sigmoid(x):
    return 1.0 / (1.0 + jnp.exp(-x))


def _modvec_kernel(s_ref, w_ref, b_ref, o_ref):
    d, tn = w_ref.shape
    rows = []
    for r in range(2):
        c = s_ref[r]
        s = c * _sigmoid(c)
        parts = []
        for j in range(tn // LANES):
            prod = w_ref[:, j * LANES:(j + 1) * LANES] * s
            part = jnp.sum(prod.reshape(d // SUBLANES, SUBLANES, LANES), axis=0)
            parts.append(jnp.sum(part, axis=0, keepdims=True))
        rows.append(jnp.concatenate(parts, axis=1))
    is_first = lax.broadcasted_iota(I32, (2, tn), 0) == 0
    o_ref[...] = jnp.where(is_first, rows[0], rows[1]) + b_ref[...]


def _modvec(cc, w, b):
    d, n = w.shape
    tn = 512
    s_rep = jnp.broadcast_to(cc[:, :, None], (2, d, LANES))
    return pl.pallas_call(
        _modvec_kernel,
        grid=(n // tn,),
        in_specs=[pl.BlockSpec((2, d, LANES), lambda j: (0, 0, 0)),
                  pl.BlockSpec((d, tn), lambda j: (0, j)),
                  pl.BlockSpec((1, tn), lambda j: (0, j))],
        out_specs=pl.BlockSpec((2, tn), lambda j: (0, j)),
        out_shape=jax.ShapeDtypeStruct((2, n), F32),
        compiler_params=_params(("arbitrary",)),
        name="modvec",
    )(s_rep, w, b.reshape(1, n))


def _pack_halves(v):
    c = v.shape[1] // 2
    bits = lax.bitcast_convert_type(v.astype(jnp.bfloat16).astype(F32), U32)
    return (bits[:, c:] & jnp.uint32(0xFFFF0000)) | (bits[:, :c] >> 16)


def _unpack_halves(w):
    lo = lax.bitcast_convert_type(w << 16, F32)
    hi = lax.bitcast_convert_type(w & jnp.uint32(0xFFFF0000), F32)
    return lo, hi


def _rmsmod_kernel(n_lat, packed, x_ref, g_ref, sh_ref, sc_ref, o_ref, *p_ref):
    tr = x_ref.shape[0]
    x = x_ref[...]
    ms = jnp.mean(x * x, axis=-1, keepdims=True)
    y = x * lax.rsqrt(ms + EPS) * g_ref[...]
    rows = pl.program_id(0) * tr + lax.broadcasted_iota(I32, (tr, 1), 0)
    is_ctx = rows >= n_lat
    sh = jnp.where(is_ctx, sh_ref[1:2, :], sh_ref[0:1, :])
    sc = jnp.where(is_ctx, sc_ref[1:2, :], sc_ref[0:1, :])
    h = y * (1.0 + sc) + sh
    o_ref[...] = h.astype(o_ref.dtype)
    if packed:
        p_ref[0][...] = _pack_halves(h)


def _rmsmod(x, g, mod, shift_blk, scale_blk, n_lat, m_rows, packed=False):
    d = x.shape[1]
    tr = 256
    out_shape = [jax.ShapeDtypeStruct((m_rows, d), MXU_DTYPE)]
    out_specs = [pl.BlockSpec((tr, d), lambda i: (i, 0))]
    if packed:
        out_shape.append(jax.ShapeDtypeStruct((m_rows, d // 2), U32))
        out_specs.append(pl.BlockSpec((tr, d // 2), lambda i: (i, 0)))
    res = pl.pallas_call(
        functools.partial(_rmsmod_kernel, n_lat, packed),
        grid=(m_rows // tr,),
        in_specs=[pl.BlockSpec((tr, d), lambda i: (i, 0)),
                  pl.BlockSpec((1, d), lambda i: (0, 0)),
                  pl.BlockSpec((2, d), lambda i: (0, shift_blk)),
                  pl.BlockSpec((2, d), lambda i: (0, scale_blk))],
        out_specs=out_specs,
        out_shape=out_shape,
        compiler_params=_params(("arbitrary",)),
        name="rmsmod",
    )(x, g.reshape(1, d), mod, mod)
    return res if packed else res[0]


def _mm_kernel(a_ref, w_ref, o_ref):
    o_ref[...] = _dot(a_ref[...], w_ref[...]).astype(o_ref.dtype)


def _mm_res_kernel(n_lat, a_ref, w_ref, r_ref, g_ref, o_ref):
    tm = a_ref.shape[0]
    acc = _dot(a_ref[...], w_ref[...])
    rows = pl.program_id(0) * tm + lax.broadcasted_iota(I32, (tm, 1), 0)
    g = jnp.where(rows >= n_lat, g_ref[1:2, :], g_ref[0:1, :])
    o_ref[...] = r_ref[...] + g * acc


def _matmul(a, w, out_dtype, m_rows):
    k = a.shape[1]
    n = w.shape[1]
    tm = _pick(m_rows, (1280, 1024, 640, 512, 256))
    tn = 512
    return pl.pallas_call(
        _mm_kernel,
        grid=(m_rows // tm, n // tn),
        in_specs=[pl.BlockSpec((tm, k), lambda i, j: (i, 0)),
                  pl.BlockSpec((k, tn), lambda i, j: (0, j))],
        out_specs=pl.BlockSpec((tm, tn), lambda i, j: (i, j)),
        out_shape=jax.ShapeDtypeStruct((m_rows, n), out_dtype),
        compiler_params=_params(("arbitrary", "arbitrary")),
        name="matmul",
    )(a, w)


def _matmul_residual(a, w, res, mod, gate_blk, n_lat, m_rows):
    k = a.shape[1]
    n = w.shape[1]
    tm = _pick(m_rows, (1280, 1024, 640, 512, 256))
    tn = 512
    nb = n // tn
    return pl.pallas_call(
        functools.partial(_mm_res_kernel, n_lat),
        grid=(m_rows // tm, nb),
        in_specs=[pl.BlockSpec((tm, k), lambda i, j: (i, 0)),
                  pl.BlockSpec((k, tn), lambda i, j: (0, j)),
                  pl.BlockSpec((tm, tn), lambda i, j: (i, j)),
                  pl.BlockSpec((2, tn), lambda i, j: (0, gate_blk * nb + j))],
        out_specs=pl.BlockSpec((tm, tn), lambda i, j: (i, j)),
        out_shape=jax.ShapeDtypeStruct((m_rows, n), F32),
        compiler_params=_params(("arbitrary", "arbitrary")),
        name="matmul_residual",
    )(a, w, res, mod)


def _conv_kernel(n_lat, t_all, b_ref, c_ref, x_ref, cp_ref, xp_ref, cn_ref, xn_ref, w_ref, o_ref):
    tr = b_ref.shape[0]
    u = c_ref[...].astype(F32) * x_ref[...].astype(F32)
    u_prev_row = cp_ref[SUBLANES - 1:SUBLANES, :].astype(F32) * xp_ref[SUBLANES - 1:SUBLANES, :].astype(F32)
    u_next_row = cn_ref[0:1, :].astype(F32) * xn_ref[0:1, :].astype(F32)
    r = lax.broadcasted_iota(I32, (tr, 1), 0)
    tok = pl.program_id(0) * tr + r
    up = jnp.where(r == 0, u_prev_row, pltpu.roll(u, 1, 0))
    un = jnp.where(r == tr - 1, u_next_row, pltpu.roll(u, tr - 1, 0))
    up = jnp.where((tok == 0) | (tok == n_lat), 0.0, up)
    un = jnp.where((tok == n_lat - 1) | (tok == t_all - 1), 0.0, un)
    y = w_ref[0:1, :] * up + w_ref[1:2, :] * u + w_ref[2:3, :] * un
    o_ref[...] = (b_ref[...].astype(F32) * y).astype(o_ref.dtype)


def _short_conv(p, conv_w, n_lat, m_rows):
    t_all = p.shape[0]
    cw = conv_w.shape[1]
    tr = 256
    tc = 512
    nc = cw // tc
    rb = tr // SUBLANES
    last = t_all // SUBLANES - 1
    return pl.pallas_call(
        functools.partial(_conv_kernel, n_lat, t_all),
        grid=(m_rows // tr, nc),
        in_specs=[pl.BlockSpec((tr, tc), lambda i, j: (i, j)),
                  pl.BlockSpec((tr, tc), lambda i, j: (i, nc + j)),
                  pl.BlockSpec((tr, tc), lambda i, j: (i, 2 * nc + j)),
                  pl.BlockSpec((SUBLANES, tc), lambda i, j: (jnp.maximum(i * rb - 1, 0), nc + j)),
                  pl.BlockSpec((SUBLANES, tc), lambda i, j: (jnp.maximum(i * rb - 1, 0), 2 * nc + j)),
                  pl.BlockSpec((SUBLANES, tc), lambda i, j: (jnp.minimum((i + 1) * rb, last), nc + j)),
                  pl.BlockSpec((SUBLANES, tc), lambda i, j: (jnp.minimum((i + 1) * rb, last), 2 * nc + j)),
                  pl.BlockSpec((3, tc), lambda i, j: (0, j))],
        out_specs=pl.BlockSpec((tr, tc), lambda i, j: (i, j)),
        out_shape=jax.ShapeDtypeStruct((m_rows, cw), MXU_DTYPE),
        compiler_params=_params(("arbitrary", "arbitrary")),
        name="short_conv",
    )(p, p, p, p, p, p, p, conv_w)


def _prep_kernel(n_heads, group, half, split, scale, x_ref, g_ref, *rest):
    if half:
        cos_ref, sa_ref, sb_ref = rest[:3]
        outs = rest[3:]
    else:
        outs = rest
    lane = lax.broadcasted_iota(I32, (1, LANES), 1)
    lo = lane < DIFF_QK
    g = g_ref[...]
    for h in range(n_heads):
        x = x_ref[:, h * LANES:(h + 1) * LANES].astype(F32)
        sq = x * x
        if group == LANES:
            ms = jnp.mean(sq, axis=-1, keepdims=True)
        else:
            s_lo = jnp.sum(jnp.where(lo, sq, 0.0), axis=-1, keepdims=True)
            s_hi = jnp.sum(jnp.where(lo, 0.0, sq), axis=-1, keepdims=True)
            ms = jnp.where(lo, s_lo, s_hi) * (1.0 / group)
        y = x * lax.rsqrt(ms + EPS) * g
        if half:
            y = (y * cos_ref[...] + pltpu.roll(y, LANES - half, 1) * sa_ref[...]
                 + pltpu.roll(y, half, 1) * sb_ref[...])
        if scale != 1.0:
            y = y * scale
        if split:
            outs[0][:, h * LANES:(h + 1) * LANES] = jnp.where(lo, y, 0.0).astype(outs[0].dtype)
            outs[1][:, h * LANES:(h + 1) * LANES] = jnp.where(lo, 0.0, y).astype(outs[1].dtype)
        else:
            outs[0][:, h * LANES:(h + 1) * LANES] = y.astype(outs[0].dtype)


def _prep(p, col0, n_heads, gain, group, rope, split=False, scale=1.0):
    t_all = p.shape[0]
    tr = _pick(t_all, (640, 256))
    hpb = _pick(math.gcd(col0 // LANES, n_heads), (8, 4, 2, 1))
    w = hpb * LANES
    cb = col0 // w
    g = jnp.tile(gain.astype(F32), LANES // gain.shape[0]).reshape(1, LANES)
    in_specs = [pl.BlockSpec((tr, w), lambda i, j: (i, cb + j)),
                pl.BlockSpec((1, LANES), lambda i, j: (0, 0))]
    args = [p, g]
    half = 0
    if rope is not None:
        half = rope[3]
        in_specs += [pl.BlockSpec((tr, LANES), lambda i, j: (i, 0))] * 3
        args += list(rope[:3])
    n_out = 2 if split else 1
    res = pl.pallas_call(
        functools.partial(_prep_kernel, hpb, group, half, split, scale),
        grid=(t_all // tr, n_heads // hpb),
        in_specs=in_specs,
        out_specs=[pl.BlockSpec((tr, w), lambda i, j: (i, j))] * n_out,
        out_shape=[jax.ShapeDtypeStruct((t_all, n_heads * LANES), MXU_DTYPE)] * n_out,
        compiler_params=_params(("arbitrary", "arbitrary")),
        name="qk_prep",
    )(*args)
    return res if split else res[0]


def _rope_tables(n_lat, t_all, seg):
    half = seg // 2
    t = jnp.arange(n_lat, dtype=I32)
    rows = (t // GRID_W).astype(F32)
    cols = (t % GRID_W).astype(F32)
    lane = jnp.arange(LANES)
    freqs = ROPE_BASE ** (-jnp.arange(0, seg, 2, dtype=F32) / seg)
    f_lane = freqs[lane % half]
    use_col = (lane // seg) % 2 == 1
    pos = jnp.where(use_col[None, :], cols[:, None], rows[:, None])
    ang = pos * f_lane[None, :]
    cos = jnp.cos(ang)
    sin = jnp.sin(ang)
    first = (lane % seg) < half
    sa = jnp.where(first[None, :], -sin, 0.0)
    sb = jnp.where(first[None, :], 0.0, sin)
    pad = t_all - n_lat
    cos = jnp.concatenate([cos, jnp.ones((pad, LANES), F32)], axis=0)
    sa = jnp.concatenate([sa, jnp.zeros((pad, LANES), F32)], axis=0)
    sb = jnp.concatenate([sb, jnp.zeros((pad, LANES), F32)], axis=0)
    return cos, sa, sb, half


def _diff_kernel(n_lat, t_all, tkc, lam_init, q1_ref, q2_ref, k_ref, v_ref, dl_ref, gs_ref, o_ref):
    tq = q1_ref.shape[0]
    qi = pl.program_id(1)
    is_ctx = qi * tq >= n_lat
    c0 = jnp.where(is_ctx, n_lat // tkc, 0)
    c1 = t_all // tkc
    q1 = q1_ref[...]
    q2 = q2_ref[...]

    def body(c, carry):
        m1, l1, a1, m2, l2, a2 = carry
        off = pl.multiple_of(c * tkc, tkc)
        kc = k_ref[pl.ds(off, tkc), :]
        vc = v_ref[pl.ds(off, tkc), :]

        def upd(q, m, l, a):
            s = _nt_dot(q, kc)
            mn = jnp.maximum(m, jnp.max(s, axis=-1, keepdims=True))
            alpha = jnp.exp(m - mn)
            p = jnp.exp(s - mn)
            l = alpha * l + jnp.sum(p, axis=-1, keepdims=True)
            a = alpha * a + _dot(p.astype(vc.dtype), vc)
            return mn, l, a

        m1, l1, a1 = upd(q1, m1, l1, a1)
        m2, l2, a2 = upd(q2, m2, l2, a2)
        return m1, l1, a1, m2, l2, a2

    minit = jnp.full((tq, 1), -jnp.inf, F32)
    linit = jnp.zeros((tq, 1), F32)
    ainit = jnp.zeros((tq, HEAD_DIM), F32)
    m1, l1, a1, m2, l2, a2 = lax.fori_loop(c0, c1, body, (minit, linit, ainit, minit, linit, ainit))
    dl = dl_ref[...]
    lam = (jnp.exp(jnp.sum(dl[0:1, :] * dl[1:2, :], axis=-1, keepdims=True))
           - jnp.exp(jnp.sum(dl[2:3, :] * dl[3:4, :], axis=-1, keepdims=True)) + lam_init)
    o = a1 / l1 - lam * (a2 / l2)
    ms = jnp.mean(o * o, axis=-1, keepdims=True)
    o = o * lax.rsqrt(ms + EPS) * gs_ref[...] * (1.0 - lam_init)
    o_ref[...] = o.astype(o_ref.dtype)


def _diff_attention(q1z, q2z, kd, p, v_col0, diff_lambda, subln, lam_init, n_lat, nq):
    t_all = kd.shape[0]
    tq = 256
    tkc = 256
    vb = v_col0 // LANES
    return pl.pallas_call(
        functools.partial(_diff_kernel, n_lat, t_all, tkc, lam_init),
        grid=(N_HEADS, nq // tq),
        in_specs=[pl.BlockSpec((tq, LANES), lambda h, i: (i, h)),
                  pl.BlockSpec((tq, LANES), lambda h, i: (i, h)),
                  pl.BlockSpec((t_all, LANES), lambda h, i: (0, h)),
                  pl.BlockSpec((t_all, LANES), lambda h, i: (0, vb + h)),
                  pl.BlockSpec((4, DIFF_QK), lambda h, i: (0, 0)),
                  pl.BlockSpec((1, HEAD_DIM), lambda h, i: (0, 0))],
        out_specs=pl.BlockSpec((tq, LANES), lambda h, i: (i, h)),
        out_shape=jax.ShapeDtypeStruct((nq, N_HEADS * HEAD_DIM), MXU_DTYPE),
        compiler_params=_params(("arbitrary", "arbitrary")),
        name="diff_attention",
    )(q1z, q2z, kd, p, diff_lambda.astype(F32), subln.astype(F32).reshape(1, HEAD_DIM))


def _swa_kernel(n_lat, t_all, q_ref, k_ref, v_ref, sink_ref, o_ref):
    w = SWA_BLOCK
    g = q_ref.shape[1] // HEAD_DIM
    b = pl.program_id(1)
    reach = jnp.where(b * w < n_lat, SWA_WINDOW, -1)
    scale = HEAD_DIM ** -0.5
    start = jnp.clip((b - 1) * w, 0, n_lat - 3 * w)
    start = pl.multiple_of(start, w)
    kw = k_ref[pl.ds(start, 3 * w), :]
    vw = v_ref[pl.ds(start, 3 * w), :]
    kc = k_ref[n_lat:t_all, :]
    vc = v_ref[n_lat:t_all, :]
    q = jnp.concatenate([q_ref[:, j * HEAD_DIM:(j + 1) * HEAD_DIM] for j in range(g)], axis=0)
    s_w = _nt_dot(q, kw) * scale
    s_c = _nt_dot(q, kc) * scale
    qpos = b * w + lax.broadcasted_iota(I32, (w, 3 * w), 0)
    kpos = start + lax.broadcasted_iota(I32, (w, 3 * w), 1)
    mask = jnp.abs(qpos - kpos) <= reach
    mask = jnp.concatenate([mask] * g, axis=0)
    s_w = jnp.where(mask, s_w, NEG)
    sink = jnp.concatenate([jnp.broadcast_to(sink_ref[0, :, j:j + 1], (w, 1)) for j in range(g)], axis=0)
    m = jnp.maximum(jnp.maximum(jnp.max(s_w, axis=-1, keepdims=True), jnp.max(s_c, axis=-1, keepdims=True)), sink)
    p_w = jnp.exp(s_w - m)
    p_c = jnp.exp(s_c - m)
    l = jnp.sum(p_w, axis=-1, keepdims=True) + jnp.sum(p_c, axis=-1, keepdims=True) + jnp.exp(sink - m)
    inv = 1.0 / l
    o = _dot((p_w * inv).astype(vw.dtype), vw) + _dot((p_c * inv).astype(vc.dtype), vc)
    for j in range(g):
        o_ref[:, j * HEAD_DIM:(j + 1) * HEAD_DIM] = o[j * w:(j + 1) * w, :].astype(o_ref.dtype)


def _swa_attention(qs, ks, p, v_col0, sink, n_lat, nq):
    t_all = ks.shape[0]
    g = N_HEADS // SWA_KV_HEADS
    vb = v_col0 // LANES
    sink3 = sink.astype(F32).reshape(SWA_KV_HEADS, 1, g)
    return pl.pallas_call(
        functools.partial(_swa_kernel, n_lat, t_all),
        grid=(SWA_KV_HEADS, nq // SWA_BLOCK),
        in_specs=[pl.BlockSpec((SWA_BLOCK, g * HEAD_DIM), lambda kv, b: (b, kv)),
                  pl.BlockSpec((t_all, LANES), lambda kv, b: (0, kv)),
                  pl.BlockSpec((t_all, LANES), lambda kv, b: (0, vb + kv)),
                  pl.BlockSpec((1, 1, g), lambda kv, b: (kv, 0, 0))],
        out_specs=pl.BlockSpec((SWA_BLOCK, g * HEAD_DIM), lambda kv, b: (b, kv)),
        out_shape=jax.ShapeDtypeStruct((nq, N_HEADS * HEAD_DIM), MXU_DTYPE),
        compiler_params=_params(("arbitrary", "arbitrary")),
        name="swa_attention",
    )(qs, ks, p, sink3)


NA_ROWS_PER_STEP = 4


def _na_kernel(n_lat, t_all, q_ref, k_ref, v_ref, bias_ref, o_ref):
    n_rows = n_lat // GRID_W
    span = NA_ROWS * GRID_W
    scale = HEAD_DIM ** -0.5
    kc = k_ref[n_lat:t_all, :]
    vc = v_ref[n_lat:t_all, :]
    for j in range(NA_ROWS_PER_STEP):
        r = pl.program_id(1) * NA_ROWS_PER_STEP + j
        is_lat = r < n_rows
        r0 = jnp.clip(r - NA_ROWS // 2, 0, n_rows - NA_ROWS)
        delta = jnp.where(is_lat, r0 - r + (NA_ROWS - 1), NA_ROWS)
        off = pl.multiple_of(r0 * GRID_W, GRID_W)
        kn = k_ref[pl.ds(off, span), :]
        vn = v_ref[pl.ds(off, span), :]
        q = q_ref[j * GRID_W:(j + 1) * GRID_W, :]
        s_n = _nt_dot(q, kn) * scale + bias_ref[0, delta]
        s_c = _nt_dot(q, kc) * scale
        m = jnp.maximum(jnp.max(s_n, axis=-1, keepdims=True), jnp.max(s_c, axis=-1, keepdims=True))
        p_n = jnp.exp(s_n - m)
        p_c = jnp.exp(s_c - m)
        inv = 1.0 / (jnp.sum(p_n, axis=-1, keepdims=True) + jnp.sum(p_c, axis=-1, keepdims=True))
        o = _dot((p_n * inv).astype(vn.dtype), vn) + _dot((p_c * inv).astype(vc.dtype), vc)
        o_ref[j * GRID_W:(j + 1) * GRID_W, :] = o.astype(o_ref.dtype)


def _na_bias_table(rpb):
    h = rpb.shape[0]
    c = jnp.arange(GRID_W)
    c0 = jnp.clip(c - NA_COLS // 2, 0, GRID_W - NA_COLS)
    kc = jnp.arange(GRID_W)
    inwin = (kc[None, :] >= c0[:, None]) & (kc[None, :] < c0[:, None] + NA_COLS)
    col_off = jnp.clip(kc[None, :] - c[:, None] + NA_COLS - 1, 0, 2 * NA_COLS - 2)
    toe = jnp.where(inwin[None, None], rpb.astype(F32)[:, :, col_off], NEG)
    rows = jnp.arange(NA_ROWS)[:, None] + jnp.arange(NA_ROWS)[None, :]
    tab = toe[:, rows]
    tab = tab.transpose(0, 1, 3, 2, 4).reshape(h, NA_ROWS, GRID_W, NA_ROWS * GRID_W)
    masked = jnp.full((h, 1, GRID_W, NA_ROWS * GRID_W), NEG, F32)
    return jnp.concatenate([tab, masked], axis=1)


def _na_attention(qn, kn, p, v_col0, bias, n_lat, nq):
    t_all = kn.shape[0]
    vb = v_col0 // LANES
    tq = NA_ROWS_PER_STEP * GRID_W
    return pl.pallas_call(
        functools.partial(_na_kernel, n_lat, t_all),
        grid=(N_HEADS, nq // tq),
        in_specs=[pl.BlockSpec((tq, LANES), lambda h, i: (i, h)),
                  pl.BlockSpec((t_all, LANES), lambda h, i: (0, h)),
                  pl.BlockSpec((t_all, LANES), lambda h, i: (0, vb + h)),
                  pl.BlockSpec((1, NA_ROWS + 1, GRID_W, NA_ROWS * GRID_W), lambda h, i: (h, 0, 0, 0))],
        out_specs=pl.BlockSpec((tq, LANES), lambda h, i: (i, h)),
        out_shape=jax.ShapeDtypeStruct((nq, N_HEADS * HEAD_DIM), MXU_DTYPE),
        compiler_params=_params(("arbitrary", "arbitrary")),
        name="na_attention",
    )(qn, kn, p, bias)


def _merge_kernel(h_ref, y0_ref, y1_ref, y2_ref, y3_ref, wg_ref, bg_ref, wb_ref, o_ref):
    h = h_ref[...]
    acc = None
    for i, y_ref in enumerate((y0_ref, y1_ref, y2_ref, y3_ref)):
        gate = _sigmoid(_dot(h, wg_ref[i]) + bg_ref[i])
        term = gate * _dot(y_ref[...], wb_ref[i])
        acc = term if acc is None else acc + term
    o_ref[...] = acc.astype(o_ref.dtype)


def _merge(h, ys, wg, bg, wb, m_rows):
    d = h.shape[1]
    bw = ys[0].shape[1]
    tm = _pick(m_rows, (1280, 1024, 640, 512, 256))
    tn = 256
    once = pl.Buffered(1)
    return pl.pallas_call(
        _merge_kernel,
        grid=(m_rows // tm, d // tn),
        in_specs=[pl.BlockSpec((tm, d), lambda i, j: (i, 0), pipeline_mode=once)]
        + [pl.BlockSpec((tm, bw), lambda i, j: (i, 0), pipeline_mode=once)] * 4
        + [pl.BlockSpec((4, d, tn), lambda i, j: (0, 0, j)),
           pl.BlockSpec((4, 1, tn), lambda i, j: (0, 0, j)),
           pl.BlockSpec((4, bw, tn), lambda i, j: (0, 0, j))],
        out_specs=pl.BlockSpec((tm, tn), lambda i, j: (i, j)),
        out_shape=jax.ShapeDtypeStruct((m_rows, d), MXU_DTYPE),
        compiler_params=_params(("arbitrary", "arbitrary"), vmem_mb=56),
        name="merge",
    )(h, *ys, wg, bg.astype(F32).reshape(4, 1, d), wb)


def _router_kernel(h_ref, w_ref, b_ref, sel_ref, idx_ref, wk_ref):
    tr = h_ref.shape[0]
    per = N_EXPERTS // N_GROUPS
    scores = _sigmoid(_nt_dot(w_ref[...], h_ref[...]))
    choice = scores + b_ref[...]
    sub = lax.broadcasted_iota(I32, (per, tr), 0)
    neg_inf = -jnp.inf
    slabs = [choice[g * per:(g + 1) * per, :] for g in range(N_GROUPS)]
    grp = []
    for c in slabs:
        m1 = jnp.max(c, axis=0, keepdims=True)
        i1 = jnp.min(jnp.where(c == m1, sub, per), axis=0, keepdims=True)
        m2 = jnp.max(jnp.where(sub == i1, neg_inf, c), axis=0, keepdims=True)
        grp.append(m1 + m2)
    gsel = [jnp.zeros((1, tr), jnp.bool_) for _ in range(N_GROUPS)]
    for _ in range(TOPK_GROUPS):
        gm = functools.reduce(jnp.maximum, grp)
        found = jnp.zeros((1, tr), jnp.bool_)
        for g in range(N_GROUPS):
            hit = (grp[g] == gm) & jnp.logical_not(found)
            found = found | hit
            gsel[g] = gsel[g] | hit
            grp[g] = jnp.where(hit, neg_inf, grp[g])
    vals = [jnp.where(gsel[g], slabs[g], neg_inf) for g in range(N_GROUPS)]
    eidx = [sub + g * per for g in range(N_GROUPS)]
    sel = [jnp.zeros((per, tr), jnp.bool_) for _ in range(N_GROUPS)]
    row8 = lax.broadcasted_iota(I32, (TOP_K, tr), 0)
    idx_out = jnp.zeros((TOP_K, tr), I32)
    w_out = jnp.zeros((TOP_K, tr), F32)
    wsum = jnp.zeros((1, tr), F32)
    for k in range(TOP_K):
        vm = jnp.max(functools.reduce(jnp.maximum, vals), axis=0, keepdims=True)
        cand = [jnp.where(vals[g] == vm, eidx[g], N_EXPERTS) for g in range(N_GROUPS)]
        ik = jnp.min(functools.reduce(jnp.minimum, cand), axis=0, keepdims=True)
        wk = jnp.zeros((1, tr), F32)
        for g in range(N_GROUPS):
            hit = eidx[g] == ik
            sel[g] = sel[g] | hit
            vals[g] = jnp.where(hit, neg_inf, vals[g])
            sc = scores[g * per:(g + 1) * per, :]
            wk = wk + jnp.sum(jnp.where(hit, sc, 0.0), axis=0, keepdims=True)
        idx_out = jnp.where(row8 == k, ik, idx_out)
        w_out = jnp.where(row8 == k, wk, w_out)
        wsum = wsum + wk
    for g in range(N_GROUPS):
        sel_ref[g * per:(g + 1) * per, :] = sel[g].astype(I32)
    idx_ref[...] = idx_out
    wk_ref[...] = w_out / wsum * ROUTED_SCALE


def _router(h2, w_rt, b_r, m_rows):
    d = h2.shape[1]
    tr = 256
    return pl.pallas_call(
        _router_kernel,
        grid=(m_rows // tr,),
        in_specs=[pl.BlockSpec((tr, d), lambda i: (i, 0)),
                  pl.BlockSpec((N_EXPERTS, d), lambda i: (0, 0)),
                  pl.BlockSpec((N_EXPERTS, 1), lambda i: (0, 0))],
        out_specs=[pl.BlockSpec((N_EXPERTS, tr), lambda i: (0, i)),
                   pl.BlockSpec((TOP_K, tr), lambda i: (0, i)),
                   pl.BlockSpec((TOP_K, tr), lambda i: (0, i))],
        out_shape=[jax.ShapeDtypeStruct((N_EXPERTS, m_rows), I32),
                   jax.ShapeDtypeStruct((TOP_K, m_rows), I32),
                   jax.ShapeDtypeStruct((TOP_K, m_rows), F32)],
        compiler_params=_params(("arbitrary",)),
        name="router",
    )(h2, w_rt, b_r.astype(F32).reshape(N_EXPERTS, 1))


def _row_gather(tok_ref, n_rows, src_hbm, dst_buf, sem, start):
    def body(r, carry):
        t = tok_ref[0, 0, r] if start else 0
        cp = pltpu.make_async_copy(src_hbm.at[pl.ds(t, 1), :], dst_buf.at[pl.ds(r, 1), :], sem)
        if start:
            cp.start()
        else:
            cp.wait()
        return carry
    lax.fori_loop(0, n_rows, body, 0, unroll=8)


def _expert_kernel(nused_ref, blke_ref, tok_ref, tokn_ref, hp_ref, wgu_ref, wd_ref, sw_ref, y_ref, xbuf, sem):
    del blke_ref
    i = pl.program_id(0)
    nu = nused_ref[0]
    tm = xbuf.shape[1]
    half = xbuf.shape[2]
    slot = i % 2

    @pl.when(i == 0)
    def _():
        _row_gather(tok_ref, tm, hp_ref, xbuf.at[0], sem.at[0], True)

    @pl.when(i + 1 < nu)
    def _():
        _row_gather(tokn_ref, tm, hp_ref, xbuf.at[1 - slot], sem.at[1 - slot], True)

    @pl.when(i < nu)
    def _():
        _row_gather(tok_ref, tm, hp_ref, xbuf.at[slot], sem.at[slot], False)
        lo, hi = _unpack_halves(xbuf[slot])
        gu = (_dot(lo.astype(MXU_DTYPE), wgu_ref[0, :half, :])
              + _dot(hi.astype(MXU_DTYPE), wgu_ref[0, half:, :]))
        g = gu[:, :EXPERT_DIM]
        u = gu[:, EXPERT_DIM:]
        a = (g * _sigmoid(g) * u).astype(MXU_DTYPE)
        y = _dot(a, wd_ref[0]) * sw_ref[...]
        y_ref[...] = _pack_halves(y)

    @pl.when(i >= nu)
    def _():
        y_ref[...] = jnp.zeros_like(y_ref)


def _experts(hp, wgu, wd, n_used, blk_e, slot_tok, slot_w, n_blk):
    d2 = hp.shape[1]
    d = 2 * d2
    tm = MOE_TILE
    tok3 = slot_tok.reshape(n_blk, 1, tm)
    last = n_blk - 1
    grid_spec = pltpu.PrefetchScalarGridSpec(
        num_scalar_prefetch=2,
        grid=(n_blk,),
        in_specs=[pl.BlockSpec((1, 1, tm), lambda i, nu, be: (i, 0, 0), memory_space=pltpu.SMEM),
                  pl.BlockSpec((1, 1, tm), lambda i, nu, be: (jnp.minimum(i + 1, last), 0, 0),
                               memory_space=pltpu.SMEM),
                  pl.BlockSpec(memory_space=pl.ANY),
                  pl.BlockSpec((1, d, 2 * EXPERT_DIM), lambda i, nu, be: (be[i], 0, 0)),
                  pl.BlockSpec((1, EXPERT_DIM, d), lambda i, nu, be: (be[i], 0, 0)),
                  pl.BlockSpec((tm, 1), lambda i, nu, be: (i, 0))],
        out_specs=pl.BlockSpec((tm, d2), lambda i, nu, be: (i, 0)),
        scratch_shapes=[pltpu.VMEM((2, tm, d2), U32), pltpu.SemaphoreType.DMA((2,))],
    )
    return pl.pallas_call(
        _expert_kernel,
        grid_spec=grid_spec,
        out_shape=jax.ShapeDtypeStruct((n_blk * tm, d2), U32),
        compiler_params=_params(("arbitrary",)),
        name="routed_experts",
    )(n_used, blk_e, tok3, tok3, hp, wgu, wd, slot_w.reshape(n_blk * tm, 1))


def _shared_kernel(h_ref, wgu_ref, wd_ref, o_ref):
    gu = _dot(h_ref[...], wgu_ref[...])
    hd = gu.shape[1] // 2
    g = gu[:, :hd]
    u = gu[:, hd:]
    a = (g * _sigmoid(g) * u).astype(MXU_DTYPE)
    o_ref[...] = _dot(a, wd_ref[...]).astype(o_ref.dtype)


def _shared_expert(h2, wgu, wd, m_rows):
    d = h2.shape[1]
    tm = 256
    return pl.pallas_call(
        _shared_kernel,
        grid=(m_rows // tm,),
        in_specs=[pl.BlockSpec((tm, d), lambda i: (i, 0)),
                  pl.BlockSpec(wgu.shape, lambda i: (0, 0)),
                  pl.BlockSpec(wd.shape, lambda i: (0, 0))],
        out_specs=pl.BlockSpec((tm, d), lambda i: (i, 0)),
        out_shape=jax.ShapeDtypeStruct((m_rows, d), MXU_DTYPE),
        compiler_params=_params(("arbitrary",)),
        name="shared_expert",
    )(h2, wgu, wd)


def _combine_kernel(n_lat, sl_ref, sln_ref, y_hbm, x_ref, sh_ref, g_ref, o_ref, buf, sem):
    i = pl.program_id(0)
    nsteps = pl.num_programs(0)
    tc = x_ref.shape[0]
    nrow = buf.shape[1]
    slot = i % 2

    @pl.when(i == 0)
    def _():
        _row_gather(sl_ref, nrow, y_hbm, buf.at[0], sem.at[0], True)

    @pl.when(i + 1 < nsteps)
    def _():
        _row_gather(sln_ref, nrow, y_hbm, buf.at[1 - slot], sem.at[1 - slot], True)

    _row_gather(sl_ref, nrow, y_hbm, buf.at[slot], sem.at[slot], False)
    lo_acc = None
    hi_acc = None
    for k in range(TOP_K):
        lo, hi = _unpack_halves(buf[slot, k * tc:(k + 1) * tc, :])
        lo_acc = lo if lo_acc is None else lo_acc + lo
        hi_acc = hi if hi_acc is None else hi_acc + hi
    routed = jnp.concatenate([lo_acc, hi_acc], axis=1)
    rows = i * tc + lax.broadcasted_iota(I32, (tc, 1), 0)
    g = jnp.where(rows >= n_lat, g_ref[1:2, :], g_ref[0:1, :])
    o_ref[...] = x_ref[...] + g * (routed + sh_ref[...].astype(F32))


def _combine(y_slots, slots, x1, sh, mod, gate_blk, n_lat, m_rows):
    d = x1.shape[1]
    tc = COMBINE_TILE
    nsteps = m_rows // tc
    last = nsteps - 1
    return pl.pallas_call(
        functools.partial(_combine_kernel, n_lat),
        grid=(nsteps,),
        in_specs=[pl.BlockSpec((1, 1, TOP_K * tc), lambda i: (i, 0, 0), memory_space=pltpu.SMEM),
                  pl.BlockSpec((1, 1, TOP_K * tc), lambda i: (jnp.minimum(i + 1, last), 0, 0),
                               memory_space=pltpu.SMEM),
                  pl.BlockSpec(memory_space=pl.ANY),
                  pl.BlockSpec((tc, d), lambda i: (i, 0)),
                  pl.BlockSpec((tc, d), lambda i: (i, 0)),
                  pl.BlockSpec((2, d), lambda i: (0, gate_blk))],
        out_specs=pl.BlockSpec((tc, d), lambda i: (i, 0)),
        out_shape=jax.ShapeDtypeStruct((m_rows, d), F32),
        scratch_shapes=[pltpu.VMEM((2, TOP_K * tc, d // 2), U32), pltpu.SemaphoreType.DMA((2,))],
        compiler_params=_params(("arbitrary",)),
        name="moe_combine",
    )(slots, slots, y_slots, x1, sh, mod)


def _dispatch_tables(sel, idx, wk, m_rows):
    tm = MOE_TILE
    tk = m_rows * TOP_K
    n_blk = (tk + tm - 1) // tm + N_EXPERTS
    cs = jnp.cumsum(sel, axis=1)
    counts = cs[:, -1]
    padded = ((counts + tm - 1) // tm) * tm
    pend = jnp.cumsum(padded)
    pstart = pend - padded
    dest_et = pstart[:, None] + cs - 1
    dest = jnp.take_along_axis(dest_et, idx, axis=0)
    tok = jnp.broadcast_to(jnp.arange(m_rows, dtype=I32)[None, :], dest.shape)
    slot_tok = jnp.zeros((n_blk * tm,), I32).at[dest.reshape(-1)].set(tok.reshape(-1))
    slot_w = jnp.zeros((n_blk * tm,), F32).at[dest.reshape(-1)].set(wk.reshape(-1))
    blk_e = jnp.clip(jnp.searchsorted(pend, jnp.arange(n_blk, dtype=I32) * tm, side="right"),
                     0, N_EXPERTS - 1).astype(I32)
    n_used = (pend[-1] // tm).astype(I32).reshape(1)
    tc = COMBINE_TILE
    slots = dest.reshape(TOP_K, m_rows // tc, tc).transpose(1, 0, 2).reshape(m_rows // tc, 1, TOP_K * tc)
    return n_used, blk_e, slot_tok, slot_w, slots.astype(I32), n_blk


def kernel(x, c, ctx, c_ctx, w_mod, b_mod, g_norm1, g_norm2, w_in, conv_w, diff_qnorm, diff_knorm, diff_lambda, diff_subln, swa_qnorm, swa_knorm, swa_sink, na_qnorm, na_knorm, na_rpb, w_gate, b_gate, w_branch, w_out, w_router, b_router, w_exp_gate, w_exp_up, w_exp_down, w_sh_gate, w_sh_up, w_sh_down):
    bsz, n, d = x.shape
    assert bsz == 1
    n_ctx = ctx.shape[1]
    t_all = n + n_ctx
    depth = w_in.shape[0]
    bw = d // 4
    cdt = MXU_DTYPE

    xa = jnp.concatenate([x[0], ctx[0]], axis=0)
    cc = jnp.stack([c[0], c_ctx], axis=0).astype(F32)
    rope_d = _rope_tables(n, t_all, DIFF_QK // 2)
    rope_s = _rope_tables(n, t_all, HEAD_DIM // 2)

    o_conv = 0
    o_dq, o_dk, o_dv = 3 * bw, 4 * bw, 5 * bw
    o_sq = 6 * bw
    o_sk = o_sq + bw
    o_sv = o_sk + SWA_KV_HEADS * HEAD_DIM
    o_nq = o_sv + SWA_KV_HEADS * HEAD_DIM
    o_nk, o_nv = o_nq + bw, o_nq + 2 * bw

    for l in range(depth):
        update_ctx = l < depth - 1
        nq = t_all if update_ctx else n
        lam_init = 0.8 - 0.6 * math.exp(-0.3 * l)
        mod = _modvec(cc, w_mod[l], b_mod[l])

        h = _rmsmod(xa, g_norm1[l], mod, 0, 1, n, t_all)
        p = _matmul(h, w_in[l].astype(cdt), cdt, t_all)

        a = _short_conv(p, conv_w[l].astype(F32), n, nq)
        q1z, q2z = _prep(p, o_dq, N_HEADS, diff_qnorm[l], DIFF_QK, rope_d, split=True, scale=DIFF_QK ** -0.5)
        kd = _prep(p, o_dk, N_HEADS, diff_knorm[l], DIFF_QK, rope_d)
        qs = _prep(p, o_sq, N_HEADS, swa_qnorm[l], HEAD_DIM, rope_s)
        ks = _prep(p, o_sk, SWA_KV_HEADS, swa_knorm[l], HEAD_DIM, rope_s)
        qn = _prep(p, o_nq, N_HEADS, na_qnorm[l], HEAD_DIM, None)
        kn = _prep(p, o_nk, N_HEADS, na_knorm[l], HEAD_DIM, None)

        yb = _diff_attention(q1z, q2z, kd, p, o_dv, diff_lambda[l], diff_subln[l], lam_init, n, nq)
        ys = _swa_attention(qs, ks, p, o_sv, swa_sink[l], n, nq)
        yn = _na_attention(qn, kn, p, o_nv, _na_bias_table(na_rpb[l]), n, nq)

        acc = _merge(h, (a, yb, ys, yn), w_gate[l].astype(cdt), b_gate[l], w_branch[l].astype(cdt), nq)
        x1 = _matmul_residual(acc, w_out[l].astype(cdt), xa, mod, 2, n, nq)

        h2, h2p = _rmsmod(x1, g_norm2[l], mod, 3, 4, n, nq, packed=True)
        sel, idx, wk = _router(h2, w_router[l].T.astype(cdt), b_router[l], nq)
        n_used, blk_e, slot_tok, slot_w, slots, n_blk = _dispatch_tables(sel, idx, wk, nq)
        wgu = jnp.concatenate([w_exp_gate[l], w_exp_up[l]], axis=-1).astype(cdt)
        y_slots = _experts(h2p, wgu, w_exp_down[l].astype(cdt), n_used, blk_e, slot_tok, slot_w, n_blk)
        wsh = jnp.concatenate([w_sh_gate[l], w_sh_up[l]], axis=-1).astype(cdt)
        sh = _shared_expert(h2, wsh, w_sh_down[l].astype(cdt), nq)
        xa = _combine(y_slots, slots, x1, sh, mod, 5, n, nq)

    return xa[:n].reshape(bsz, n, d)
```

```python
import functools
import math

import jax
import jax.numpy as jnp
from jax import lax
from jax.experimental import pallas as pl
from jax.experimental.pallas import tpu as pltpu

GRID_W = 64
HEAD_DIM = 128
DIFF_QK = 64
SWA_WINDOW = 128
SWA_BLOCK = 128
SWA_KV_HEADS = 2
N_HEADS = 8
NA_ROWS = 8
NA_COLS = 16
ROPE_BASE = 10000.0
N_EXPERTS = 64
N_GROUPS = 8
TOPK_GROUPS = 4
TOP_K = 8
EXPERT_DIM = 256
ROUTED_SCALE = 2.5
EPS = 1e-6
NEG = -1e30

LANES = 128
SUBLANES = 8
V7X_VMEM_BYTES = 64 * 1024 * 1024

MXU_DTYPE = jnp.bfloat16
MOE_TILE = 256
COMBINE_TILE = 64

F32 = jnp.float32
I32 = jnp.int32
U32 = jnp.uint32


def _params(sem, vmem_mb=48):
    return pltpu.CompilerParams(dimension_semantics=sem, vmem_limit_bytes=vmem_mb * 1024 * 1024)


def _pick(m, cands):
    for c in cands:
        if m % c == 0:
            return c
    raise ValueError(f"no tile for {m} in {cands}")


def _nt_dot(a, b):
    return lax.dot_general(a, b, (((1,), (1,)), ((), ())), preferred_element_type=F32)


def _dot(a, b):
    return jnp.dot(a, b, preferred_element_type=F32)


def _sigmoid(x):
    return 1.0 / (1.0 + jnp.exp(-x))


def _modvec_kernel(s_ref, w_ref, b_ref, o_ref):
    d, tn = w_ref.shape
    rows = []
    for r in range(2):
        c = s_ref[r]
        s = c * _sigmoid(c)
        parts = []
        for j in range(tn // LANES):
            prod = w_ref[:, j * LANES:(j + 1) * LANES] * s
            part = jnp.sum(prod.reshape(d // SUBLANES, SUBLANES, LANES), axis=0)
            parts.append(jnp.sum(part, axis=0, keepdims=True))
        rows.append(jnp.concatenate(parts, axis=1))
    is_first = lax.broadcasted_iota(I32, (2, tn), 0) == 0
    o_ref[...] = jnp.where(is_first, rows[0], rows[1]) + b_ref[...]


def _modvec(cc, w, b):
    d, n = w.shape
    tn = 512
    s_rep = jnp.broadcast_to(cc[:, :, None], (2, d, LANES))
    return pl.pallas_call(
        _modvec_kernel,
        grid=(n // tn,),
        in_specs=[pl.BlockSpec((2, d, LANES), lambda j: (0, 0, 0)),
                  pl.BlockSpec((d, tn), lambda j: (0, j)),
                  pl.BlockSpec((1, tn), lambda j: (0, j))],
        out_specs=pl.BlockSpec((2, tn), lambda j: (0, j)),
        out_shape=jax.ShapeDtypeStruct((2, n), F32),
        compiler_params=_params(("arbitrary",)),
        name="modvec",
    )(s_rep, w, b.reshape(1, n))


def _pack_halves(v):
    c = v.shape[1] // 2
    bits = lax.bitcast_convert_type(v.astype(jnp.bfloat16).astype(F32), U32)
    return (bits[:, c:] & jnp.uint32(0xFFFF0000)) | (bits[:, :c] >> 16)


def _unpack_halves(w):
    lo = lax.bitcast_convert_type(w << 16, F32)
    hi = lax.bitcast_convert_type(w & jnp.uint32(0xFFFF0000), F32)
    return lo, hi


def _rmsmod_kernel(n_lat, packed, x_ref, g_ref, sh_ref, sc_ref, o_ref, *p_ref):
    tr = x_ref.shape[0]
    x = x_ref[...]
    ms = jnp.mean(x * x, axis=-1, keepdims=True)
    y = x * lax.rsqrt(ms + EPS) * g_ref[...]
    rows = pl.program_id(0) * tr + lax.broadcasted_iota(I32, (tr, 1), 0)
    is_ctx = rows >= n_lat
    sh = jnp.where(is_ctx, sh_ref[1:2, :], sh_ref[0:1, :])
    sc = jnp.where(is_ctx, sc_ref[1:2, :], sc_ref[0:1, :])
    h = y * (1.0 + sc) + sh
    o_ref[...] = h.astype(o_ref.dtype)
    if packed:
        p_ref[0][...] = _pack_halves(h)


def _rmsmod(x, g, mod, shift_blk, scale_blk, n_lat, m_rows, packed=False):
    d = x.shape[1]
    tr = 256
    out_shape = [jax.ShapeDtypeStruct((m_rows, d), MXU_DTYPE)]
    out_specs = [pl.BlockSpec((tr, d), lambda i: (i, 0))]
    if packed:
        out_shape.append(jax.ShapeDtypeStruct((m_rows, d // 2), U32))
        out_specs.append(pl.BlockSpec((tr, d // 2), lambda i: (i, 0)))
    res = pl.pallas_call(
        functools.partial(_rmsmod_kernel, n_lat, packed),
        grid=(m_rows // tr,),
        in_specs=[pl.BlockSpec((tr, d), lambda i: (i, 0)),
                  pl.BlockSpec((1, d), lambda i: (0, 0)),
                  pl.BlockSpec((2, d), lambda i: (0, shift_blk)),
                  pl.BlockSpec((2, d), lambda i: (0, scale_blk))],
        out_specs=out_specs,
        out_shape=out_shape,
        compiler_params=_params(("arbitrary",)),
        name="rmsmod",
    )(x, g.reshape(1, d), mod, mod)
    return res if packed else res[0]


def _mm_kernel(a_ref, w_ref, o_ref):
    o_ref[...] = _dot(a_ref[...], w_ref[...]).astype(o_ref.dtype)


def _mm_res_kernel(n_lat, a_ref, w_ref, r_ref, g_ref, o_ref):
    tm = a_ref.shape[0]
    acc = _dot(a_ref[...], w_ref[...])
    rows = pl.program_id(0) * tm + lax.broadcasted_iota(I32, (tm, 1), 0)
    g = jnp.where(rows >= n_lat, g_ref[1:2, :], g_ref[0:1, :])
    o_ref[...] = r_ref[...] + g * acc


def _matmul(a, w, out_dtype, m_rows):
    k = a.shape[1]
    n = w.shape[1]
    tm = _pick(m_rows, (1280, 1024, 640, 512, 256))
    tn = 512
    return pl.pallas_call(
        _mm_kernel,
        grid=(m_rows // tm, n // tn),
        in_specs=[pl.BlockSpec((tm, k), lambda i, j: (i, 0)),
                  pl.BlockSpec((k, tn), lambda i, j: (0, j))],
        out_specs=pl.BlockSpec((tm, tn), lambda i, j: (i, j)),
        out_shape=jax.ShapeDtypeStruct((m_rows, n), out_dtype),
        compiler_params=_params(("arbitrary", "arbitrary")),
        name="matmul",
    )(a, w)


def _matmul_residual(a, w, res, mod, gate_blk, n_lat, m_rows):
    k = a.shape[1]
    n = w.shape[1]
    tm = _pick(m_rows, (1280, 1024, 640, 512, 256))
    tn = 512
    nb = n // tn
    return pl.pallas_call(
        functools.partial(_mm_res_kernel, n_lat),
        grid=(m_rows // tm, nb),
        in_specs=[pl.BlockSpec((tm, k), lambda i, j: (i, 0)),
                  pl.BlockSpec((k, tn), lambda i, j: (0, j)),
                  pl.BlockSpec((tm, tn), lambda i, j: (i, j)),
                  pl.BlockSpec((2, tn), lambda i, j: (0, gate_blk * nb + j))],
        out_specs=pl.BlockSpec((tm, tn), lambda i, j: (i, j)),
        out_shape=jax.ShapeDtypeStruct((m_rows, n), F32),
        compiler_params=_params(("arbitrary", "arbitrary")),
        name="matmul_residual",
    )(a, w, res, mod)


def _conv_kernel(n_lat, t_all, b_ref, c_ref, x_ref, cp_ref, xp_ref, cn_ref, xn_ref, w_ref, o_ref):
    tr = b_ref.shape[0]
    u = c_ref[...].astype(F32) * x_ref[...].astype(F32)
    u_prev_row = cp_ref[SUBLANES - 1:SUBLANES, :].astype(F32) * xp_ref[SUBLANES - 1:SUBLANES, :].astype(F32)
    u_next_row = cn_ref[0:1, :].astype(F32) * xn_ref[0:1, :].astype(F32)
    r = lax.broadcasted_iota(I32, (tr, 1), 0)
    tok = pl.program_id(0) * tr + r
    up = jnp.where(r == 0, u_prev_row, pltpu.roll(u, 1, 0))
    un = jnp.where(r == tr - 1, u_next_row, pltpu.roll(u, tr - 1, 0))
    up = jnp.where((tok == 0) | (tok == n_lat), 0.0, up)
    un = jnp.where((tok == n_lat - 1) | (tok == t_all - 1), 0.0, un)
    y = w_ref[0:1, :] * up + w_ref[1:2, :] * u + w_ref[2:3, :] * un
    o_ref[...] = (b_ref[...].astype(F32) * y).astype(o_ref.dtype)


def _short_conv(p, conv_w, n_lat, m_rows):
    t_all = p.shape[0]
    cw = conv_w.shape[1]
    tr = 256
    tc = 512
    nc = cw // tc
    rb = tr // SUBLANES
    last = t_all // SUBLANES - 1
    return pl.pallas_call(
        functools.partial(_conv_kernel, n_lat, t_all),
        grid=(m_rows // tr, nc),
        in_specs=[pl.BlockSpec((tr, tc), lambda i, j: (i, j)),
                  pl.BlockSpec((tr, tc), lambda i, j: (i, nc + j)),
                  pl.BlockSpec((tr, tc), lambda i, j: (i, 2 * nc + j)),
                  pl.BlockSpec((SUBLANES, tc), lambda i, j: (jnp.maximum(i * rb - 1, 0), nc + j)),
                  pl.BlockSpec((SUBLANES, tc), lambda i, j: (jnp.maximum(i * rb - 1, 0), 2 * nc + j)),
                  pl.BlockSpec((SUBLANES, tc), lambda i, j: (jnp.minimum((i + 1) * rb, last), nc + j)),
                  pl.BlockSpec((SUBLANES, tc), lambda i, j: (jnp.minimum((i + 1) * rb, last), 2 * nc + j)),
                  pl.BlockSpec((3, tc), lambda i, j: (0, j))],
        out_specs=pl.BlockSpec((tr, tc), lambda i, j: (i, j)),
        out_shape=jax.ShapeDtypeStruct((m_rows, cw), MXU_DTYPE),
        compiler_params=_params(("arbitrary", "arbitrary")),
        name="short_conv",
    )(p, p, p, p, p, p, p, conv_w)


def _prep_kernel(n_heads, group, half, split, scale, x_ref, g_ref, *rest):
    if half:
        cos_ref, sa_ref, sb_ref = rest[:3]
        outs = rest[3:]
    else:
        outs = rest
    lane = lax.broadcasted_iota(I32, (1, LANES), 1)
    lo = lane < DIFF_QK
    g = g_ref[...]
    for h in range(n_heads):
        x = x_ref[:, h * LANES:(h + 1) * LANES].astype(F32)
        sq = x * x
        if group == LANES:
            ms = jnp.mean(sq, axis=-1, keepdims=True)
        else:
            s_lo = jnp.sum(jnp.where(lo, sq, 0.0), axis=-1, keepdims=True)
            s_hi = jnp.sum(jnp.where(lo, 0.0, sq), axis=-1, keepdims=True)
            ms = jnp.where(lo, s_lo, s_hi) * (1.0 / group)
        y = x * lax.rsqrt(ms + EPS) * g
        if half:
            y = (y * cos_ref[...] + pltpu.roll(y, LANES - half, 1) * sa_ref[...]
                 + pltpu.roll(y, half, 1) * sb_ref[...])
        if scale != 1.0:
            y = y * scale
        if split:
            outs[0][:, h * LANES:(h + 1) * LANES] = jnp.where(lo, y, 0.0).astype(outs[0].dtype)
            outs[1][:, h * LANES:(h + 1) * LANES] = jnp.where(lo, 0.0, y).astype(outs[1].dtype)
        else:
            outs[0][:, h * LANES:(h + 1) * LANES] = y.astype(outs[0].dtype)


def _prep(p, col0, n_heads, gain, group, rope, split=False, scale=1.0):
    t_all = p.shape[0]
    tr = _pick(t_all, (640, 256))
    hpb = _pick(math.gcd(col0 // LANES, n_heads), (8, 4, 2, 1))
    w = hpb * LANES
    cb = col0 // w
    g = jnp.tile(gain.astype(F32), LANES // gain.shape[0]).reshape(1, LANES)
    in_specs = [pl.BlockSpec((tr, w), lambda i, j: (i, cb + j)),
                pl.BlockSpec((1, LANES), lambda i, j: (0, 0))]
    args = [p, g]
    half = 0
    if rope is not None:
        half = rope[3]
        in_specs += [pl.BlockSpec((tr, LANES), lambda i, j: (i, 0))] * 3
        args += list(rope[:3])
    n_out = 2 if split else 1
    res = pl.pallas_call(
        functools.partial(_prep_kernel, hpb, group, half, split, scale),
        grid=(t_all // tr, n_heads // hpb),
        in_specs=in_specs,
        out_specs=[pl.BlockSpec((tr, w), lambda i, j: (i, j))] * n_out,
        out_shape=[jax.ShapeDtypeStruct((t_all, n_heads * LANES), MXU_DTYPE)] * n_out,
        compiler_params=_params(("arbitrary", "arbitrary")),
        name="qk_prep",
    )(*args)
    return res if split else res[0]


def _rope_tables(n_lat, t_all, seg):
    half = seg // 2
    n_rows = n_lat // GRID_W
    lane = jnp.arange(LANES)
    freqs = ROPE_BASE ** (-jnp.arange(0, seg, 2, dtype=F32) / seg)
    f_lane = freqs[lane % half]
    use_col = ((lane // seg) % 2 == 1)[None, None, :]
    ang_r = jnp.arange(n_rows, dtype=F32)[:, None] * f_lane[None, :]
    ang_c = jnp.arange(GRID_W, dtype=F32)[:, None] * f_lane[None, :]

    def expand(fn):
        full = jnp.where(use_col, fn(ang_c)[None, :, :], fn(ang_r)[:, None, :])
        return full.reshape(n_lat, LANES)

    cos = expand(jnp.cos)
    sin = expand(jnp.sin)
    first = (lane % seg) < half
    sa = jnp.where(first[None, :], -sin, 0.0)
    sb = jnp.where(first[None, :], 0.0, sin)
    pad = t_all - n_lat
    cos = jnp.concatenate([cos, jnp.ones((pad, LANES), F32)], axis=0)
    sa = jnp.concatenate([sa, jnp.zeros((pad, LANES), F32)], axis=0)
    sb = jnp.concatenate([sb, jnp.zeros((pad, LANES), F32)], axis=0)
    return cos, sa, sb, half


DIFF_TQ = 512
DIFF_TK = 512


def _diff_kernel(n_pairs, tkc, lam_init, q1_ref, q2_ref, k_ref, vx_ref, dl_ref, gs_ref, o_ref, s_scr):
    tq = q1_ref.shape[0]
    t_all = k_ref.shape[0]
    n_lat = 2 * n_pairs * tkc
    q1 = q1_ref[...]
    q2 = q2_ref[...]

    def scores(kc):
        return _nt_dot(q1, kc), _nt_dot(q2, kc)

    def fold(s, vc, m, a):
        mn = jnp.maximum(m, jnp.max(s, axis=-1, keepdims=True))
        alpha = jnp.exp2(m - mn)
        p = jnp.exp2(s - mn).astype(vc.dtype)
        return mn, alpha * a + _dot(p, vc)

    def lat_scores_to(buf, c):
        off = pl.multiple_of(c * tkc, tkc)
        s1, s2 = scores(k_ref[pl.ds(off, tkc), :])
        s_scr[buf, 0] = s1
        s_scr[buf, 1] = s2

    def lat_fold_from(buf, c, carry):
        m1, a1, m2, a2 = carry
        off = pl.multiple_of(c * tkc, tkc)
        vc = vx_ref[pl.ds(off, tkc), :]
        m1, a1 = fold(s_scr[buf, 0], vc, m1, a1)
        m2, a2 = fold(s_scr[buf, 1], vc, m2, a2)
        return m1, a1, m2, a2

    minit = jnp.full((tq, 1), -jnp.inf, F32)
    ainit = jnp.zeros((tq, 2 * HEAD_DIM), F32)
    carry = (minit, ainit, minit, ainit)
    if n_pairs > 0:
        lat_scores_to(0, 0)

        def pair(j, carry):
            lat_scores_to(1, 2 * j + 1)
            carry = lat_fold_from(0, 2 * j, carry)
            lat_scores_to(0, 2 * j + 2)
            return lat_fold_from(1, 2 * j + 1, carry)

        carry = lax.fori_loop(0, n_pairs - 1, pair, carry)
        lat_scores_to(1, 2 * n_pairs - 1)
        carry = lat_fold_from(0, 2 * n_pairs - 2, carry)
    s1c, s2c = scores(k_ref[n_lat:t_all, :])
    if n_pairs > 0:
        carry = lat_fold_from(1, 2 * n_pairs - 1, carry)
    m1, a1, m2, a2 = carry
    vcx = vx_ref[n_lat:t_all, :]
    m1, a1 = fold(s1c, vcx, m1, a1)
    m2, a2 = fold(s2c, vcx, m2, a2)

    dl = dl_ref[...]
    lam = (jnp.exp(jnp.sum(dl[0:1, :] * dl[1:2, :], axis=-1, keepdims=True))
           - jnp.exp(jnp.sum(dl[2:3, :] * dl[3:4, :], axis=-1, keepdims=True)) + lam_init)
    o = (a1[:, :HEAD_DIM] / a1[:, HEAD_DIM:HEAD_DIM + 1]
         - lam * (a2[:, :HEAD_DIM] / a2[:, HEAD_DIM:HEAD_DIM + 1]))
    ms = jnp.mean(o * o, axis=-1, keepdims=True)
    o = o * lax.rsqrt(ms + EPS) * gs_ref[...] * (1.0 - lam_init)
    o_ref[...] = o.astype(o_ref.dtype)


def _diff_attention(q1z, q2z, kd, vx, diff_lambda, subln, lam_init, n_lat, latent):
    t_all = kd.shape[0]
    n_ctx = t_all - n_lat
    if latent:
        tq, tkc, rows, blk0 = DIFF_TQ, DIFF_TK, n_lat, 0
        assert n_lat % (2 * tkc) == 0 and n_lat % tq == 0
        n_pairs = n_lat // (2 * tkc)
    else:
        tq, tkc, rows, blk0, n_pairs = n_ctx, DIFF_TK, n_ctx, n_lat // n_ctx, 0
        assert n_lat % n_ctx == 0
    if n_pairs == 0:
        k_spec = pl.BlockSpec((n_ctx, LANES), lambda h, i: (n_lat // n_ctx, h))
        v_spec = pl.BlockSpec((n_ctx, 2 * LANES), lambda h, i: (n_lat // n_ctx, h))
    else:
        k_spec = pl.BlockSpec((t_all, LANES), lambda h, i: (0, h))
        v_spec = pl.BlockSpec((t_all, 2 * LANES), lambda h, i: (0, h))
    return pl.pallas_call(
        functools.partial(_diff_kernel, n_pairs, tkc, lam_init),
        grid=(N_HEADS, rows // tq),
        in_specs=[pl.BlockSpec((tq, LANES), lambda h, i: (blk0 + i, h)),
                  pl.BlockSpec((tq, LANES), lambda h, i: (blk0 + i, h)),
                  k_spec, v_spec,
                  pl.BlockSpec((4, DIFF_QK), lambda h, i: (0, 0)),
                  pl.BlockSpec((1, HEAD_DIM), lambda h, i: (0, 0))],
        out_specs=pl.BlockSpec((tq, LANES), lambda h, i: (i, h)),
        out_shape=jax.ShapeDtypeStruct((rows, N_HEADS * HEAD_DIM), MXU_DTYPE),
        scratch_shapes=[pltpu.VMEM((2, 2, tq, tkc), F32)],
        compiler_params=_params(("arbitrary", "arbitrary")),
        name="diff_attention" if latent else "diff_attention_ctx",
    )(q1z, q2z, kd, vx, diff_lambda.astype(F32), subln.astype(F32).reshape(1, HEAD_DIM))


def _swa_kernel(n_lat, t_all, q_ref, k_ref, v_ref, sink_ref, o_ref):
    w = SWA_BLOCK
    g = q_ref.shape[1] // HEAD_DIM
    b = pl.program_id(1)
    reach = jnp.where(b * w < n_lat, SWA_WINDOW, -1)
    scale = HEAD_DIM ** -0.5
    start = jnp.clip((b - 1) * w, 0, n_lat - 3 * w)
    start = pl.multiple_of(start, w)
    kw = k_ref[pl.ds(start, 3 * w), :]
    vw = v_ref[pl.ds(start, 3 * w), :]
    kc = k_ref[n_lat:t_all, :]
    vc = v_ref[n_lat:t_all, :]
    q = jnp.concatenate([q_ref[:, j * HEAD_DIM:(j + 1) * HEAD_DIM] for j in range(g)], axis=0)
    s_w = _nt_dot(q, kw) * scale
    s_c = _nt_dot(q, kc) * scale
    qpos = b * w + lax.broadcasted_iota(I32, (w, 3 * w), 0)
    kpos = start + lax.broadcasted_iota(I32, (w, 3 * w), 1)
    mask = jnp.abs(qpos - kpos) <= reach
    mask = jnp.concatenate([mask] * g, axis=0)
    s_w = jnp.where(mask, s_w, NEG)
    sink = jnp.concatenate([jnp.broadcast_to(sink_ref[0, :, j:j + 1], (w, 1)) for j in range(g)], axis=0)
    m = jnp.maximum(jnp.maximum(jnp.max(s_w, axis=-1, keepdims=True), jnp.max(s_c, axis=-1, keepdims=True)), sink)
    p_w = jnp.exp(s_w - m)
    p_c = jnp.exp(s_c - m)
    l = jnp.sum(p_w, axis=-1, keepdims=True) + jnp.sum(p_c, axis=-1, keepdims=True) + jnp.exp(sink - m)
    inv = 1.0 / l
    o = _dot((p_w * inv).astype(vw.dtype), vw) + _dot((p_c * inv).astype(vc.dtype), vc)
    for j in range(g):
        o_ref[:, j * HEAD_DIM:(j + 1) * HEAD_DIM] = o[j * w:(j + 1) * w, :].astype(o_ref.dtype)


def _swa_attention(qs, ks, p, v_col0, sink, n_lat, nq):
    t_all = ks.shape[0]
    g = N_HEADS // SWA_KV_HEADS
    vb = v_col0 // LANES
    sink3 = sink.astype(F32).reshape(SWA_KV_HEADS, 1, g)
    return pl.pallas_call(
        functools.partial(_swa_kernel, n_lat, t_all),
        grid=(SWA_KV_HEADS, nq // SWA_BLOCK),
        in_specs=[pl.BlockSpec((SWA_BLOCK, g * HEAD_DIM), lambda kv, b: (b, kv)),
                  pl.BlockSpec((t_all, LANES), lambda kv, b: (0, kv)),
                  pl.BlockSpec((t_all, LANES), lambda kv, b: (0, vb + kv)),
                  pl.BlockSpec((1, 1, g), lambda kv, b: (kv, 0, 0))],
        out_specs=pl.BlockSpec((SWA_BLOCK, g * HEAD_DIM), lambda kv, b: (b, kv)),
        out_shape=jax.ShapeDtypeStruct((nq, N_HEADS * HEAD_DIM), MXU_DTYPE),
        compiler_params=_params(("arbitrary", "arbitrary")),
        name="swa_attention",
    )(qs, ks, p, sink3)


NA_ROWS_PER_STEP = 4


def _na_kernel(n_lat, t_all, q_ref, k_ref, v_ref, bias_ref, o_ref):
    n_rows = n_lat // GRID_W
    span = NA_ROWS * GRID_W
    scale = HEAD_DIM ** -0.5
    kc = k_ref[n_lat:t_all, :]
    vc = v_ref[n_lat:t_all, :]
    for j in range(NA_ROWS_PER_STEP):
        r = pl.program_id(1) * NA_ROWS_PER_STEP + j
        is_lat = r < n_rows
        r0 = jnp.clip(r - NA_ROWS // 2, 0, n_rows - NA_ROWS)
        delta = jnp.where(is_lat, r0 - r + (NA_ROWS - 1), NA_ROWS)
        off = pl.multiple_of(r0 * GRID_W, GRID_W)
        kn = k_ref[pl.ds(off, span), :]
        vn = v_ref[pl.ds(off, span), :]
        q = q_ref[j * GRID_W:(j + 1) * GRID_W, :]
        s_n = _nt_dot(q, kn) * scale + bias_ref[0, delta]
        s_c = _nt_dot(q, kc) * scale
        m = jnp.maximum(jnp.max(s_n, axis=-1, keepdims=True), jnp.max(s_c, axis=-1, keepdims=True))
        p_n = jnp.exp(s_n - m)
        p_c = jnp.exp(s_c - m)
        inv = 1.0 / (jnp.sum(p_n, axis=-1, keepdims=True) + jnp.sum(p_c, axis=-1, keepdims=True))
        o = _dot((p_n * inv).astype(vn.dtype), vn) + _dot((p_c * inv).astype(vc.dtype), vc)
        o_ref[j * GRID_W:(j + 1) * GRID_W, :] = o.astype(o_ref.dtype)


def _na_bias_table(rpb):
    h = rpb.shape[0]
    c = jnp.arange(GRID_W)
    c0 = jnp.clip(c - NA_COLS // 2, 0, GRID_W - NA_COLS)
    kc = jnp.arange(GRID_W)
    inwin = (kc[None, :] >= c0[:, None]) & (kc[None, :] < c0[:, None] + NA_COLS)
    col_off = jnp.clip(kc[None, :] - c[:, None] + NA_COLS - 1, 0, 2 * NA_COLS - 2)
    toe = jnp.where(inwin[None, None], rpb.astype(F32)[:, :, col_off], NEG)
    rows = jnp.arange(NA_ROWS)[:, None] + jnp.arange(NA_ROWS)[None, :]
    tab = toe[:, rows]
    tab = tab.transpose(0, 1, 3, 2, 4).reshape(h, NA_ROWS, GRID_W, NA_ROWS * GRID_W)
    masked = jnp.full((h, 1, GRID_W, NA_ROWS * GRID_W), NEG, F32)
    return jnp.concatenate([tab, masked], axis=1)


def _na_attention(qn, kn, p, v_col0, bias, n_lat, nq):
    t_all = kn.shape[0]
    vb = v_col0 // LANES
    tq = NA_ROWS_PER_STEP * GRID_W
    return pl.pallas_call(
        functools.partial(_na_kernel, n_lat, t_all),
        grid=(N_HEADS, nq // tq),
        in_specs=[pl.BlockSpec((tq, LANES), lambda h, i: (i, h)),
                  pl.BlockSpec((t_all, LANES), lambda h, i: (0, h)),
                  pl.BlockSpec((t_all, LANES), lambda h, i: (0, vb + h)),
                  pl.BlockSpec((1, NA_ROWS + 1, GRID_W, NA_ROWS * GRID_W), lambda h, i: (h, 0, 0, 0))],
        out_specs=pl.BlockSpec((tq, LANES), lambda h, i: (i, h)),
        out_shape=jax.ShapeDtypeStruct((nq, N_HEADS * HEAD_DIM), MXU_DTYPE),
        compiler_params=_params(("arbitrary", "arbitrary")),
        name="na_attention",
    )(qn, kn, p, bias)


def _merge_kernel(h_ref, y0_ref, y1_ref, y2_ref, y3_ref, wg_ref, bg_ref, wb_ref, o_ref):
    h = h_ref[...]
    acc = None
    for i, y_ref in enumerate((y0_ref, y1_ref, y2_ref, y3_ref)):
        gate = _sigmoid(_dot(h, wg_ref[i]) + bg_ref[i])
        term = gate * _dot(y_ref[...], wb_ref[i])
        acc = term if acc is None else acc + term
    o_ref[...] = acc.astype(o_ref.dtype)


def _merge(h, ys, wg, bg, wb, m_rows):
    d = h.shape[1]
    bw = ys[0].shape[1]
    tm = _pick(m_rows, (1280, 1024, 640, 512, 256))
    tn = 256
    once = pl.Buffered(1)
    return pl.pallas_call(
        _merge_kernel,
        grid=(m_rows // tm, d // tn),
        in_specs=[pl.BlockSpec((tm, d), lambda i, j: (i, 0), pipeline_mode=once)]
        + [pl.BlockSpec((tm, bw), lambda i, j: (i, 0), pipeline_mode=once)] * 4
        + [pl.BlockSpec((4, d, tn), lambda i, j: (0, 0, j)),
           pl.BlockSpec((4, 1, tn), lambda i, j: (0, 0, j)),
           pl.BlockSpec((4, bw, tn), lambda i, j: (0, 0, j))],
        out_specs=pl.BlockSpec((tm, tn), lambda i, j: (i, j)),
        out_shape=jax.ShapeDtypeStruct((m_rows, d), MXU_DTYPE),
        compiler_params=_params(("arbitrary", "arbitrary"), vmem_mb=56),
        name="merge",
    )(h, *ys, wg, bg.astype(F32).reshape(4, 1, d), wb)


def _router_kernel(h_ref, w_ref, b_ref, sel_ref, idx_ref, wk_ref):
    tr = h_ref.shape[0]
    per = N_EXPERTS // N_GROUPS
    scores = _sigmoid(_nt_dot(w_ref[...], h_ref[...]))
    choice = scores + b_ref[...]
    sub = lax.broadcasted_iota(I32, (per, tr), 0)
    neg_inf = -jnp.inf
    slabs = [choice[g * per:(g + 1) * per, :] for g in range(N_GROUPS)]
    grp = []
    for c in slabs:
        m1 = jnp.max(c, axis=0, keepdims=True)
        i1 = jnp.min(jnp.where(c == m1, sub, per), axis=0, keepdims=True)
        m2 = jnp.max(jnp.where(sub == i1, neg_inf, c), axis=0, keepdims=True)
        grp.append(m1 + m2)
    gsel = [jnp.zeros((1, tr), jnp.bool_) for _ in range(N_GROUPS)]
    for _ in range(TOPK_GROUPS):
        gm = functools.reduce(jnp.maximum, grp)
        found = jnp.zeros((1, tr), jnp.bool_)
        for g in range(N_GROUPS):
            hit = (grp[g] == gm) & jnp.logical_not(found)
            found = found | hit
            gsel[g] = gsel[g] | hit
            grp[g] = jnp.where(hit, neg_inf, grp[g])
    vals = [jnp.where(gsel[g], slabs[g], neg_inf) for g in range(N_GROUPS)]
    eidx = [sub + g * per for g in range(N_GROUPS)]
    sel = [jnp.zeros((per, tr), jnp.bool_) for _ in range(N_GROUPS)]
    row8 = lax.broadcasted_iota(I32, (TOP_K, tr), 0)
    idx_out = jnp.zeros((TOP_K, tr), I32)
    w_out = jnp.zeros((TOP_K, tr), F32)
    wsum = jnp.zeros((1, tr), F32)
    for k in range(TOP_K):
        vm = jnp.max(functools.reduce(jnp.maximum, vals), axis=0, keepdims=True)
        cand = [jnp.where(vals[g] == vm, eidx[g], N_EXPERTS) for g in range(N_GROUPS)]
        ik = jnp.min(functools.reduce(jnp.minimum, cand), axis=0, keepdims=True)
        wk = jnp.zeros((1, tr), F32)
        for g in range(N_GROUPS):
            hit = eidx[g] == ik
            sel[g] = sel[g] | hit
            vals[g] = jnp.where(hit, neg_inf, vals[g])
            sc = scores[g * per:(g + 1) * per, :]
            wk = wk + jnp.sum(jnp.where(hit, sc, 0.0), axis=0, keepdims=True)
        idx_out = jnp.where(row8 == k, ik, idx_out)
        w_out = jnp.where(row8 == k, wk, w_out)
        wsum = wsum + wk
    for g in range(N_GROUPS):
        sel_ref[g * per:(g + 1) * per, :] = sel[g].astype(I32)
    idx_ref[...] = idx_out
    wk_ref[...] = w_out / wsum * ROUTED_SCALE


def _router(h2, w_rt, b_r, m_rows):
    d = h2.shape[1]
    tr = 256
    return pl.pallas_call(
        _router_kernel,
        grid=(m_rows // tr,),
        in_specs=[pl.BlockSpec((tr, d), lambda i: (i, 0)),
                  pl.BlockSpec((N_EXPERTS, d), lambda i: (0, 0)),
                  pl.BlockSpec((N_EXPERTS, 1), lambda i: (0, 0))],
        out_specs=[pl.BlockSpec((N_EXPERTS, tr), lambda i: (0, i)),
                   pl.BlockSpec((TOP_K, tr), lambda i: (0, i)),
                   pl.BlockSpec((TOP_K, tr), lambda i: (0, i))],
        out_shape=[jax.ShapeDtypeStruct((N_EXPERTS, m_rows), I32),
                   jax.ShapeDtypeStruct((TOP_K, m_rows), I32),
                   jax.ShapeDtypeStruct((TOP_K, m_rows), F32)],
        compiler_params=_params(("arbitrary",)),
        name="router",
    )(h2, w_rt, b_r.astype(F32).reshape(N_EXPERTS, 1))


def _row_gather(tok_ref, n_rows, src_hbm, dst_buf, sem, start):
    def body(r, carry):
        t = tok_ref[0, 0, r] if start else 0
        cp = pltpu.make_async_copy(src_hbm.at[pl.ds(t, 1), :], dst_buf.at[pl.ds(r, 1), :], sem)
        if start:
            cp.start()
        else:
            cp.wait()
        return carry
    lax.fori_loop(0, n_rows, body, 0, unroll=8)


def _expert_kernel(nused_ref, blke_ref, tok_ref, tokn_ref, hp_ref, wgu_ref, wd_ref, sw_ref, y_ref, xbuf, sem):
    del blke_ref
    i = pl.program_id(0)
    nu = nused_ref[0]
    tm = xbuf.shape[1]
    half = xbuf.shape[2]
    slot = i % 2

    @pl.when(i == 0)
    def _():
        _row_gather(tok_ref, tm, hp_ref, xbuf.at[0], sem.at[0], True)

    @pl.when(i + 1 < nu)
    def _():
        _row_gather(tokn_ref, tm, hp_ref, xbuf.at[1 - slot], sem.at[1 - slot], True)

    @pl.when(i < nu)
    def _():
        _row_gather(tok_ref, tm, hp_ref, xbuf.at[slot], sem.at[slot], False)
        lo, hi = _unpack_halves(xbuf[slot])
        gu = (_dot(lo.astype(MXU_DTYPE), wgu_ref[0, :half, :])
              + _dot(hi.astype(MXU_DTYPE), wgu_ref[0, half:, :]))
        g = gu[:, :EXPERT_DIM]
        u = gu[:, EXPERT_DIM:]
        a = (g * _sigmoid(g) * u).astype(MXU_DTYPE)
        y = _dot(a, wd_ref[0]) * sw_ref[...]
        y_ref[...] = _pack_halves(y)

    @pl.when(i >= nu)
    def _():
        y_ref[...] = jnp.zeros_like(y_ref)


def _experts(hp, wgu, wd, n_used, blk_e, slot_tok, slot_w, n_blk):
    d2 = hp.shape[1]
    d = 2 * d2
    tm = MOE_TILE
    tok3 = slot_tok.reshape(n_blk, 1, tm)
    last = n_blk - 1
    grid_spec = pltpu.PrefetchScalarGridSpec(
        num_scalar_prefetch=2,
        grid=(n_blk,),
        in_specs=[pl.BlockSpec((1, 1, tm), lambda i, nu, be: (i, 0, 0), memory_space=pltpu.SMEM),
                  pl.BlockSpec((1, 1, tm), lambda i, nu, be: (jnp.minimum(i + 1, last), 0, 0),
                               memory_space=pltpu.SMEM),
                  pl.BlockSpec(memory_space=pl.ANY),
                  pl.BlockSpec((1, d, 2 * EXPERT_DIM), lambda i, nu, be: (be[i], 0, 0)),
                  pl.BlockSpec((1, EXPERT_DIM, d), lambda i, nu, be: (be[i], 0, 0)),
                  pl.BlockSpec((tm, 1), lambda i, nu, be: (i, 0))],
        out_specs=pl.BlockSpec((tm, d2), lambda i, nu, be: (i, 0)),
        scratch_shapes=[pltpu.VMEM((2, tm, d2), U32), pltpu.SemaphoreType.DMA((2,))],
    )
    return pl.pallas_call(
        _expert_kernel,
        grid_spec=grid_spec,
        out_shape=jax.ShapeDtypeStruct((n_blk * tm, d2), U32),
        compiler_params=_params(("arbitrary",)),
        name="routed_experts",
    )(n_used, blk_e, tok3, tok3, hp, wgu, wd, slot_w.reshape(n_blk * tm, 1))


def _shared_kernel(h_ref, wgu_ref, wd_ref, o_ref):
    gu = _dot(h_ref[...], wgu_ref[...])
    hd = gu.shape[1] // 2
    g = gu[:, :hd]
    u = gu[:, hd:]
    a = (g * _sigmoid(g) * u).astype(MXU_DTYPE)
    o_ref[...] = _dot(a, wd_ref[...]).astype(o_ref.dtype)


def _shared_expert(h2, wgu, wd, m_rows):
    d = h2.shape[1]
    tm = 256
    return pl.pallas_call(
        _shared_kernel,
        grid=(m_rows // tm,),
        in_specs=[pl.BlockSpec((tm, d), lambda i: (i, 0)),
                  pl.BlockSpec(wgu.shape, lambda i: (0, 0)),
                  pl.BlockSpec(wd.shape, lambda i: (0, 0))],
        out_specs=pl.BlockSpec((tm, d), lambda i: (i, 0)),
        out_shape=jax.ShapeDtypeStruct((m_rows, d), MXU_DTYPE),
        compiler_params=_params(("arbitrary",)),
        name="shared_expert",
    )(h2, wgu, wd)


def _combine_kernel(n_lat, sl_ref, sln_ref, y_hbm, x_ref, sh_ref, g_ref, o_ref, buf, sem):
    i = pl.program_id(0)
    nsteps = pl.num_programs(0)
    tc = x_ref.shape[0]
    nrow = buf.shape[1]
    slot = i % 2

    @pl.when(i == 0)
    def _():
        _row_gather(sl_ref, nrow, y_hbm, buf.at[0], sem.at[0], True)

    @pl.when(i + 1 < nsteps)
    def _():
        _row_gather(sln_ref, nrow, y_hbm, buf.at[1 - slot], sem.at[1 - slot], True)

    _row_gather(sl_ref, nrow, y_hbm, buf.at[slot], sem.at[slot], False)
    lo_acc = None
    hi_acc = None
    for k in range(TOP_K):
        lo, hi = _unpack_halves(buf[slot, k * tc:(k + 1) * tc, :])
        lo_acc = lo if lo_acc is None else lo_acc + lo
        hi_acc = hi if hi_acc is None else hi_acc + hi
    routed = jnp.concatenate([lo_acc, hi_acc], axis=1)
    rows = i * tc + lax.broadcasted_iota(I32, (tc, 1), 0)
    g = jnp.where(rows >= n_lat, g_ref[1:2, :], g_ref[0:1, :])
    o_ref[...] = x_ref[...] + g * (routed + sh_ref[...].astype(F32))


def _combine(y_slots, slots, x1, sh, mod, gate_blk, n_lat, m_rows):
    d = x1.shape[1]
    tc = COMBINE_TILE
    nsteps = m_rows // tc
    last = nsteps - 1
    return pl.pallas_call(
        functools.partial(_combine_kernel, n_lat),
        grid=(nsteps,),
        in_specs=[pl.BlockSpec((1, 1, TOP_K * tc), lambda i: (i, 0, 0), memory_space=pltpu.SMEM),
                  pl.BlockSpec((1, 1, TOP_K * tc), lambda i: (jnp.minimum(i + 1, last), 0, 0),
                               memory_space=pltpu.SMEM),
                  pl.BlockSpec(memory_space=pl.ANY),
                  pl.BlockSpec((tc, d), lambda i: (i, 0)),
                  pl.BlockSpec((tc, d), lambda i: (i, 0)),
                  pl.BlockSpec((2, d), lambda i: (0, gate_blk))],
        out_specs=pl.BlockSpec((tc, d), lambda i: (i, 0)),
        out_shape=jax.ShapeDtypeStruct((m_rows, d), F32),
        scratch_shapes=[pltpu.VMEM((2, TOP_K * tc, d // 2), U32), pltpu.SemaphoreType.DMA((2,))],
        compiler_params=_params(("arbitrary",)),
        name="moe_combine",
    )(slots, slots, y_slots, x1, sh, mod)


def _dispatch_tables(sel, idx, wk, m_rows):
    tm = MOE_TILE
    tk = m_rows * TOP_K
    n_blk = (tk + tm - 1) // tm + N_EXPERTS
    cs = jnp.cumsum(sel, axis=1)
    counts = cs[:, -1]
    padded = ((counts + tm - 1) // tm) * tm
    pend = jnp.cumsum(padded)
    pstart = pend - padded
    dest_et = pstart[:, None] + cs - 1
    dest = jnp.take_along_axis(dest_et, idx, axis=0)
    tok = jnp.broadcast_to(jnp.arange(m_rows, dtype=I32)[None, :], dest.shape)
    slot_tok = jnp.zeros((n_blk * tm,), I32).at[dest.reshape(-1)].set(tok.reshape(-1))
    slot_w = jnp.zeros((n_blk * tm,), F32).at[dest.reshape(-1)].set(wk.reshape(-1))
    blk_start = jnp.arange(n_blk, dtype=I32) * tm
    blk_e = jnp.minimum(jnp.sum((pend[None, :] <= blk_start[:, None]).astype(I32), axis=1), N_EXPERTS - 1)
    n_used = (pend[-1] // tm).astype(I32).reshape(1)
    tc = COMBINE_TILE
    slots = dest.reshape(TOP_K, m_rows // tc, tc).transpose(1, 0, 2).reshape(m_rows // tc, 1, TOP_K * tc)
    return n_used, blk_e, slot_tok, slot_w, slots.astype(I32), n_blk


def kernel(x, c, ctx, c_ctx, w_mod, b_mod, g_norm1, g_norm2, w_in, conv_w, diff_qnorm, diff_knorm, diff_lambda, diff_subln, swa_qnorm, swa_knorm, swa_sink, na_qnorm, na_knorm, na_rpb, w_gate, b_gate, w_branch, w_out, w_router, b_router, w_exp_gate, w_exp_up, w_exp_down, w_sh_gate, w_sh_up, w_sh_down):
    bsz, n, d = x.shape
    assert bsz == 1
    n_ctx = ctx.shape[1]
    t_all = n + n_ctx
    depth = w_in.shape[0]
    bw = d // 4
    cdt = MXU_DTYPE

    xa = jnp.concatenate([x[0], ctx[0]], axis=0)
    cc = jnp.stack([c[0], c_ctx], axis=0).astype(F32)
    rope_d = _rope_tables(n, t_all, DIFF_QK // 2)
    rope_s = _rope_tables(n, t_all, HEAD_DIM // 2)

    o_conv = 0
    o_dq, o_dk, o_dv = 3 * bw, 4 * bw, 5 * bw
    o_sq = 6 * bw
    o_sk = o_sq + bw
    o_sv = o_sk + SWA_KV_HEADS * HEAD_DIM
    o_nq = o_sv + SWA_KV_HEADS * HEAD_DIM
    o_nk, o_nv = o_nq + bw, o_nq + 2 * bw

    for l in range(depth):
        update_ctx = l < depth - 1
        nq = t_all if update_ctx else n
        lam_init = 0.8 - 0.6 * math.exp(-0.3 * l)
        mod = _modvec(cc, w_mod[l], b_mod[l])

        h = _rmsmod(xa, g_norm1[l], mod, 0, 1, n, t_all)
        p = _matmul(h, w_in[l].astype(cdt), cdt, t_all)

        a = _short_conv(p, conv_w[l].astype(F32), n, nq)
        q1z, q2z = _prep(p, o_dq, N_HEADS, diff_qnorm[l], DIFF_QK, rope_d, split=True,
                         scale=DIFF_QK ** -0.5 * math.log2(math.e))
        kd = _prep(p, o_dk, N_HEADS, diff_knorm[l], DIFF_QK, rope_d)
        qs = _prep(p, o_sq, N_HEADS, swa_qnorm[l], HEAD_DIM, rope_s)
        ks = _prep(p, o_sk, SWA_KV_HEADS, swa_knorm[l], HEAD_DIM, rope_s)
        qn = _prep(p, o_nq, N_HEADS, na_qnorm[l], HEAD_DIM, None)
        kn = _prep(p, o_nk, N_HEADS, na_knorm[l], HEAD_DIM, None)

        v_d = p[:, o_dv:o_dv + bw].reshape(t_all, N_HEADS, HEAD_DIM)
        vx = jnp.concatenate([v_d, jnp.ones_like(v_d)], axis=-1).reshape(t_all, 2 * bw)
        yb = _diff_attention(q1z, q2z, kd, vx, diff_lambda[l], diff_subln[l], lam_init, n, True)
        if update_ctx:
            yb_c = _diff_attention(q1z, q2z, kd, vx, diff_lambda[l], diff_subln[l], lam_init, n, False)
            yb = jnp.concatenate([yb, yb_c], axis=0)
        ys = _swa_attention(qs, ks, p, o_sv, swa_sink[l], n, nq)
        yn = _na_attention(qn, kn, p, o_nv, _na_bias_table(na_rpb[l]), n, nq)

        acc = _merge(h, (a, yb, ys, yn), w_gate[l].astype(cdt), b_gate[l], w_branch[l].astype(cdt), nq)
        x1 = _matmul_residual(acc, w_out[l].astype(cdt), xa, mod, 2, n, nq)

        h2, h2p = _rmsmod(x1, g_norm2[l], mod, 3, 4, n, nq, packed=True)
        sel, idx, wk = _router(h2, w_router[l].T.astype(cdt), b_router[l], nq)
        n_used, blk_e, slot_tok, slot_w, slots, n_blk = _dispatch_tables(sel, idx, wk, nq)
        wgu = jnp.concatenate([w_exp_gate[l], w_exp_up[l]], axis=-1).astype(cdt)
        y_slots = _experts(h2p, wgu, w_exp_down[l].astype(cdt), n_used, blk_e, slot_tok, slot_w, n_blk)
        wsh = jnp.concatenate([w_sh_gate[l], w_sh_up[l]], axis=-1).astype(cdt)
        sh = _shared_expert(h2, wsh, w_sh_down[l].astype(cdt), nq)
        xa = _combine(y_slots, slots, x1, sh, mod, 5, n, nq)

    return xa[:n].reshape(bsz, n, d)
```

```python
import functools
import math

import jax
import jax.numpy as jnp
from jax import lax
from jax.experimental import pallas as pl
from jax.experimental.pallas import tpu as pltpu

GRID_W = 64
HEAD_DIM = 128
DIFF_QK = 64
SWA_WINDOW = 128
SWA_BLOCK = 128
SWA_KV_HEADS = 2
N_HEADS = 8
NA_ROWS = 8
NA_COLS = 16
ROPE_BASE = 10000.0
N_EXPERTS = 64
N_GROUPS = 8
TOPK_GROUPS = 4
TOP_K = 8
EXPERT_DIM = 256
ROUTED_SCALE = 2.5
EPS = 1e-6
NEG = -1e30

LANES = 128
SUBLANES = 8
V7X_VMEM_BYTES = 64 * 1024 * 1024

MXU_DTYPE = jnp.bfloat16
MOE_TILE = 256
COMBINE_TILE = 64

F32 = jnp.float32
I32 = jnp.int32
U32 = jnp.uint32


def _params(sem, vmem_mb=48):
    return pltpu.CompilerParams(dimension_semantics=sem, vmem_limit_bytes=vmem_mb * 1024 * 1024)


def _pick(m, cands):
    for c in cands:
        if m % c == 0:
            return c
    raise ValueError(f"no tile for {m} in {cands}")


def _nt_dot(a, b):
    return lax.dot_general(a, b, (((1,), (1,)), ((), ())), preferred_element_type=F32)


def _dot(a, b):
    return jnp.dot(a, b, preferred_element_type=F32)


def _sigmoid(x):
    return 1.0 / (1.0 + jnp.exp(-x))


def _modvec_kernel(s_ref, w_ref, b_ref, o_ref):
    d, tn = w_ref.shape
    rows = []
    for r in range(2):
        c = s_ref[r]
        s = c * _sigmoid(c)
        parts = []
        for j in range(tn // LANES):
            prod = w_ref[:, j * LANES:(j + 1) * LANES] * s
            part = jnp.sum(prod.reshape(d // SUBLANES, SUBLANES, LANES), axis=0)
            parts.append(jnp.sum(part, axis=0, keepdims=True))
        rows.append(jnp.concatenate(parts, axis=1))
    is_first = lax.broadcasted_iota(I32, (2, tn), 0) == 0
    o_ref[...] = jnp.where(is_first, rows[0], rows[1]) + b_ref[...]


def _modvec(cc, w, b):
    d, n = w.shape
    tn = 512
    s_rep = jnp.broadcast_to(cc[:, :, None], (2, d, LANES))
    return pl.pallas_call(
        _modvec_kernel,
        grid=(n // tn,),
        in_specs=[pl.BlockSpec((2, d, LANES), lambda j: (0, 0, 0)),
                  pl.BlockSpec((d, tn), lambda j: (0, j)),
                  pl.BlockSpec((1, tn), lambda j: (0, j))],
        out_specs=pl.BlockSpec((2, tn), lambda j: (0, j)),
        out_shape=jax.ShapeDtypeStruct((2, n), F32),
        compiler_params=_params(("arbitrary",)),
        name="modvec",
    )(s_rep, w, b.reshape(1, n))


def _pack_halves(v):
    c = v.shape[1] // 2
    bits = lax.bitcast_convert_type(v.astype(jnp.bfloat16).astype(F32), U32)
    return (bits[:, c:] & jnp.uint32(0xFFFF0000)) | (bits[:, :c] >> 16)


def _unpack_halves(w):
    lo = lax.bitcast_convert_type(w << 16, F32)
    hi = lax.bitcast_convert_type(w & jnp.uint32(0xFFFF0000), F32)
    return lo, hi


def _rmsmod_kernel(n_lat, packed, x_ref, g_ref, sh_ref, sc_ref, o_ref, *p_ref):
    tr = x_ref.shape[0]
    x = x_ref[...]
    ms = jnp.mean(x * x, axis=-1, keepdims=True)
    y = x * lax.rsqrt(ms + EPS) * g_ref[...]
    rows = pl.program_id(0) * tr + lax.broadcasted_iota(I32, (tr, 1), 0)
    is_ctx = rows >= n_lat
    sh = jnp.where(is_ctx, sh_ref[1:2, :], sh_ref[0:1, :])
    sc = jnp.where(is_ctx, sc_ref[1:2, :], sc_ref[0:1, :])
    h = y * (1.0 + sc) + sh
    o_ref[...] = h.astype(o_ref.dtype)
    if packed:
        p_ref[0][...] = _pack_halves(h)


def _rmsmod(x, g, mod, shift_blk, scale_blk, n_lat, m_rows, packed=False):
    d = x.shape[1]
    tr = 256
    out_shape = [jax.ShapeDtypeStruct((m_rows, d), MXU_DTYPE)]
    out_specs = [pl.BlockSpec((tr, d), lambda i: (i, 0))]
    if packed:
        out_shape.append(jax.ShapeDtypeStruct((m_rows, d // 2), U32))
        out_specs.append(pl.BlockSpec((tr, d // 2), lambda i: (i, 0)))
    res = pl.pallas_call(
        functools.partial(_rmsmod_kernel, n_lat, packed),
        grid=(m_rows // tr,),
        in_specs=[pl.BlockSpec((tr, d), lambda i: (i, 0)),
                  pl.BlockSpec((1, d), lambda i: (0, 0)),
                  pl.BlockSpec((2, d), lambda i: (0, shift_blk)),
                  pl.BlockSpec((2, d), lambda i: (0, scale_blk))],
        out_specs=out_specs,
        out_shape=out_shape,
        compiler_params=_params(("arbitrary",)),
        name="rmsmod",
    )(x, g.reshape(1, d), mod, mod)
    return res if packed else res[0]


def _mm_kernel(a_ref, w_ref, o_ref):
    o_ref[...] = _dot(a_ref[...], w_ref[...]).astype(o_ref.dtype)


def _mm_res_kernel(n_lat, a_ref, w_ref, r_ref, g_ref, o_ref):
    tm = a_ref.shape[0]
    acc = _dot(a_ref[...], w_ref[...])
    rows = pl.program_id(0) * tm + lax.broadcasted_iota(I32, (tm, 1), 0)
    g = jnp.where(rows >= n_lat, g_ref[1:2, :], g_ref[0:1, :])
    o_ref[...] = r_ref[...] + g * acc


def _matmul(a, w, out_dtype, m_rows):
    k = a.shape[1]
    n = w.shape[1]
    tm = _pick(m_rows, (1280, 1024, 640, 512, 256))
    tn = 512
    return pl.pallas_call(
        _mm_kernel,
        grid=(m_rows // tm, n // tn),
        in_specs=[pl.BlockSpec((tm, k), lambda i, j: (i, 0)),
                  pl.BlockSpec((k, tn), lambda i, j: (0, j))],
        out_specs=pl.BlockSpec((tm, tn), lambda i, j: (i, j)),
        out_shape=jax.ShapeDtypeStruct((m_rows, n), out_dtype),
        compiler_params=_params(("arbitrary", "arbitrary")),
        name="matmul",
    )(a, w)


def _matmul_residual(a, w, res, mod, gate_blk, n_lat, m_rows):
    k = a.shape[1]
    n = w.shape[1]
    tm = _pick(m_rows, (1280, 1024, 640, 512, 256))
    tn = 512
    nb = n // tn
    return pl.pallas_call(
        functools.partial(_mm_res_kernel, n_lat),
        grid=(m_rows // tm, nb),
        in_specs=[pl.BlockSpec((tm, k), lambda i, j: (i, 0)),
                  pl.BlockSpec((k, tn), lambda i, j: (0, j)),
                  pl.BlockSpec((tm, tn), lambda i, j: (i, j)),
                  pl.BlockSpec((2, tn), lambda i, j: (0, gate_blk * nb + j))],
        out_specs=pl.BlockSpec((tm, tn), lambda i, j: (i, j)),
        out_shape=jax.ShapeDtypeStruct((m_rows, n), F32),
        compiler_params=_params(("arbitrary", "arbitrary")),
        name="matmul_residual",
    )(a, w, res, mod)


def _conv_kernel(n_lat, t_all, b_ref, c_ref, x_ref, cp_ref, xp_ref, cn_ref, xn_ref, w_ref, o_ref):
    tr = b_ref.shape[0]
    u = c_ref[...].astype(F32) * x_ref[...].astype(F32)
    u_prev_row = cp_ref[SUBLANES - 1:SUBLANES, :].astype(F32) * xp_ref[SUBLANES - 1:SUBLANES, :].astype(F32)
    u_next_row = cn_ref[0:1, :].astype(F32) * xn_ref[0:1, :].astype(F32)
    r = lax.broadcasted_iota(I32, (tr, 1), 0)
    tok = pl.program_id(0) * tr + r
    up = jnp.where(r == 0, u_prev_row, pltpu.roll(u, 1, 0))
    un = jnp.where(r == tr - 1, u_next_row, pltpu.roll(u, tr - 1, 0))
    up = jnp.where((tok == 0) | (tok == n_lat), 0.0, up)
    un = jnp.where((tok == n_lat - 1) | (tok == t_all - 1), 0.0, un)
    y = w_ref[0:1, :] * up + w_ref[1:2, :] * u + w_ref[2:3, :] * un
    o_ref[...] = (b_ref[...].astype(F32) * y).astype(o_ref.dtype)


def _short_conv(p, conv_w, n_lat, m_rows):
    t_all = p.shape[0]
    cw = conv_w.shape[1]
    tr = 256
    tc = 512
    nc = cw // tc
    rb = tr // SUBLANES
    last = t_all // SUBLANES - 1
    return pl.pallas_call(
        functools.partial(_conv_kernel, n_lat, t_all),
        grid=(m_rows // tr, nc),
        in_specs=[pl.BlockSpec((tr, tc), lambda i, j: (i, j)),
                  pl.BlockSpec((tr, tc), lambda i, j: (i, nc + j)),
                  pl.BlockSpec((tr, tc), lambda i, j: (i, 2 * nc + j)),
                  pl.BlockSpec((SUBLANES, tc), lambda i, j: (jnp.maximum(i * rb - 1, 0), nc + j)),
                  pl.BlockSpec((SUBLANES, tc), lambda i, j: (jnp.maximum(i * rb - 1, 0), 2 * nc + j)),
                  pl.BlockSpec((SUBLANES, tc), lambda i, j: (jnp.minimum((i + 1) * rb, last), nc + j)),
                  pl.BlockSpec((SUBLANES, tc), lambda i, j: (jnp.minimum((i + 1) * rb, last), 2 * nc + j)),
                  pl.BlockSpec((3, tc), lambda i, j: (0, j))],
        out_specs=pl.BlockSpec((tr, tc), lambda i, j: (i, j)),
        out_shape=jax.ShapeDtypeStruct((m_rows, cw), MXU_DTYPE),
        compiler_params=_params(("arbitrary", "arbitrary")),
        name="short_conv",
    )(p, p, p, p, p, p, p, conv_w)


def _prep_kernel(n_heads, group, half, split, scale, x_ref, g_ref, *rest):
    if half:
        cos_ref, sa_ref, sb_ref = rest[:3]
        outs = rest[3:]
    else:
        outs = rest
    lane = lax.broadcasted_iota(I32, (1, LANES), 1)
    lo = lane < DIFF_QK
    g = g_ref[...]
    for h in range(n_heads):
        x = x_ref[:, h * LANES:(h + 1) * LANES].astype(F32)
        sq = x * x
        if group == LANES:
            ms = jnp.mean(sq, axis=-1, keepdims=True)
        else:
            s_lo = jnp.sum(jnp.where(lo, sq, 0.0), axis=-1, keepdims=True)
            s_hi = jnp.sum(jnp.where(lo, 0.0, sq), axis=-1, keepdims=True)
            ms = jnp.where(lo, s_lo, s_hi) * (1.0 / group)
        y = x * lax.rsqrt(ms + EPS) * g
        if half:
            y = (y * cos_ref[...] + pltpu.roll(y, LANES - half, 1) * sa_ref[...]
                 + pltpu.roll(y, half, 1) * sb_ref[...])
        if scale != 1.0:
            y = y * scale
        if split:
            outs[0][:, h * LANES:(h + 1) * LANES] = jnp.where(lo, y, 0.0).astype(outs[0].dtype)
            outs[1][:, h * LANES:(h + 1) * LANES] = jnp.where(lo, 0.0, y).astype(outs[1].dtype)
        else:
            outs[0][:, h * LANES:(h + 1) * LANES] = y.astype(outs[0].dtype)


def _prep(p, col0, n_heads, gain, group, rope, split=False, scale=1.0):
    t_all = p.shape[0]
    tr = _pick(t_all, (640, 256))
    hpb = _pick(math.gcd(col0 // LANES, n_heads), (8, 4, 2, 1))
    w = hpb * LANES
    cb = col0 // w
    g = jnp.tile(gain.astype(F32), LANES // gain.shape[0]).reshape(1, LANES)
    in_specs = [pl.BlockSpec((tr, w), lambda i, j: (i, cb + j)),
                pl.BlockSpec((1, LANES), lambda i, j: (0, 0))]
    args = [p, g]
    half = 0
    if rope is not None:
        half = rope[3]
        in_specs += [pl.BlockSpec((tr, LANES), lambda i, j: (i, 0))] * 3
        args += list(rope[:3])
    n_out = 2 if split else 1
    res = pl.pallas_call(
        functools.partial(_prep_kernel, hpb, group, half, split, scale),
        grid=(t_all // tr, n_heads // hpb),
        in_specs=in_specs,
        out_specs=[pl.BlockSpec((tr, w), lambda i, j: (i, j))] * n_out,
        out_shape=[jax.ShapeDtypeStruct((t_all, n_heads * LANES), MXU_DTYPE)] * n_out,
        compiler_params=_params(("arbitrary", "arbitrary")),
        name="qk_prep",
    )(*args)
    return res if split else res[0]


def _rope_tables(n_lat, t_all, seg):
    half = seg // 2
    n_rows = n_lat // GRID_W
    lane = jnp.arange(LANES)
    freqs = ROPE_BASE ** (-jnp.arange(0, seg, 2, dtype=F32) / seg)
    f_lane = freqs[lane % half]
    use_col = ((lane // seg) % 2 == 1)[None, None, :]
    ang_r = jnp.arange(n_rows, dtype=F32)[:, None] * f_lane[None, :]
    ang_c = jnp.arange(GRID_W, dtype=F32)[:, None] * f_lane[None, :]

    def expand(fn):
        full = jnp.where(use_col, fn(ang_c)[None, :, :], fn(ang_r)[:, None, :])
        return full.reshape(n_lat, LANES)

    cos = expand(jnp.cos)
    sin = expand(jnp.sin)
    first = (lane % seg) < half
    sa = jnp.where(first[None, :], -sin, 0.0)
    sb = jnp.where(first[None, :], 0.0, sin)
    pad = t_all - n_lat
    cos = jnp.concatenate([cos, jnp.ones((pad, LANES), F32)], axis=0)
    sa = jnp.concatenate([sa, jnp.zeros((pad, LANES), F32)], axis=0)
    sb = jnp.concatenate([sb, jnp.zeros((pad, LANES), F32)], axis=0)
    return cos, sa, sb, half


DIFF_TQ = 512
DIFF_TK = 512
DIFF_GROUP = 4


def _diff_kernel(n_pairs, tkc, lam_init, q1_ref, q2_ref, k_ref, vx_ref, dl_ref, gs_ref, o_ref, s_scr):
    tq = q1_ref.shape[0]
    t_all = k_ref.shape[0]
    n_lat = 2 * n_pairs * tkc
    q1 = q1_ref[...]
    q2 = q2_ref[...]

    def scores(kc):
        return _nt_dot(q1, kc), _nt_dot(q2, kc)

    def fold(s, vc, m, a):
        mn = jnp.maximum(m, jnp.max(s, axis=-1, keepdims=True))
        alpha = jnp.exp2(m - mn)
        p = jnp.exp2(s - mn).astype(vc.dtype)
        return mn, alpha * a + _dot(p, vc)

    def lat_scores_to(buf, c):
        off = pl.multiple_of(c * tkc, tkc)
        s1, s2 = scores(k_ref[pl.ds(off, tkc), :])
        s_scr[buf, 0] = s1
        s_scr[buf, 1] = s2

    def lat_fold_from(buf, c, carry):
        m1, a1, m2, a2 = carry
        off = pl.multiple_of(c * tkc, tkc)
        vc = vx_ref[pl.ds(off, tkc), :]
        m1, a1 = fold(s_scr[buf, 0], vc, m1, a1)
        m2, a2 = fold(s_scr[buf, 1], vc, m2, a2)
        return m1, a1, m2, a2

    minit = jnp.full((tq, 1), -jnp.inf, F32)
    ainit = jnp.zeros((tq, 2 * HEAD_DIM), F32)
    carry = (minit, ainit, minit, ainit)
    n_chunks = 2 * n_pairs
    group = _pick(n_chunks, (DIFF_GROUP, 2)) if n_pairs > 0 else 0
    if n_pairs > 0:
        lat_scores_to(0, 0)

        def run_group(c0, carry, last):
            for u in range(group):
                if not (last and u == group - 1):
                    lat_scores_to((u + 1) % 2, c0 + u + 1)
                    carry = lat_fold_from(u % 2, c0 + u, carry)
            return carry

        carry = lax.fori_loop(0, n_chunks // group - 1,
                              lambda j, cr: run_group(j * group, cr, False), carry)
        carry = run_group(n_chunks - group, carry, True)
    s1c, s2c = scores(k_ref[n_lat:t_all, :])
    if n_pairs > 0:
        carry = lat_fold_from(1, n_chunks - 1, carry)
    m1, a1, m2, a2 = carry
    vcx = vx_ref[n_lat:t_all, :]
    m1, a1 = fold(s1c, vcx, m1, a1)
    m2, a2 = fold(s2c, vcx, m2, a2)

    dl = dl_ref[...]
    lam = (jnp.exp(jnp.sum(dl[0:1, :] * dl[1:2, :], axis=-1, keepdims=True))
           - jnp.exp(jnp.sum(dl[2:3, :] * dl[3:4, :], axis=-1, keepdims=True)) + lam_init)
    o = (a1[:, :HEAD_DIM] / a1[:, HEAD_DIM:HEAD_DIM + 1]
         - lam * (a2[:, :HEAD_DIM] / a2[:, HEAD_DIM:HEAD_DIM + 1]))
    ms = jnp.mean(o * o, axis=-1, keepdims=True)
    o = o * lax.rsqrt(ms + EPS) * gs_ref[...] * (1.0 - lam_init)
    o_ref[...] = o.astype(o_ref.dtype)


def _diff_attention(q1z, q2z, kd, vx, diff_lambda, subln, lam_init, n_lat, latent):
    t_all = kd.shape[0]
    n_ctx = t_all - n_lat
    if latent:
        tq, tkc, rows, blk0 = DIFF_TQ, DIFF_TK, n_lat, 0
        assert n_lat % (2 * tkc) == 0 and n_lat % tq == 0
        n_pairs = n_lat // (2 * tkc)
    else:
        tq, tkc, rows, blk0, n_pairs = n_ctx, DIFF_TK, n_ctx, n_lat // n_ctx, 0
        assert n_lat % n_ctx == 0
    if n_pairs == 0:
        k_spec = pl.BlockSpec((n_ctx, LANES), lambda h, i: (n_lat // n_ctx, h))
        v_spec = pl.BlockSpec((n_ctx, 2 * LANES), lambda h, i: (n_lat // n_ctx, h))
    else:
        k_spec = pl.BlockSpec((t_all, LANES), lambda h, i: (0, h))
        v_spec = pl.BlockSpec((t_all, 2 * LANES), lambda h, i: (0, h))
    return pl.pallas_call(
        functools.partial(_diff_kernel, n_pairs, tkc, lam_init),
        grid=(N_HEADS, rows // tq),
        in_specs=[pl.BlockSpec((tq, LANES), lambda h, i: (blk0 + i, h)),
                  pl.BlockSpec((tq, LANES), lambda h, i: (blk0 + i, h)),
                  k_spec, v_spec,
                  pl.BlockSpec((4, DIFF_QK), lambda h, i: (0, 0)),
                  pl.BlockSpec((1, HEAD_DIM), lambda h, i: (0, 0))],
        out_specs=pl.BlockSpec((tq, LANES), lambda h, i: (i, h)),
        out_shape=jax.ShapeDtypeStruct((rows, N_HEADS * HEAD_DIM), MXU_DTYPE),
        scratch_shapes=[pltpu.VMEM((2, 2, tq, tkc), F32)],
        compiler_params=_params(("arbitrary", "arbitrary")),
        name="diff_attention" if latent else "diff_attention_ctx",
    )(q1z, q2z, kd, vx, diff_lambda.astype(F32), subln.astype(F32).reshape(1, HEAD_DIM))


def _swa_kernel(n_lat, t_all, q_ref, k_ref, v_ref, sink_ref, o_ref):
    w = SWA_BLOCK
    g = q_ref.shape[1] // HEAD_DIM
    b = pl.program_id(1)
    reach = jnp.where(b * w < n_lat, SWA_WINDOW, -1)
    scale = HEAD_DIM ** -0.5
    start = jnp.clip((b - 1) * w, 0, n_lat - 3 * w)
    start = pl.multiple_of(start, w)
    kw = k_ref[pl.ds(start, 3 * w), :]
    vw = v_ref[pl.ds(start, 3 * w), :]
    kc = k_ref[n_lat:t_all, :]
    vc = v_ref[n_lat:t_all, :]
    q = jnp.concatenate([q_ref[:, j * HEAD_DIM:(j + 1) * HEAD_DIM] for j in range(g)], axis=0)
    s_w = _nt_dot(q, kw) * scale
    s_c = _nt_dot(q, kc) * scale
    qpos = b * w + lax.broadcasted_iota(I32, (w, 3 * w), 0)
    kpos = start + lax.broadcasted_iota(I32, (w, 3 * w), 1)
    mask = jnp.abs(qpos - kpos) <= reach
    mask = jnp.concatenate([mask] * g, axis=0)
    s_w = jnp.where(mask, s_w, NEG)
    sink = jnp.concatenate([jnp.broadcast_to(sink_ref[0, :, j:j + 1], (w, 1)) for j in range(g)], axis=0)
    m = jnp.maximum(jnp.maximum(jnp.max(s_w, axis=-1, keepdims=True), jnp.max(s_c, axis=-1, keepdims=True)), sink)
    p_w = jnp.exp(s_w - m)
    p_c = jnp.exp(s_c - m)
    l = jnp.sum(p_w, axis=-1, keepdims=True) + jnp.sum(p_c, axis=-1, keepdims=True) + jnp.exp(sink - m)
    inv = 1.0 / l
    o = _dot((p_w * inv).astype(vw.dtype), vw) + _dot((p_c * inv).astype(vc.dtype), vc)
    for j in range(g):
        o_ref[:, j * HEAD_DIM:(j + 1) * HEAD_DIM] = o[j * w:(j + 1) * w, :].astype(o_ref.dtype)


def _swa_attention(qs, ks, p, v_col0, sink, n_lat, nq):
    t_all = ks.shape[0]
    g = N_HEADS // SWA_KV_HEADS
    vb = v_col0 // LANES
    sink3 = sink.astype(F32).reshape(SWA_KV_HEADS, 1, g)
    return pl.pallas_call(
        functools.partial(_swa_kernel, n_lat, t_all),
        grid=(SWA_KV_HEADS, nq // SWA_BLOCK),
        in_specs=[pl.BlockSpec((SWA_BLOCK, g * HEAD_DIM), lambda kv, b: (b, kv)),
                  pl.BlockSpec((t_all, LANES), lambda kv, b: (0, kv)),
                  pl.BlockSpec((t_all, LANES), lambda kv, b: (0, vb + kv)),
                  pl.BlockSpec((1, 1, g), lambda kv, b: (kv, 0, 0))],
        out_specs=pl.BlockSpec((SWA_BLOCK, g * HEAD_DIM), lambda kv, b: (b, kv)),
        out_shape=jax.ShapeDtypeStruct((nq, N_HEADS * HEAD_DIM), MXU_DTYPE),
        compiler_params=_params(("arbitrary", "arbitrary")),
        name="swa_attention",
    )(qs, ks, p, sink3)


NA_ROWS_PER_STEP = 4


def _na_kernel(n_lat, t_all, q_ref, k_ref, v_ref, bias_ref, o_ref):
    n_rows = n_lat // GRID_W
    span = NA_ROWS * GRID_W
    scale = HEAD_DIM ** -0.5
    kc = k_ref[n_lat:t_all, :]
    vc = v_ref[n_lat:t_all, :]
    for j in range(NA_ROWS_PER_STEP):
        r = pl.program_id(1) * NA_ROWS_PER_STEP + j
        is_lat = r < n_rows
        r0 = jnp.clip(r - NA_ROWS // 2, 0, n_rows - NA_ROWS)
        delta = jnp.where(is_lat, r0 - r + (NA_ROWS - 1), NA_ROWS)
        off = pl.multiple_of(r0 * GRID_W, GRID_W)
        kn = k_ref[pl.ds(off, span), :]
        vn = v_ref[pl.ds(off, span), :]
        q = q_ref[j * GRID_W:(j + 1) * GRID_W, :]
        s_n = _nt_dot(q, kn) * scale + bias_ref[0, delta]
        s_c = _nt_dot(q, kc) * scale
        m = jnp.maximum(jnp.max(s_n, axis=-1, keepdims=True), jnp.max(s_c, axis=-1, keepdims=True))
        p_n = jnp.exp(s_n - m)
        p_c = jnp.exp(s_c - m)
        inv = 1.0 / (jnp.sum(p_n, axis=-1, keepdims=True) + jnp.sum(p_c, axis=-1, keepdims=True))
        o = _dot((p_n * inv).astype(vn.dtype), vn) + _dot((p_c * inv).astype(vc.dtype), vc)
        o_ref[j * GRID_W:(j + 1) * GRID_W, :] = o.astype(o_ref.dtype)


def _na_bias_table(rpb):
    h = rpb.shape[0]
    c = jnp.arange(GRID_W)
    c0 = jnp.clip(c - NA_COLS // 2, 0, GRID_W - NA_COLS)
    kc = jnp.arange(GRID_W)
    inwin = (kc[None, :] >= c0[:, None]) & (kc[None, :] < c0[:, None] + NA_COLS)
    col_off = jnp.clip(kc[None, :] - c[:, None] + NA_COLS - 1, 0, 2 * NA_COLS - 2)
    toe = jnp.where(inwin[None, None], rpb.astype(F32)[:, :, col_off], NEG)
    rows = jnp.arange(NA_ROWS)[:, None] + jnp.arange(NA_ROWS)[None, :]
    tab = toe[:, rows]
    tab = tab.transpose(0, 1, 3, 2, 4).reshape(h, NA_ROWS, GRID_W, NA_ROWS * GRID_W)
    masked = jnp.full((h, 1, GRID_W, NA_ROWS * GRID_W), NEG, F32)
    return jnp.concatenate([tab, masked], axis=1)


def _na_attention(qn, kn, p, v_col0, bias, n_lat, nq):
    t_all = kn.shape[0]
    vb = v_col0 // LANES
    tq = NA_ROWS_PER_STEP * GRID_W
    return pl.pallas_call(
        functools.partial(_na_kernel, n_lat, t_all),
        grid=(N_HEADS, nq // tq),
        in_specs=[pl.BlockSpec((tq, LANES), lambda h, i: (i, h)),
                  pl.BlockSpec((t_all, LANES), lambda h, i: (0, h)),
                  pl.BlockSpec((t_all, LANES), lambda h, i: (0, vb + h)),
                  pl.BlockSpec((1, NA_ROWS + 1, GRID_W, NA_ROWS * GRID_W), lambda h, i: (h, 0, 0, 0))],
        out_specs=pl.BlockSpec((tq, LANES), lambda h, i: (i, h)),
        out_shape=jax.ShapeDtypeStruct((nq, N_HEADS * HEAD_DIM), MXU_DTYPE),
        compiler_params=_params(("arbitrary", "arbitrary")),
        name="na_attention",
    )(qn, kn, p, bias)


def _merge_kernel(h_ref, y0_ref, y1_ref, y2_ref, y3_ref, wg_ref, bg_ref, wb_ref, o_ref):
    h = h_ref[...]
    acc = None
    for i, y_ref in enumerate((y0_ref, y1_ref, y2_ref, y3_ref)):
        gate = _sigmoid(_dot(h, wg_ref[i]) + bg_ref[i])
        term = gate * _dot(y_ref[...], wb_ref[i])
        acc = term if acc is None else acc + term
    o_ref[...] = acc.astype(o_ref.dtype)


def _merge(h, ys, wg, bg, wb, m_rows):
    d = h.shape[1]
    bw = ys[0].shape[1]
    tm = _pick(m_rows, (1280, 1024, 640, 512, 256))
    tn = 256
    once = pl.Buffered(1)
    return pl.pallas_call(
        _merge_kernel,
        grid=(m_rows // tm, d // tn),
        in_specs=[pl.BlockSpec((tm, d), lambda i, j: (i, 0), pipeline_mode=once)]
        + [pl.BlockSpec((tm, bw), lambda i, j: (i, 0), pipeline_mode=once)] * 4
        + [pl.BlockSpec((4, d, tn), lambda i, j: (0, 0, j)),
           pl.BlockSpec((4, 1, tn), lambda i, j: (0, 0, j)),
           pl.BlockSpec((4, bw, tn), lambda i, j: (0, 0, j))],
        out_specs=pl.BlockSpec((tm, tn), lambda i, j: (i, j)),
        out_shape=jax.ShapeDtypeStruct((m_rows, d), MXU_DTYPE),
        compiler_params=_params(("arbitrary", "arbitrary"), vmem_mb=56),
        name="merge",
    )(h, *ys, wg, bg.astype(F32).reshape(4, 1, d), wb)


def _router_kernel(h_ref, w_ref, b_ref, sel_ref, idx_ref, wk_ref):
    tr = h_ref.shape[0]
    per = N_EXPERTS // N_GROUPS
    scores = _sigmoid(_nt_dot(w_ref[...], h_ref[...]))
    choice = scores + b_ref[...]
    sub = lax.broadcasted_iota(I32, (per, tr), 0)
    neg_inf = -jnp.inf
    slabs = [choice[g * per:(g + 1) * per, :] for g in range(N_GROUPS)]
    grp = []
    for c in slabs:
        m1 = jnp.max(c, axis=0, keepdims=True)
        i1 = jnp.min(jnp.where(c == m1, sub, per), axis=0, keepdims=True)
        m2 = jnp.max(jnp.where(sub == i1, neg_inf, c), axis=0, keepdims=True)
        grp.append(m1 + m2)
    gsel = [jnp.zeros((1, tr), jnp.bool_) for _ in range(N_GROUPS)]
    for _ in range(TOPK_GROUPS):
        gm = functools.reduce(jnp.maximum, grp)
        found = jnp.zeros((1, tr), jnp.bool_)
        for g in range(N_GROUPS):
            hit = (grp[g] == gm) & jnp.logical_not(found)
            found = found | hit
            gsel[g] = gsel[g] | hit
            grp[g] = jnp.where(hit, neg_inf, grp[g])
    vals = [jnp.where(gsel[g], slabs[g], neg_inf) for g in range(N_GROUPS)]
    eidx = [sub + g * per for g in range(N_GROUPS)]
    sel = [jnp.zeros((per, tr), jnp.bool_) for _ in range(N_GROUPS)]
    row8 = lax.broadcasted_iota(I32, (TOP_K, tr), 0)
    idx_out = jnp.zeros((TOP_K, tr), I32)
    w_out = jnp.zeros((TOP_K, tr), F32)
    wsum = jnp.zeros((1, tr), F32)
    for k in range(TOP_K):
        vm = jnp.max(functools.reduce(jnp.maximum, vals), axis=0, keepdims=True)
        cand = [jnp.where(vals[g] == vm, eidx[g], N_EXPERTS) for g in range(N_GROUPS)]
        ik = jnp.min(functools.reduce(jnp.minimum, cand), axis=0, keepdims=True)
        wk = jnp.zeros((1, tr), F32)
        for g in range(N_GROUPS):
            hit = eidx[g] == ik
            sel[g] = sel[g] | hit
            vals[g] = jnp.where(hit, neg_inf, vals[g])
            sc = scores[g * per:(g + 1) * per, :]
            wk = wk + jnp.sum(jnp.where(hit, sc, 0.0), axis=0, keepdims=True)
        idx_out = jnp.where(row8 == k, ik, idx_out)
        w_out = jnp.where(row8 == k, wk, w_out)
        wsum = wsum + wk
    for g in range(N_GROUPS):
        sel_ref[g * per:(g + 1) * per, :] = sel[g].astype(I32)
    idx_ref[...] = idx_out
    wk_ref[...] = w_out / wsum * ROUTED_SCALE


def _router(h2, w_rt, b_r, m_rows):
    d = h2.shape[1]
    tr = 256
    return pl.pallas_call(
        _router_kernel,
        grid=(m_rows // tr,),
        in_specs=[pl.BlockSpec((tr, d), lambda i: (i, 0)),
                  pl.BlockSpec((N_EXPERTS, d), lambda i: (0, 0)),
                  pl.BlockSpec((N_EXPERTS, 1), lambda i: (0, 0))],
        out_specs=[pl.BlockSpec((N_EXPERTS, tr), lambda i: (0, i)),
                   pl.BlockSpec((TOP_K, tr), lambda i: (0, i)),
                   pl.BlockSpec((TOP_K, tr), lambda i: (0, i))],
        out_shape=[jax.ShapeDtypeStruct((N_EXPERTS, m_rows), I32),
                   jax.ShapeDtypeStruct((TOP_K, m_rows), I32),
                   jax.ShapeDtypeStruct((TOP_K, m_rows), F32)],
        compiler_params=_params(("arbitrary",)),
        name="router",
    )(h2, w_rt, b_r.astype(F32).reshape(N_EXPERTS, 1))


def _row_gather(tok_ref, n_rows, src_hbm, dst_buf, sem, start):
    def body(j, carry):
        for u in range(SUBLANES):
            r = j * SUBLANES + u
            t = tok_ref[0, 0, r] if start else 0
            cp = pltpu.make_async_copy(src_hbm.at[pl.ds(t, 1), :], dst_buf.at[pl.ds(r, 1), :], sem)
            if start:
                cp.start(priority=u % 2)
            else:
                cp.wait()
        return carry
    lax.fori_loop(0, n_rows // SUBLANES, body, 0)


def _expert_kernel(nused_ref, blke_ref, tok_ref, tokn_ref, hp_ref, wgu_ref, wd_ref, sw_ref, y_ref, xbuf, sem):
    del blke_ref
    i = pl.program_id(0)
    nu = nused_ref[0]
    tm = xbuf.shape[1]
    half = xbuf.shape[2]
    slot = i % 2

    @pl.when(i == 0)
    def _():
        _row_gather(tok_ref, tm, hp_ref, xbuf.at[0], sem.at[0], True)

    @pl.when(i + 1 < nu)
    def _():
        _row_gather(tokn_ref, tm, hp_ref, xbuf.at[1 - slot], sem.at[1 - slot], True)

    @pl.when(i < nu)
    def _():
        _row_gather(tok_ref, tm, hp_ref, xbuf.at[slot], sem.at[slot], False)
        lo, hi = _unpack_halves(xbuf[slot])
        gu = (_dot(lo.astype(MXU_DTYPE), wgu_ref[0, :half, :])
              + _dot(hi.astype(MXU_DTYPE), wgu_ref[0, half:, :]))
        g = gu[:, :EXPERT_DIM]
        u = gu[:, EXPERT_DIM:]
        a = (g * _sigmoid(g) * u).astype(MXU_DTYPE)
        y = _dot(a, wd_ref[0]) * sw_ref[...]
        y_ref[...] = _pack_halves(y)

    @pl.when(i >= nu)
    def _():
        y_ref[...] = jnp.zeros_like(y_ref)


def _experts(hp, wgu, wd, n_used, blk_e, slot_tok, slot_w, n_blk):
    d2 = hp.shape[1]
    d = 2 * d2
    tm = MOE_TILE
    tok3 = slot_tok.reshape(n_blk, 1, tm)
    last = n_blk - 1
    grid_spec = pltpu.PrefetchScalarGridSpec(
        num_scalar_prefetch=2,
        grid=(n_blk,),
        in_specs=[pl.BlockSpec((1, 1, tm), lambda i, nu, be: (i, 0, 0), memory_space=pltpu.SMEM),
                  pl.BlockSpec((1, 1, tm), lambda i, nu, be: (jnp.minimum(i + 1, last), 0, 0),
                               memory_space=pltpu.SMEM),
                  pl.BlockSpec(memory_space=pl.ANY),
                  pl.BlockSpec((1, d, 2 * EXPERT_DIM), lambda i, nu, be: (be[i], 0, 0)),
                  pl.BlockSpec((1, EXPERT_DIM, d), lambda i, nu, be: (be[i], 0, 0)),
                  pl.BlockSpec((tm, 1), lambda i, nu, be: (i, 0))],
        out_specs=pl.BlockSpec((tm, d2), lambda i, nu, be: (i, 0)),
        scratch_shapes=[pltpu.VMEM((2, tm, d2), U32), pltpu.SemaphoreType.DMA((2,))],
    )
    return pl.pallas_call(
        _expert_kernel,
        grid_spec=grid_spec,
        out_shape=jax.ShapeDtypeStruct((n_blk * tm, d2), U32),
        compiler_params=_params(("arbitrary",)),
        name="routed_experts",
    )(n_used, blk_e, tok3, tok3, hp, wgu, wd, slot_w.reshape(n_blk * tm, 1))


def _shared_kernel(h_ref, wgu_ref, wd_ref, o_ref):
    gu = _dot(h_ref[...], wgu_ref[...])
    hd = gu.shape[1] // 2
    g = gu[:, :hd]
    u = gu[:, hd:]
    a = (g * _sigmoid(g) * u).astype(MXU_DTYPE)
    o_ref[...] = _dot(a, wd_ref[...]).astype(o_ref.dtype)


def _shared_expert(h2, wgu, wd, m_rows):
    d = h2.shape[1]
    tm = 256
    return pl.pallas_call(
        _shared_kernel,
        grid=(m_rows // tm,),
        in_specs=[pl.BlockSpec((tm, d), lambda i: (i, 0)),
                  pl.BlockSpec(wgu.shape, lambda i: (0, 0)),
                  pl.BlockSpec(wd.shape, lambda i: (0, 0))],
        out_specs=pl.BlockSpec((tm, d), lambda i: (i, 0)),
        out_shape=jax.ShapeDtypeStruct((m_rows, d), MXU_DTYPE),
        compiler_params=_params(("arbitrary",)),
        name="shared_expert",
    )(h2, wgu, wd)


def _combine_kernel(n_lat, sl_ref, sln_ref, y_hbm, x_ref, sh_ref, g_ref, o_ref, buf, sem):
    i = pl.program_id(0)
    nsteps = pl.num_programs(0)
    tc = x_ref.shape[0]
    nrow = buf.shape[1]
    slot = i % 2

    @pl.when(i == 0)
    def _():
        _row_gather(sl_ref, nrow, y_hbm, buf.at[0], sem.at[0], True)

    @pl.when(i + 1 < nsteps)
    def _():
        _row_gather(sln_ref, nrow, y_hbm, buf.at[1 - slot], sem.at[1 - slot], True)

    _row_gather(sl_ref, nrow, y_hbm, buf.at[slot], sem.at[slot], False)
    lo_acc = None
    hi_acc = None
    for k in range(TOP_K):
        lo, hi = _unpack_halves(buf[slot, k * tc:(k + 1) * tc, :])
        lo_acc = lo if lo_acc is None else lo_acc + lo
        hi_acc = hi if hi_acc is None else hi_acc + hi
    routed = jnp.concatenate([lo_acc, hi_acc], axis=1)
    rows = i * tc + lax.broadcasted_iota(I32, (tc, 1), 0)
    g = jnp.where(rows >= n_lat, g_ref[1:2, :], g_ref[0:1, :])
    o_ref[...] = x_ref[...] + g * (routed + sh_ref[...].astype(F32))


def _combine(y_slots, slots, x1, sh, mod, gate_blk, n_lat, m_rows):
    d = x1.shape[1]
    tc = COMBINE_TILE
    nsteps = m_rows // tc
    last = nsteps - 1
    return pl.pallas_call(
        functools.partial(_combine_kernel, n_lat),
        grid=(nsteps,),
        in_specs=[pl.BlockSpec((1, 1, TOP_K * tc), lambda i: (i, 0, 0), memory_space=pltpu.SMEM),
                  pl.BlockSpec((1, 1, TOP_K * tc), lambda i: (jnp.minimum(i + 1, last), 0, 0),
                               memory_space=pltpu.SMEM),
                  pl.BlockSpec(memory_space=pl.ANY),
                  pl.BlockSpec((tc, d), lambda i: (i, 0)),
                  pl.BlockSpec((tc, d), lambda i: (i, 0)),
                  pl.BlockSpec((2, d), lambda i: (0, gate_blk))],
        out_specs=pl.BlockSpec((tc, d), lambda i: (i, 0)),
        out_shape=jax.ShapeDtypeStruct((m_rows, d), F32),
        scratch_shapes=[pltpu.VMEM((2, TOP_K * tc, d // 2), U32), pltpu.SemaphoreType.DMA((2,))],
        compiler_params=_params(("arbitrary",)),
        name="moe_combine",
    )(slots, slots, y_slots, x1, sh, mod)


def _dispatch_tables(sel, idx, wk, m_rows):
    tm = MOE_TILE
    tk = m_rows * TOP_K
    n_blk = (tk + tm - 1) // tm + N_EXPERTS
    cs = jnp.cumsum(sel, axis=1)
    counts = cs[:, -1]
    padded = ((counts + tm - 1) // tm) * tm
    pend = jnp.cumsum(padded)
    pstart = pend - padded
    start = jnp.cumsum(counts) - counts
    dest_et = pstart[:, None] + cs - 1
    dest = jnp.take_along_axis(dest_et, idx, axis=0)
    blk_start = jnp.arange(n_blk, dtype=I32) * tm
    blk_e = jnp.minimum(jnp.sum((pend[None, :] <= blk_start[:, None]).astype(I32), axis=1), N_EXPERTS - 1)
    key = idx.T.reshape(-1) * tk + jnp.arange(tk, dtype=I32)
    f_sorted = jnp.sort(key) % tk
    s = jnp.arange(n_blk * tm, dtype=I32)
    e_s = blk_e[s // tm]
    r_s = s - pstart[e_s]
    valid = r_s < counts[e_s]
    src = jnp.clip(start[e_s] + r_s, 0, tk - 1)
    f_s = f_sorted[src]
    slot_tok = jnp.where(valid, f_s // TOP_K, 0)
    slot_w = jnp.where(valid, wk.T.reshape(-1)[f_s], 0.0)
    n_used = (pend[-1] // tm).astype(I32).reshape(1)
    tc = COMBINE_TILE
    slots = dest.reshape(TOP_K, m_rows // tc, tc).transpose(1, 0, 2).reshape(m_rows // tc, 1, TOP_K * tc)
    return n_used, blk_e, slot_tok, slot_w, slots.astype(I32), n_blk


def kernel(x, c, ctx, c_ctx, w_mod, b_mod, g_norm1, g_norm2, w_in, conv_w, diff_qnorm, diff_knorm, diff_lambda, diff_subln, swa_qnorm, swa_knorm, swa_sink, na_qnorm, na_knorm, na_rpb, w_gate, b_gate, w_branch, w_out, w_router, b_router, w_exp_gate, w_exp_up, w_exp_down, w_sh_gate, w_sh_up, w_sh_down):
    bsz, n, d = x.shape
    assert bsz == 1
    n_ctx = ctx.shape[1]
    t_all = n + n_ctx
    depth = w_in.shape[0]
    bw = d // 4
    cdt = MXU_DTYPE

    xa = jnp.concatenate([x[0], ctx[0]], axis=0)
    cc = jnp.stack([c[0], c_ctx], axis=0).astype(F32)
    rope_d = _rope_tables(n, t_all, DIFF_QK // 2)
    rope_s = _rope_tables(n, t_all, HEAD_DIM // 2)

    o_conv = 0
    o_dq, o_dk, o_dv = 3 * bw, 4 * bw, 5 * bw
    o_sq = 6 * bw
    o_sk = o_sq + bw
    o_sv = o_sk + SWA_KV_HEADS * HEAD_DIM
    o_nq = o_sv + SWA_KV_HEADS * HEAD_DIM
    o_nk, o_nv = o_nq + bw, o_nq + 2 * bw

    for l in range(depth):
        update_ctx = l < depth - 1
        nq = t_all if update_ctx else n
        lam_init = 0.8 - 0.6 * math.exp(-0.3 * l)
        mod = _modvec(cc, w_mod[l], b_mod[l])

        h = _rmsmod(xa, g_norm1[l], mod, 0, 1, n, t_all)
        p = _matmul(h, w_in[l].astype(cdt), cdt, t_all)

        a = _short_conv(p, conv_w[l].astype(F32), n, nq)
        q1z, q2z = _prep(p, o_dq, N_HEADS, diff_qnorm[l], DIFF_QK, rope_d, split=True,
                         scale=DIFF_QK ** -0.5 * math.log2(math.e))
        kd = _prep(p, o_dk, N_HEADS, diff_knorm[l], DIFF_QK, rope_d)
        qs = _prep(p, o_sq, N_HEADS, swa_qnorm[l], HEAD_DIM, rope_s)
        ks = _prep(p, o_sk, SWA_KV_HEADS, swa_knorm[l], HEAD_DIM, rope_s)
        qn = _prep(p, o_nq, N_HEADS, na_qnorm[l], HEAD_DIM, None)
        kn = _prep(p, o_nk, N_HEADS, na_knorm[l], HEAD_DIM, None)

        v_d = p[:, o_dv:o_dv + bw].reshape(t_all, N_HEADS, HEAD_DIM)
        vx = jnp.concatenate([v_d, jnp.ones_like(v_d)], axis=-1).reshape(t_all, 2 * bw)
        yb = _diff_attention(q1z, q2z, kd, vx, diff_lambda[l], diff_subln[l], lam_init, n, True)
        if update_ctx:
            yb_c = _diff_attention(q1z, q2z, kd, vx, diff_lambda[l], diff_subln[l], lam_init, n, False)
            yb = jnp.concatenate([yb, yb_c], axis=0)
        ys = _swa_attention(qs, ks, p, o_sv, swa_sink[l], n, nq)
        yn = _na_attention(qn, kn, p, o_nv, _na_bias_table(na_rpb[l]), n, nq)

        acc = _merge(h, (a, yb, ys, yn), w_gate[l].astype(cdt), b_gate[l], w_branch[l].astype(cdt), nq)
        x1 = _matmul_residual(acc, w_out[l].astype(cdt), xa, mod, 2, n, nq)

        h2, h2p = _rmsmod(x1, g_norm2[l], mod, 3, 4, n, nq, packed=True)
        sel, idx, wk = _router(h2, w_router[l].T.astype(cdt), b_router[l], nq)
        n_used, blk_e, slot_tok, slot_w, slots, n_blk = _dispatch_tables(sel, idx, wk, nq)
        wgu = jnp.concatenate([w_exp_gate[l], w_exp_up[l]], axis=-1).astype(cdt)
        y_slots = _experts(h2p, wgu, w_exp_down[l].astype(cdt), n_used, blk_e, slot_tok, slot_w, n_blk)
        wsh = jnp.concatenate([w_sh_gate[l], w_sh_up[l]], axis=-1).astype(cdt)
        sh = _shared_expert(h2, wsh, w_sh_down[l].astype(cdt), nq)
        xa = _combine(y_slots, slots, x1, sh, mod, 5, n, nq)

    return xa[:n].reshape(bsz, n, d)
```

```python
import functools
import math

import jax
import jax.numpy as jnp
from jax import lax
from jax.experimental import pallas as pl
from jax.experimental.pallas import tpu as pltpu

GRID_W = 64
HEAD_DIM = 128
DIFF_QK = 64
SWA_WINDOW = 128
SWA_BLOCK = 128
SWA_KV_HEADS = 2
N_HEADS = 8
NA_ROWS = 8
NA_COLS = 16
ROPE_BASE = 10000.0
N_EXPERTS = 64
N_GROUPS = 8
TOPK_GROUPS = 4
TOP_K = 8
EXPERT_DIM = 256
ROUTED_SCALE = 2.5
EPS = 1e-6
NEG = -1e30

LANES = 128
SUBLANES = 8
V7X_VMEM_BYTES = 64 * 1024 * 1024

MXU_DTYPE = jnp.bfloat16
MOE_TILE = 256
COMBINE_TILE = 64

F32 = jnp.float32
I32 = jnp.int32
U32 = jnp.uint32


def _params(sem, vmem_mb=48):
    return pltpu.CompilerParams(dimension_semantics=sem, vmem_limit_bytes=vmem_mb * 1024 * 1024)


def _pick(m, cands):
    for c in cands:
        if m % c == 0:
            return c
    raise ValueError(f"no tile for {m} in {cands}")


def _nt_dot(a, b):
    return lax.dot_general(a, b, (((1,), (1,)), ((), ())), preferred_element_type=F32)


def _dot(a, b):
    return jnp.dot(a, b, preferred_element_type=F32)


def _sigmoid(x):
    return 1.0 / (1.0 + jnp.exp(-x))


def _modvec_kernel(s_ref, w_ref, b_ref, o_ref):
    d, tn = w_ref.shape
    rows = []
    for r in range(2):
        c = s_ref[r]
        s = c * _sigmoid(c)
        parts = []
        for j in range(tn // LANES):
            prod = w_ref[:, j * LANES:(j + 1) * LANES] * s
            part = jnp.sum(prod.reshape(d // SUBLANES, SUBLANES, LANES), axis=0)
            parts.append(jnp.sum(part, axis=0, keepdims=True))
        rows.append(jnp.concatenate(parts, axis=1))
    is_first = lax.broadcasted_iota(I32, (2, tn), 0) == 0
    o_ref[...] = jnp.where(is_first, rows[0], rows[1]) + b_ref[...]


def _modvec(cc, w, b):
    d, n = w.shape
    tn = 512
    s_rep = jnp.broadcast_to(cc[:, :, None], (2, d, LANES))
    return pl.pallas_call(
        _modvec_kernel,
        grid=(n // tn,),
        in_specs=[pl.BlockSpec((2, d, LANES), lambda j: (0, 0, 0)),
                  pl.BlockSpec((d, tn), lambda j: (0, j)),
                  pl.BlockSpec((1, tn), lambda j: (0, j))],
        out_specs=pl.BlockSpec((2, tn), lambda j: (0, j)),
        out_shape=jax.ShapeDtypeStruct((2, n), F32),
        compiler_params=_params(("arbitrary",)),
        name="modvec",
    )(s_rep, w, b.reshape(1, n))


def _pack_halves(v):
    c = v.shape[1] // 2
    bits = lax.bitcast_convert_type(v.astype(jnp.bfloat16).astype(F32), U32)
    return (bits[:, c:] & jnp.uint32(0xFFFF0000)) | (bits[:, :c] >> 16)


def _unpack_halves(w):
    lo = lax.bitcast_convert_type(w << 16, F32)
    hi = lax.bitcast_convert_type(w & jnp.uint32(0xFFFF0000), F32)
    return lo, hi


def _rmsmod_kernel(n_lat, packed, x_ref, g_ref, sh_ref, sc_ref, o_ref, *p_ref):
    tr = x_ref.shape[0]
    x = x_ref[...]
    ms = jnp.mean(x * x, axis=-1, keepdims=True)
    y = x * lax.rsqrt(ms + EPS) * g_ref[...]
    rows = pl.program_id(0) * tr + lax.broadcasted_iota(I32, (tr, 1), 0)
    is_ctx = rows >= n_lat
    sh = jnp.where(is_ctx, sh_ref[1:2, :], sh_ref[0:1, :])
    sc = jnp.where(is_ctx, sc_ref[1:2, :], sc_ref[0:1, :])
    h = y * (1.0 + sc) + sh
    o_ref[...] = h.astype(o_ref.dtype)
    if packed:
        p_ref[0][...] = _pack_halves(h)


def _rmsmod(x, g, mod, shift_blk, scale_blk, n_lat, m_rows, packed=False):
    d = x.shape[1]
    tr = 256
    out_shape = [jax.ShapeDtypeStruct((m_rows, d), MXU_DTYPE)]
    out_specs = [pl.BlockSpec((tr, d), lambda i: (i, 0))]
    if packed:
        out_shape.append(jax.ShapeDtypeStruct((m_rows, d // 2), U32))
        out_specs.append(pl.BlockSpec((tr, d // 2), lambda i: (i, 0)))
    res = pl.pallas_call(
        functools.partial(_rmsmod_kernel, n_lat, packed),
        grid=(m_rows // tr,),
        in_specs=[pl.BlockSpec((tr, d), lambda i: (i, 0)),
                  pl.BlockSpec((1, d), lambda i: (0, 0)),
                  pl.BlockSpec((2, d), lambda i: (0, shift_blk)),
                  pl.BlockSpec((2, d), lambda i: (0, scale_blk))],
        out_specs=out_specs,
        out_shape=out_shape,
        compiler_params=_params(("arbitrary",)),
        name="rmsmod",
    )(x, g.reshape(1, d), mod, mod)
    return res if packed else res[0]


def _mm_kernel(a_ref, w_ref, o_ref):
    o_ref[...] = _dot(a_ref[...], w_ref[...]).astype(o_ref.dtype)


def _mm_res_kernel(n_lat, a_ref, w_ref, r_ref, g_ref, o_ref):
    tm = a_ref.shape[0]
    acc = _dot(a_ref[...], w_ref[...])
    rows = pl.program_id(0) * tm + lax.broadcasted_iota(I32, (tm, 1), 0)
    g = jnp.where(rows >= n_lat, g_ref[1:2, :], g_ref[0:1, :])
    o_ref[...] = r_ref[...] + g * acc


def _matmul(a, w, out_dtype, m_rows):
    k = a.shape[1]
    n = w.shape[1]
    tm = _pick(m_rows, (1280, 1024, 640, 512, 256))
    tn = 512
    return pl.pallas_call(
        _mm_kernel,
        grid=(m_rows // tm, n // tn),
        in_specs=[pl.BlockSpec((tm, k), lambda i, j: (i, 0)),
                  pl.BlockSpec((k, tn), lambda i, j: (0, j))],
        out_specs=pl.BlockSpec((tm, tn), lambda i, j: (i, j)),
        out_shape=jax.ShapeDtypeStruct((m_rows, n), out_dtype),
        compiler_params=_params(("arbitrary", "arbitrary")),
        name="matmul",
    )(a, w)


def _matmul_residual(a, w, res, mod, gate_blk, n_lat, m_rows):
    k = a.shape[1]
    n = w.shape[1]
    tm = _pick(m_rows, (1280, 1024, 640, 512, 256))
    tn = 512
    nb = n // tn
    return pl.pallas_call(
        functools.partial(_mm_res_kernel, n_lat),
        grid=(m_rows // tm, nb),
        in_specs=[pl.BlockSpec((tm, k), lambda i, j: (i, 0)),
                  pl.BlockSpec((k, tn), lambda i, j: (0, j)),
                  pl.BlockSpec((tm, tn), lambda i, j: (i, j)),
                  pl.BlockSpec((2, tn), lambda i, j: (0, gate_blk * nb + j))],
        out_specs=pl.BlockSpec((tm, tn), lambda i, j: (i, j)),
        out_shape=jax.ShapeDtypeStruct((m_rows, n), F32),
        compiler_params=_params(("arbitrary", "arbitrary")),
        name="matmul_residual",
    )(a, w, res, mod)


def _conv_kernel(n_lat, t_all, b_ref, c_ref, x_ref, cp_ref, xp_ref, cn_ref, xn_ref, w_ref, o_ref):
    tr = b_ref.shape[0]
    u = c_ref[...].astype(F32) * x_ref[...].astype(F32)
    u_prev_row = cp_ref[SUBLANES - 1:SUBLANES, :].astype(F32) * xp_ref[SUBLANES - 1:SUBLANES, :].astype(F32)
    u_next_row = cn_ref[0:1, :].astype(F32) * xn_ref[0:1, :].astype(F32)
    r = lax.broadcasted_iota(I32, (tr, 1), 0)
    tok = pl.program_id(0) * tr + r
    up = jnp.where(r == 0, u_prev_row, pltpu.roll(u, 1, 0))
    un = jnp.where(r == tr - 1, u_next_row, pltpu.roll(u, tr - 1, 0))
    up = jnp.where((tok == 0) | (tok == n_lat), 0.0, up)
    un = jnp.where((tok == n_lat - 1) | (tok == t_all - 1), 0.0, un)
    y = w_ref[0:1, :] * up + w_ref[1:2, :] * u + w_ref[2:3, :] * un
    o_ref[...] = (b_ref[...].astype(F32) * y).astype(o_ref.dtype)


def _short_conv(p, conv_w, n_lat, m_rows):
    t_all = p.shape[0]
    cw = conv_w.shape[1]
    tr = 256
    tc = 512
    nc = cw // tc
    rb = tr // SUBLANES
    last = t_all // SUBLANES - 1
    return pl.pallas_call(
        functools.partial(_conv_kernel, n_lat, t_all),
        grid=(m_rows // tr, nc),
        in_specs=[pl.BlockSpec((tr, tc), lambda i, j: (i, j)),
                  pl.BlockSpec((tr, tc), lambda i, j: (i, nc + j)),
                  pl.BlockSpec((tr, tc), lambda i, j: (i, 2 * nc + j)),
                  pl.BlockSpec((SUBLANES, tc), lambda i, j: (jnp.maximum(i * rb - 1, 0), nc + j)),
                  pl.BlockSpec((SUBLANES, tc), lambda i, j: (jnp.maximum(i * rb - 1, 0), 2 * nc + j)),
                  pl.BlockSpec((SUBLANES, tc), lambda i, j: (jnp.minimum((i + 1) * rb, last), nc + j)),
                  pl.BlockSpec((SUBLANES, tc), lambda i, j: (jnp.minimum((i + 1) * rb, last), 2 * nc + j)),
                  pl.BlockSpec((3, tc), lambda i, j: (0, j))],
        out_specs=pl.BlockSpec((tr, tc), lambda i, j: (i, j)),
        out_shape=jax.ShapeDtypeStruct((m_rows, cw), MXU_DTYPE),
        compiler_params=_params(("arbitrary", "arbitrary")),
        name="short_conv",
    )(p, p, p, p, p, p, p, conv_w)


def _prep_kernel(n_heads, group, half, split, scale, x_ref, g_ref, *rest):
    if half:
        cos_ref, sa_ref, sb_ref = rest[:3]
        outs = rest[3:]
    else:
        outs = rest
    lane = lax.broadcasted_iota(I32, (1, LANES), 1)
    lo = lane < DIFF_QK
    g = g_ref[...]
    for h in range(n_heads):
        x = x_ref[:, h * LANES:(h + 1) * LANES].astype(F32)
        sq = x * x
        if group == LANES:
            ms = jnp.mean(sq, axis=-1, keepdims=True)
        else:
            s_lo = jnp.sum(jnp.where(lo, sq, 0.0), axis=-1, keepdims=True)
            s_hi = jnp.sum(jnp.where(lo, 0.0, sq), axis=-1, keepdims=True)
            ms = jnp.where(lo, s_lo, s_hi) * (1.0 / group)
        y = x * lax.rsqrt(ms + EPS) * g
        if half:
            y = (y * cos_ref[...] + pltpu.roll(y, LANES - half, 1) * sa_ref[...]
                 + pltpu.roll(y, half, 1) * sb_ref[...])
        if scale != 1.0:
            y = y * scale
        if split:
            outs[0][:, h * LANES:(h + 1) * LANES] = jnp.where(lo, y, 0.0).astype(outs[0].dtype)
            outs[1][:, h * LANES:(h + 1) * LANES] = jnp.where(lo, 0.0, y).astype(outs[1].dtype)
        else:
            outs[0][:, h * LANES:(h + 1) * LANES] = y.astype(outs[0].dtype)


def _prep(p, col0, n_heads, gain, group, rope, split=False, scale=1.0):
    t_all = p.shape[0]
    tr = _pick(t_all, (640, 256))
    hpb = _pick(math.gcd(col0 // LANES, n_heads), (8, 4, 2, 1))
    w = hpb * LANES
    cb = col0 // w
    g = jnp.tile(gain.astype(F32), LANES // gain.shape[0]).reshape(1, LANES)
    in_specs = [pl.BlockSpec((tr, w), lambda i, j: (i, cb + j)),
                pl.BlockSpec((1, LANES), lambda i, j: (0, 0))]
    args = [p, g]
    half = 0
    if rope is not None:
        half = rope[3]
        in_specs += [pl.BlockSpec((tr, LANES), lambda i, j: (i, 0))] * 3
        args += list(rope[:3])
    n_out = 2 if split else 1
    res = pl.pallas_call(
        functools.partial(_prep_kernel, hpb, group, half, split, scale),
        grid=(t_all // tr, n_heads // hpb),
        in_specs=in_specs,
        out_specs=[pl.BlockSpec((tr, w), lambda i, j: (i, j))] * n_out,
        out_shape=[jax.ShapeDtypeStruct((t_all, n_heads * LANES), MXU_DTYPE)] * n_out,
        compiler_params=_params(("arbitrary", "arbitrary")),
        name="qk_prep",
    )(*args)
    return res if split else res[0]


def _rope_tables(n_lat, t_all, seg):
    half = seg // 2
    n_rows = n_lat // GRID_W
    lane = jnp.arange(LANES)
    freqs = ROPE_BASE ** (-jnp.arange(0, seg, 2, dtype=F32) / seg)
    f_lane = freqs[lane % half]
    use_col = ((lane // seg) % 2 == 1)[None, None, :]
    ang_r = jnp.arange(n_rows, dtype=F32)[:, None] * f_lane[None, :]
    ang_c = jnp.arange(GRID_W, dtype=F32)[:, None] * f_lane[None, :]

    def expand(fn):
        full = jnp.where(use_col, fn(ang_c)[None, :, :], fn(ang_r)[:, None, :])
        return full.reshape(n_lat, LANES)

    cos = expand(jnp.cos)
    sin = expand(jnp.sin)
    first = (lane % seg) < half
    sa = jnp.where(first[None, :], -sin, 0.0)
    sb = jnp.where(first[None, :], 0.0, sin)
    pad = t_all - n_lat
    cos = jnp.concatenate([cos, jnp.ones((pad, LANES), F32)], axis=0)
    sa = jnp.concatenate([sa, jnp.zeros((pad, LANES), F32)], axis=0)
    sb = jnp.concatenate([sb, jnp.zeros((pad, LANES), F32)], axis=0)
    return cos, sa, sb, half


DIFF_TQ = 512
DIFF_TK = 512
DIFF_GROUP = 4


def _diff_kernel(n_pairs, tkc, lam_init, q1_ref, q2_ref, k_ref, vx_ref, dl_ref, gs_ref, o_ref, s_scr):
    tq = q1_ref.shape[0]
    t_all = k_ref.shape[0]
    n_lat = 2 * n_pairs * tkc
    q1 = q1_ref[...]
    q2 = q2_ref[...]

    def scores(kc):
        return _nt_dot(q1, kc), _nt_dot(q2, kc)

    def fold(s, vc, m, a):
        mn = jnp.maximum(m, jnp.max(s, axis=-1, keepdims=True))
        alpha = jnp.exp2(m - mn)
        p = jnp.exp2(s - mn).astype(vc.dtype)
        return mn, alpha * a + _dot(p, vc)

    def lat_scores_to(buf, c):
        off = pl.multiple_of(c * tkc, tkc)
        s1, s2 = scores(k_ref[pl.ds(off, tkc), :])
        s_scr[buf, 0] = s1
        s_scr[buf, 1] = s2

    def lat_fold_from(buf, c, carry):
        m1, a1, m2, a2 = carry
        off = pl.multiple_of(c * tkc, tkc)
        vc = vx_ref[pl.ds(off, tkc), :]
        m1, a1 = fold(s_scr[buf, 0], vc, m1, a1)
        m2, a2 = fold(s_scr[buf, 1], vc, m2, a2)
        return m1, a1, m2, a2

    minit = jnp.full((tq, 1), -jnp.inf, F32)
    ainit = jnp.zeros((tq, 2 * HEAD_DIM), F32)
    carry = (minit, ainit, minit, ainit)
    n_chunks = 2 * n_pairs
    group = _pick(n_chunks, (DIFF_GROUP, 2)) if n_pairs > 0 else 0
    if n_pairs > 0:
        lat_scores_to(0, 0)

        def run_group(c0, carry, last):
            for u in range(group):
                if not (last and u == group - 1):
                    lat_scores_to((u + 1) % 2, c0 + u + 1)
                    carry = lat_fold_from(u % 2, c0 + u, carry)
            return carry

        carry = lax.fori_loop(0, n_chunks // group - 1,
                              lambda j, cr: run_group(j * group, cr, False), carry)
        carry = run_group(n_chunks - group, carry, True)
    s1c, s2c = scores(k_ref[n_lat:t_all, :])
    if n_pairs > 0:
        carry = lat_fold_from(1, n_chunks - 1, carry)
    m1, a1, m2, a2 = carry
    vcx = vx_ref[n_lat:t_all, :]
    m1, a1 = fold(s1c, vcx, m1, a1)
    m2, a2 = fold(s2c, vcx, m2, a2)

    dl = dl_ref[...]
    lam = (jnp.exp(jnp.sum(dl[0:1, :] * dl[1:2, :], axis=-1, keepdims=True))
           - jnp.exp(jnp.sum(dl[2:3, :] * dl[3:4, :], axis=-1, keepdims=True)) + lam_init)
    o = (a1[:, :HEAD_DIM] / a1[:, HEAD_DIM:HEAD_DIM + 1]
         - lam * (a2[:, :HEAD_DIM] / a2[:, HEAD_DIM:HEAD_DIM + 1]))
    ms = jnp.mean(o * o, axis=-1, keepdims=True)
    o = o * lax.rsqrt(ms + EPS) * gs_ref[...] * (1.0 - lam_init)
    o_ref[...] = o.astype(o_ref.dtype)


def _diff_attention(q1z, q2z, kd, vx, diff_lambda, subln, lam_init, n_lat, latent):
    t_all = kd.shape[0]
    n_ctx = t_all - n_lat
    if latent:
        tq, tkc, rows, blk0 = DIFF_TQ, DIFF_TK, n_lat, 0
        assert n_lat % (2 * tkc) == 0 and n_lat % tq == 0
        n_pairs = n_lat // (2 * tkc)
    else:
        tq, tkc, rows, blk0, n_pairs = n_ctx, DIFF_TK, n_ctx, n_lat // n_ctx, 0
        assert n_lat % n_ctx == 0
    if n_pairs == 0:
        k_spec = pl.BlockSpec((n_ctx, LANES), lambda h, i: (n_lat // n_ctx, h))
        v_spec = pl.BlockSpec((n_ctx, 2 * LANES), lambda h, i: (n_lat // n_ctx, h))
    else:
        k_spec = pl.BlockSpec((t_all, LANES), lambda h, i: (0, h))
        v_spec = pl.BlockSpec((t_all, 2 * LANES), lambda h, i: (0, h))
    return pl.pallas_call(
        functools.partial(_diff_kernel, n_pairs, tkc, lam_init),
        grid=(N_HEADS, rows // tq),
        in_specs=[pl.BlockSpec((tq, LANES), lambda h, i: (blk0 + i, h)),
                  pl.BlockSpec((tq, LANES), lambda h, i: (blk0 + i, h)),
                  k_spec, v_spec,
                  pl.BlockSpec((4, DIFF_QK), lambda h, i: (0, 0)),
                  pl.BlockSpec((1, HEAD_DIM), lambda h, i: (0, 0))],
        out_specs=pl.BlockSpec((tq, LANES), lambda h, i: (i, h)),
        out_shape=jax.ShapeDtypeStruct((rows, N_HEADS * HEAD_DIM), MXU_DTYPE),
        scratch_shapes=[pltpu.VMEM((2, 2, tq, tkc), F32)],
        compiler_params=_params(("arbitrary", "arbitrary")),
        name="diff_attention" if latent else "diff_attention_ctx",
    )(q1z, q2z, kd, vx, diff_lambda.astype(F32), subln.astype(F32).reshape(1, HEAD_DIM))


def _swa_kernel(n_lat, t_all, q_ref, k_ref, v_ref, sink_ref, o_ref):
    w = SWA_BLOCK
    g = q_ref.shape[1] // HEAD_DIM
    b = pl.program_id(1)
    reach = jnp.where(b * w < n_lat, SWA_WINDOW, -1)
    scale = HEAD_DIM ** -0.5
    start = jnp.clip((b - 1) * w, 0, n_lat - 3 * w)
    start = pl.multiple_of(start, w)
    kw = k_ref[pl.ds(start, 3 * w), :]
    vw = v_ref[pl.ds(start, 3 * w), :]
    kc = k_ref[n_lat:t_all, :]
    vc = v_ref[n_lat:t_all, :]
    q = jnp.concatenate([q_ref[:, j * HEAD_DIM:(j + 1) * HEAD_DIM] for j in range(g)], axis=0)
    s_w = _nt_dot(q, kw) * scale
    s_c = _nt_dot(q, kc) * scale
    qpos = b * w + lax.broadcasted_iota(I32, (w, 3 * w), 0)
    kpos = start + lax.broadcasted_iota(I32, (w, 3 * w), 1)
    mask = jnp.abs(qpos - kpos) <= reach
    mask = jnp.concatenate([mask] * g, axis=0)
    s_w = jnp.where(mask, s_w, NEG)
    sink = jnp.concatenate([jnp.broadcast_to(sink_ref[0, :, j:j + 1], (w, 1)) for j in range(g)], axis=0)
    m = jnp.maximum(jnp.maximum(jnp.max(s_w, axis=-1, keepdims=True), jnp.max(s_c, axis=-1, keepdims=True)), sink)
    p_w = jnp.exp(s_w - m)
    p_c = jnp.exp(s_c - m)
    l = jnp.sum(p_w, axis=-1, keepdims=True) + jnp.sum(p_c, axis=-1, keepdims=True) + jnp.exp(sink - m)
    inv = 1.0 / l
    o = _dot((p_w * inv).astype(vw.dtype), vw) + _dot((p_c * inv).astype(vc.dtype), vc)
    for j in range(g):
        o_ref[:, j * HEAD_DIM:(j + 1) * HEAD_DIM] = o[j * w:(j + 1) * w, :].astype(o_ref.dtype)


def _swa_attention(qs, ks, p, v_col0, sink, n_lat, nq):
    t_all = ks.shape[0]
    g = N_HEADS // SWA_KV_HEADS
    vb = v_col0 // LANES
    sink3 = sink.astype(F32).reshape(SWA_KV_HEADS, 1, g)
    return pl.pallas_call(
        functools.partial(_swa_kernel, n_lat, t_all),
        grid=(SWA_KV_HEADS, nq // SWA_BLOCK),
        in_specs=[pl.BlockSpec((SWA_BLOCK, g * HEAD_DIM), lambda kv, b: (b, kv)),
                  pl.BlockSpec((t_all, LANES), lambda kv, b: (0, kv)),
                  pl.BlockSpec((t_all, LANES), lambda kv, b: (0, vb + kv)),
                  pl.BlockSpec((1, 1, g), lambda kv, b: (kv, 0, 0))],
        out_specs=pl.BlockSpec((SWA_BLOCK, g * HEAD_DIM), lambda kv, b: (b, kv)),
        out_shape=jax.ShapeDtypeStruct((nq, N_HEADS * HEAD_DIM), MXU_DTYPE),
        compiler_params=_params(("arbitrary", "arbitrary")),
        name="swa_attention",
    )(qs, ks, p, sink3)


NA_ROWS_PER_STEP = 4


def _na_kernel(n_lat, t_all, q_ref, k_ref, v_ref, bias_ref, o_ref):
    n_rows = n_lat // GRID_W
    span = NA_ROWS * GRID_W
    scale = HEAD_DIM ** -0.5
    kc = k_ref[n_lat:t_all, :]
    vc = v_ref[n_lat:t_all, :]
    for j in range(NA_ROWS_PER_STEP):
        r = pl.program_id(1) * NA_ROWS_PER_STEP + j
        is_lat = r < n_rows
        r0 = jnp.clip(r - NA_ROWS // 2, 0, n_rows - NA_ROWS)
        delta = jnp.where(is_lat, r0 - r + (NA_ROWS - 1), NA_ROWS)
        off = pl.multiple_of(r0 * GRID_W, GRID_W)
        kn = k_ref[pl.ds(off, span), :]
        vn = v_ref[pl.ds(off, span), :]
        q = q_ref[j * GRID_W:(j + 1) * GRID_W, :]
        s_n = _nt_dot(q, kn) * scale + bias_ref[0, delta]
        s_c = _nt_dot(q, kc) * scale
        m = jnp.maximum(jnp.max(s_n, axis=-1, keepdims=True), jnp.max(s_c, axis=-1, keepdims=True))
        p_n = jnp.exp(s_n - m)
        p_c = jnp.exp(s_c - m)
        inv = 1.0 / (jnp.sum(p_n, axis=-1, keepdims=True) + jnp.sum(p_c, axis=-1, keepdims=True))
        o = _dot((p_n * inv).astype(vn.dtype), vn) + _dot((p_c * inv).astype(vc.dtype), vc)
        o_ref[j * GRID_W:(j + 1) * GRID_W, :] = o.astype(o_ref.dtype)


def _na_bias_table(rpb):
    h = rpb.shape[0]
    c = jnp.arange(GRID_W)
    c0 = jnp.clip(c - NA_COLS // 2, 0, GRID_W - NA_COLS)
    kc = jnp.arange(GRID_W)
    inwin = (kc[None, :] >= c0[:, None]) & (kc[None, :] < c0[:, None] + NA_COLS)
    col_off = jnp.clip(kc[None, :] - c[:, None] + NA_COLS - 1, 0, 2 * NA_COLS - 2)
    toe = jnp.where(inwin[None, None], rpb.astype(F32)[:, :, col_off], NEG)
    tab = jnp.stack([toe[:, dl:dl + NA_ROWS] for dl in range(NA_ROWS)], axis=1)
    tab = tab.transpose(0, 1, 3, 2, 4).reshape(h, NA_ROWS, GRID_W, NA_ROWS * GRID_W)
    masked = jnp.full((h, 1, GRID_W, NA_ROWS * GRID_W), NEG, F32)
    return jnp.concatenate([tab, masked], axis=1)


def _na_attention(qn, kn, p, v_col0, bias, n_lat, nq):
    t_all = kn.shape[0]
    vb = v_col0 // LANES
    tq = NA_ROWS_PER_STEP * GRID_W
    return pl.pallas_call(
        functools.partial(_na_kernel, n_lat, t_all),
        grid=(N_HEADS, nq // tq),
        in_specs=[pl.BlockSpec((tq, LANES), lambda h, i: (i, h)),
                  pl.BlockSpec((t_all, LANES), lambda h, i: (0, h)),
                  pl.BlockSpec((t_all, LANES), lambda h, i: (0, vb + h)),
                  pl.BlockSpec((1, NA_ROWS + 1, GRID_W, NA_ROWS * GRID_W), lambda h, i: (h, 0, 0, 0))],
        out_specs=pl.BlockSpec((tq, LANES), lambda h, i: (i, h)),
        out_shape=jax.ShapeDtypeStruct((nq, N_HEADS * HEAD_DIM), MXU_DTYPE),
        compiler_params=_params(("arbitrary", "arbitrary")),
        name="na_attention",
    )(qn, kn, p, bias)


def _merge_kernel(h_ref, y0_ref, y1_ref, y2_ref, y3_ref, wg_ref, bg_ref, wb_ref, o_ref):
    h = h_ref[...]
    acc = None
    for i, y_ref in enumerate((y0_ref, y1_ref, y2_ref, y3_ref)):
        gate = _sigmoid(_dot(h, wg_ref[i]) + bg_ref[i])
        term = gate * _dot(y_ref[...], wb_ref[i])
        acc = term if acc is None else acc + term
    o_ref[...] = acc.astype(o_ref.dtype)


def _merge(h, ys, wg, bg, wb, m_rows):
    d = h.shape[1]
    bw = ys[0].shape[1]
    tm = _pick(m_rows, (1280, 1024, 640, 512, 256))
    tn = 256
    once = pl.Buffered(1)
    return pl.pallas_call(
        _merge_kernel,
        grid=(m_rows // tm, d // tn),
        in_specs=[pl.BlockSpec((tm, d), lambda i, j: (i, 0), pipeline_mode=once)]
        + [pl.BlockSpec((tm, bw), lambda i, j: (i, 0), pipeline_mode=once)] * 4
        + [pl.BlockSpec((4, d, tn), lambda i, j: (0, 0, j)),
           pl.BlockSpec((4, 1, tn), lambda i, j: (0, 0, j)),
           pl.BlockSpec((4, bw, tn), lambda i, j: (0, 0, j))],
        out_specs=pl.BlockSpec((tm, tn), lambda i, j: (i, j)),
        out_shape=jax.ShapeDtypeStruct((m_rows, d), MXU_DTYPE),
        compiler_params=_params(("arbitrary", "arbitrary"), vmem_mb=56),
        name="merge",
    )(h, *ys, wg, bg.astype(F32).reshape(4, 1, d), wb)


def _router_kernel(h_ref, w_ref, b_ref, sel_ref, idx_ref, wk_ref):
    tr = h_ref.shape[0]
    per = N_EXPERTS // N_GROUPS
    scores = _sigmoid(_nt_dot(w_ref[...], h_ref[...]))
    choice = scores + b_ref[...]
    sub = lax.broadcasted_iota(I32, (per, tr), 0)
    neg_inf = -jnp.inf
    slabs = [choice[g * per:(g + 1) * per, :] for g in range(N_GROUPS)]
    grp = []
    for c in slabs:
        m1 = jnp.max(c, axis=0, keepdims=True)
        i1 = jnp.min(jnp.where(c == m1, sub, per), axis=0, keepdims=True)
        m2 = jnp.max(jnp.where(sub == i1, neg_inf, c), axis=0, keepdims=True)
        grp.append(m1 + m2)
    gsel = [jnp.zeros((1, tr), jnp.bool_) for _ in range(N_GROUPS)]
    for _ in range(TOPK_GROUPS):
        gm = functools.reduce(jnp.maximum, grp)
        found = jnp.zeros((1, tr), jnp.bool_)
        for g in range(N_GROUPS):
            hit = (grp[g] == gm) & jnp.logical_not(found)
            found = found | hit
            gsel[g] = gsel[g] | hit
            grp[g] = jnp.where(hit, neg_inf, grp[g])
    vals = [jnp.where(gsel[g], slabs[g], neg_inf) for g in range(N_GROUPS)]
    eidx = [sub + g * per for g in range(N_GROUPS)]
    sel = [jnp.zeros((per, tr), jnp.bool_) for _ in range(N_GROUPS)]
    row8 = lax.broadcasted_iota(I32, (TOP_K, tr), 0)
    idx_out = jnp.zeros((TOP_K, tr), I32)
    w_out = jnp.zeros((TOP_K, tr), F32)
    wsum = jnp.zeros((1, tr), F32)
    for k in range(TOP_K):
        vm = jnp.max(functools.reduce(jnp.maximum, vals), axis=0, keepdims=True)
        cand = [jnp.where(vals[g] == vm, eidx[g], N_EXPERTS) for g in range(N_GROUPS)]
        ik = jnp.min(functools.reduce(jnp.minimum, cand), axis=0, keepdims=True)
        wk = jnp.zeros((1, tr), F32)
        for g in range(N_GROUPS):
            hit = eidx[g] == ik
            sel[g] = sel[g] | hit
            vals[g] = jnp.where(hit, neg_inf, vals[g])
            sc = scores[g * per:(g + 1) * per, :]
            wk = wk + jnp.sum(jnp.where(hit, sc, 0.0), axis=0, keepdims=True)
        idx_out = jnp.where(row8 == k, ik, idx_out)
        w_out = jnp.where(row8 == k, wk, w_out)
        wsum = wsum + wk
    for g in range(N_GROUPS):
        sel_ref[g * per:(g + 1) * per, :] = sel[g].astype(I32)
    idx_ref[...] = idx_out
    wk_ref[...] = w_out / wsum * ROUTED_SCALE


def _router(h2, w_rt, b_r, m_rows):
    d = h2.shape[1]
    tr = 256
    return pl.pallas_call(
        _router_kernel,
        grid=(m_rows // tr,),
        in_specs=[pl.BlockSpec((tr, d), lambda i: (i, 0)),
                  pl.BlockSpec((N_EXPERTS, d), lambda i: (0, 0)),
                  pl.BlockSpec((N_EXPERTS, 1), lambda i: (0, 0))],
        out_specs=[pl.BlockSpec((N_EXPERTS, tr), lambda i: (0, i)),
                   pl.BlockSpec((TOP_K, tr), lambda i: (0, i)),
                   pl.BlockSpec((TOP_K, tr), lambda i: (0, i))],
        out_shape=[jax.ShapeDtypeStruct((N_EXPERTS, m_rows), I32),
                   jax.ShapeDtypeStruct((TOP_K, m_rows), I32),
                   jax.ShapeDtypeStruct((TOP_K, m_rows), F32)],
        compiler_params=_params(("arbitrary",)),
        name="router",
    )(h2, w_rt, b_r.astype(F32).reshape(N_EXPERTS, 1))


def _row_gather(tok_ref, n_rows, src_hbm, dst_buf, sem, start):
    def body(j, carry):
        for u in range(SUBLANES):
            r = j * SUBLANES + u
            t = tok_ref[0, 0, r] if start else 0
            cp = pltpu.make_async_copy(src_hbm.at[pl.ds(t, 1), :], dst_buf.at[pl.ds(r, 1), :], sem)
            if start:
                cp.start()
            else:
                cp.wait()
        return carry
    lax.fori_loop(0, n_rows // SUBLANES, body, 0)


def _expert_kernel(nused_ref, blke_ref, tok_ref, tokn_ref, hp_ref, wg_ref, wu_ref, wd_ref, sw_ref, y_ref,
                   xbuf, sem, wgu_c, wd_c):
    i = pl.program_id(0)
    nu = nused_ref[0]
    tm = xbuf.shape[1]
    half = xbuf.shape[2]
    slot = i % 2

    @pl.when(i == 0)
    def _():
        _row_gather(tok_ref, tm, hp_ref, xbuf.at[0], sem.at[0], True)

    @pl.when(i + 1 < nu)
    def _():
        _row_gather(tokn_ref, tm, hp_ref, xbuf.at[1 - slot], sem.at[1 - slot], True)

    @pl.when((i < nu) & ((i == 0) | (blke_ref[i] != blke_ref[jnp.maximum(i - 1, 0)])))
    def _():
        wgu_c[:, :EXPERT_DIM] = wg_ref[0].astype(wgu_c.dtype)
        wgu_c[:, EXPERT_DIM:] = wu_ref[0].astype(wgu_c.dtype)
        wd_c[...] = wd_ref[0].astype(wd_c.dtype)

    @pl.when(i < nu)
    def _():
        _row_gather(tok_ref, tm, hp_ref, xbuf.at[slot], sem.at[slot], False)
        lo, hi = _unpack_halves(xbuf[slot])
        gu = (_dot(lo.astype(MXU_DTYPE), wgu_c[:half, :])
              + _dot(hi.astype(MXU_DTYPE), wgu_c[half:, :]))
        g = gu[:, :EXPERT_DIM]
        u = gu[:, EXPERT_DIM:]
        a = (g * _sigmoid(g) * u).astype(MXU_DTYPE)
        y = _dot(a, wd_c[...]) * sw_ref[...]
        y_ref[...] = _pack_halves(y)

    @pl.when(i >= nu)
    def _():
        y_ref[...] = jnp.zeros_like(y_ref)


def _experts(hp, wg, wu, wd, n_used, blk_e, slot_tok, slot_w, n_blk):
    d2 = hp.shape[1]
    d = 2 * d2
    tm = MOE_TILE
    tok3 = slot_tok.reshape(n_blk, 1, tm)
    last = n_blk - 1
    grid_spec = pltpu.PrefetchScalarGridSpec(
        num_scalar_prefetch=2,
        grid=(n_blk,),
        in_specs=[pl.BlockSpec((1, 1, tm), lambda i, nu, be: (i, 0, 0), memory_space=pltpu.SMEM),
                  pl.BlockSpec((1, 1, tm), lambda i, nu, be: (jnp.minimum(i + 1, last), 0, 0),
                               memory_space=pltpu.SMEM),
                  pl.BlockSpec(memory_space=pl.ANY),
                  pl.BlockSpec((1, d, EXPERT_DIM), lambda i, nu, be: (be[i], 0, 0)),
                  pl.BlockSpec((1, d, EXPERT_DIM), lambda i, nu, be: (be[i], 0, 0)),
                  pl.BlockSpec((1, EXPERT_DIM, d), lambda i, nu, be: (be[i], 0, 0)),
                  pl.BlockSpec((tm, 1), lambda i, nu, be: (i, 0))],
        out_specs=pl.BlockSpec((tm, d2), lambda i, nu, be: (i, 0)),
        scratch_shapes=[pltpu.VMEM((2, tm, d2), U32), pltpu.SemaphoreType.DMA((2,)),
                        pltpu.VMEM((d, 2 * EXPERT_DIM), MXU_DTYPE), pltpu.VMEM((EXPERT_DIM, d), MXU_DTYPE)],
    )
    return pl.pallas_call(
        _expert_kernel,
        grid_spec=grid_spec,
        out_shape=jax.ShapeDtypeStruct((n_blk * tm, d2), U32),
        compiler_params=_params(("arbitrary",), vmem_mb=56),
        name="routed_experts",
    )(n_used, blk_e, tok3, tok3, hp, wg, wu, wd, slot_w.reshape(n_blk * tm, 1))


def _shared_kernel(h_ref, wgu_ref, wd_ref, o_ref):
    gu = _dot(h_ref[...], wgu_ref[...])
    hd = gu.shape[1] // 2
    g = gu[:, :hd]
    u = gu[:, hd:]
    a = (g * _sigmoid(g) * u).astype(MXU_DTYPE)
    o_ref[...] = _dot(a, wd_ref[...]).astype(o_ref.dtype)


def _shared_expert(h2, wgu, wd, m_rows):
    d = h2.shape[1]
    tm = 256
    return pl.pallas_call(
        _shared_kernel,
        grid=(m_rows // tm,),
        in_specs=[pl.BlockSpec((tm, d), lambda i: (i, 0)),
                  pl.BlockSpec(wgu.shape, lambda i: (0, 0)),
                  pl.BlockSpec(wd.shape, lambda i: (0, 0))],
        out_specs=pl.BlockSpec((tm, d), lambda i: (i, 0)),
        out_shape=jax.ShapeDtypeStruct((m_rows, d), MXU_DTYPE),
        compiler_params=_params(("arbitrary",)),
        name="shared_expert",
    )(h2, wgu, wd)


def _combine_kernel(n_lat, sl_ref, sln_ref, y_hbm, x_ref, sh_ref, g_ref, o_ref, buf, sem):
    i = pl.program_id(0)
    nsteps = pl.num_programs(0)
    tc = x_ref.shape[0]
    nrow = buf.shape[1]
    slot = i % 2

    @pl.when(i == 0)
    def _():
        _row_gather(sl_ref, nrow, y_hbm, buf.at[0], sem.at[0], True)

    @pl.when(i + 1 < nsteps)
    def _():
        _row_gather(sln_ref, nrow, y_hbm, buf.at[1 - slot], sem.at[1 - slot], True)

    _row_gather(sl_ref, nrow, y_hbm, buf.at[slot], sem.at[slot], False)
    lo_acc = None
    hi_acc = None
    for k in range(TOP_K):
        lo, hi = _unpack_halves(buf[slot, k * tc:(k + 1) * tc, :])
        lo_acc = lo if lo_acc is None else lo_acc + lo
        hi_acc = hi if hi_acc is None else hi_acc + hi
    routed = jnp.concatenate([lo_acc, hi_acc], axis=1)
    rows = i * tc + lax.broadcasted_iota(I32, (tc, 1), 0)
    g = jnp.where(rows >= n_lat, g_ref[1:2, :], g_ref[0:1, :])
    o_ref[...] = x_ref[...] + g * (routed + sh_ref[...].astype(F32))


def _combine(y_slots, slots, x1, sh, mod, gate_blk, n_lat, m_rows):
    d = x1.shape[1]
    tc = COMBINE_TILE
    nsteps = m_rows // tc
    last = nsteps - 1
    return pl.pallas_call(
        functools.partial(_combine_kernel, n_lat),
        grid=(nsteps,),
        in_specs=[pl.BlockSpec((1, 1, TOP_K * tc), lambda i: (i, 0, 0), memory_space=pltpu.SMEM),
                  pl.BlockSpec((1, 1, TOP_K * tc), lambda i: (jnp.minimum(i + 1, last), 0, 0),
                               memory_space=pltpu.SMEM),
                  pl.BlockSpec(memory_space=pl.ANY),
                  pl.BlockSpec((tc, d), lambda i: (i, 0)),
                  pl.BlockSpec((tc, d), lambda i: (i, 0)),
                  pl.BlockSpec((2, d), lambda i: (0, gate_blk))],
        out_specs=pl.BlockSpec((tc, d), lambda i: (i, 0)),
        out_shape=jax.ShapeDtypeStruct((m_rows, d), F32),
        scratch_shapes=[pltpu.VMEM((2, TOP_K * tc, d // 2), U32), pltpu.SemaphoreType.DMA((2,))],
        compiler_params=_params(("arbitrary",)),
        name="moe_combine",
    )(slots, slots, y_slots, x1, sh, mod)


def _dispatch_tables(sel, idx, wk, m_rows):
    tm = MOE_TILE
    tk = m_rows * TOP_K
    n_blk = (tk + tm - 1) // tm + N_EXPERTS
    cs = jnp.cumsum(sel, axis=1)
    counts = cs[:, -1]
    padded = ((counts + tm - 1) // tm) * tm
    pend = jnp.cumsum(padded)
    pstart = pend - padded
    start = jnp.cumsum(counts) - counts
    dest_et = pstart[:, None] + cs - 1
    dest = jnp.take_along_axis(dest_et, idx, axis=0)
    blk_start = jnp.arange(n_blk, dtype=I32) * tm
    blk_e = jnp.minimum(jnp.sum((pend[None, :] <= blk_start[:, None]).astype(I32), axis=1), N_EXPERTS - 1)
    key = idx.T.reshape(-1) * tk + jnp.arange(tk, dtype=I32)
    key_s, w_s = lax.sort((key, wk.T.reshape(-1)), num_keys=1)
    tok_s = jnp.concatenate([(key_s % tk) // TOP_K, jnp.zeros((tm,), I32)])
    w_s = jnp.concatenate([w_s, jnp.zeros((tm,), F32)])
    off_b = blk_start - pstart[blk_e]
    base_b = jnp.clip(start[blk_e] + off_b, 0, tk)
    cnt_b = counts[blk_e] - off_b
    run = jax.vmap(lambda a, b: lax.dynamic_slice(a, (b,), (tm,)), in_axes=(None, 0))
    valid = jnp.arange(tm, dtype=I32)[None, :] < cnt_b[:, None]
    slot_tok = jnp.where(valid, run(tok_s, base_b), 0).reshape(-1)
    slot_w = jnp.where(valid, run(w_s, base_b), 0.0).reshape(-1)
    n_used = (pend[-1] // tm).astype(I32).reshape(1)
    tc = COMBINE_TILE
    slots = dest.reshape(TOP_K, m_rows // tc, tc).transpose(1, 0, 2).reshape(m_rows // tc, 1, TOP_K * tc)
    return n_used, blk_e, slot_tok, slot_w, slots.astype(I32), n_blk


def kernel(x, c, ctx, c_ctx, w_mod, b_mod, g_norm1, g_norm2, w_in, conv_w, diff_qnorm, diff_knorm, diff_lambda, diff_subln, swa_qnorm, swa_knorm, swa_sink, na_qnorm, na_knorm, na_rpb, w_gate, b_gate, w_branch, w_out, w_router, b_router, w_exp_gate, w_exp_up, w_exp_down, w_sh_gate, w_sh_up, w_sh_down):
    bsz, n, d = x.shape
    assert bsz == 1
    n_ctx = ctx.shape[1]
    t_all = n + n_ctx
    depth = w_in.shape[0]
    bw = d // 4
    cdt = MXU_DTYPE

    xa = jnp.concatenate([x[0], ctx[0]], axis=0)
    cc = jnp.stack([c[0], c_ctx], axis=0).astype(F32)
    rope_d = _rope_tables(n, t_all, DIFF_QK // 2)
    rope_s = _rope_tables(n, t_all, HEAD_DIM // 2)

    o_conv = 0
    o_dq, o_dk, o_dv = 3 * bw, 4 * bw, 5 * bw
    o_sq = 6 * bw
    o_sk = o_sq + bw
    o_sv = o_sk + SWA_KV_HEADS * HEAD_DIM
    o_nq = o_sv + SWA_KV_HEADS * HEAD_DIM
    o_nk, o_nv = o_nq + bw, o_nq + 2 * bw

    for l in range(depth):
        update_ctx = l < depth - 1
        nq = t_all if update_ctx else n
        lam_init = 0.8 - 0.6 * math.exp(-0.3 * l)
        mod = _modvec(cc, w_mod[l], b_mod[l])

        h = _rmsmod(xa, g_norm1[l], mod, 0, 1, n, t_all)
        p = _matmul(h, w_in[l].astype(cdt), cdt, t_all)

        a = _short_conv(p, conv_w[l].astype(F32), n, nq)
        q1z, q2z = _prep(p, o_dq, N_HEADS, diff_qnorm[l], DIFF_QK, rope_d, split=True,
                         scale=DIFF_QK ** -0.5 * math.log2(math.e))
        kd = _prep(p, o_dk, N_HEADS, diff_knorm[l], DIFF_QK, rope_d)
        qs = _prep(p, o_sq, N_HEADS, swa_qnorm[l], HEAD_DIM, rope_s)
        ks = _prep(p, o_sk, SWA_KV_HEADS, swa_knorm[l], HEAD_DIM, rope_s)
        qn = _prep(p, o_nq, N_HEADS, na_qnorm[l], HEAD_DIM, None)
        kn = _prep(p, o_nk, N_HEADS, na_knorm[l], HEAD_DIM, None)

        v_d = p[:, o_dv:o_dv + bw].reshape(t_all, N_HEADS, HEAD_DIM)
        vx = jnp.concatenate([v_d, jnp.ones_like(v_d)], axis=-1).reshape(t_all, 2 * bw)
        yb = _diff_attention(q1z, q2z, kd, vx, diff_lambda[l], diff_subln[l], lam_init, n, True)
        if update_ctx:
            yb_c = _diff_attention(q1z, q2z, kd, vx, diff_lambda[l], diff_subln[l], lam_init, n, False)
            yb = jnp.concatenate([yb, yb_c], axis=0)
        ys = _swa_attention(qs, ks, p, o_sv, swa_sink[l], n, nq)
        yn = _na_attention(qn, kn, p, o_nv, _na_bias_table(na_rpb[l]), n, nq)

        acc = _merge(h, (a, yb, ys, yn), w_gate[l].astype(cdt), b_gate[l], w_branch[l].astype(cdt), nq)
        x1 = _matmul_residual(acc, w_out[l].astype(cdt), xa, mod, 2, n, nq)

        h2, h2p = _rmsmod(x1, g_norm2[l], mod, 3, 4, n, nq, packed=True)
        sel, idx, wk = _router(h2, w_router[l].T.astype(cdt), b_router[l], nq)
        n_used, blk_e, slot_tok, slot_w, slots, n_blk = _dispatch_tables(sel, idx, wk, nq)
        y_slots = _experts(h2p, w_exp_gate[l], w_exp_up[l], w_exp_down[l], n_used, blk_e, slot_tok, slot_w, n_blk)
        wsh = jnp.concatenate([w_sh_gate[l], w_sh_up[l]], axis=-1).astype(cdt)
        sh = _shared_expert(h2, wsh, w_sh_down[l].astype(cdt), nq)
        xa = _combine(y_slots, slots, x1, sh, mod, 5, n, nq)

    return xa[:n].reshape(bsz, n, d)
```

```python
import functools
import math

import jax
import jax.numpy as jnp
from jax import lax
from jax.experimental import pallas as pl
from jax.experimental.pallas import tpu as pltpu

GRID_W = 64
HEAD_DIM = 128
DIFF_QK = 64
SWA_WINDOW = 128
SWA_BLOCK = 128
SWA_KV_HEADS = 2
N_HEADS = 8
NA_ROWS = 8
NA_COLS = 16
ROPE_BASE = 10000.0
N_EXPERTS = 64
N_GROUPS = 8
TOPK_GROUPS = 4
TOP_K = 8
EXPERT_DIM = 256
ROUTED_SCALE = 2.5
EPS = 1e-6
NEG = -1e30

LANES = 128
SUBLANES = 8
V7X_VMEM_BYTES = 64 * 1024 * 1024

MXU_DTYPE = jnp.bfloat16
MOE_TILE = 256
COMBINE_TILE = 64

F32 = jnp.float32
I32 = jnp.int32
U32 = jnp.uint32


def _params(sem, vmem_mb=48):
    return pltpu.CompilerParams(dimension_semantics=sem, vmem_limit_bytes=vmem_mb * 1024 * 1024)


def _pick(m, cands):
    for c in cands:
        if m % c == 0:
            return c
    raise ValueError(f"no tile for {m} in {cands}")


def _nt_dot(a, b):
    return lax.dot_general(a, b, (((1,), (1,)), ((), ())), preferred_element_type=F32)


def _dot(a, b):
    return jnp.dot(a, b, preferred_element_type=F32)


def _sigmoid(x):
    return 1.0 / (1.0 + jnp.exp(-x))


def _modvec_kernel(s_ref, w_ref, b_ref, o_ref):
    _, d, tn = w_ref.shape
    rows = []
    for r in range(2):
        c = s_ref[r]
        s = c * _sigmoid(c)
        parts = []
        for j in range(tn // LANES):
            prod = w_ref[0, :, j * LANES:(j + 1) * LANES] * s
            part = jnp.sum(prod.reshape(d // SUBLANES, SUBLANES, LANES), axis=0)
            parts.append(jnp.sum(part, axis=0, keepdims=True))
        rows.append(jnp.concatenate(parts, axis=1))
    is_first = lax.broadcasted_iota(I32, (2, tn), 0) == 0
    o_ref[...] = jnp.where(is_first, rows[0], rows[1]) + b_ref[0]


def _modvec(cc, w, b, layer):
    depth, d, n = w.shape
    tn = 512
    s_rep = jnp.broadcast_to(cc[:, :, None], (2, d, LANES))
    return pl.pallas_call(
        _modvec_kernel,
        grid=(n // tn,),
        in_specs=[pl.BlockSpec((2, d, LANES), lambda j: (0, 0, 0)),
                  pl.BlockSpec((1, d, tn), lambda j: (layer, 0, j)),
                  pl.BlockSpec((1, 1, tn), lambda j: (layer, 0, j))],
        out_specs=pl.BlockSpec((2, tn), lambda j: (0, j)),
        out_shape=jax.ShapeDtypeStruct((2, n), F32),
        compiler_params=_params(("arbitrary",)),
        name="modvec",
    )(s_rep, w, b.reshape(depth, 1, n))


def _pack_halves(v):
    c = v.shape[1] // 2
    bits = lax.bitcast_convert_type(v.astype(jnp.bfloat16).astype(F32), U32)
    return (bits[:, c:] & jnp.uint32(0xFFFF0000)) | (bits[:, :c] >> 16)


def _unpack_halves(w):
    lo = lax.bitcast_convert_type(w << 16, F32)
    hi = lax.bitcast_convert_type(w & jnp.uint32(0xFFFF0000), F32)
    return lo, hi


def _rmsmod_kernel(n_lat, packed, x_ref, g_ref, sh_ref, sc_ref, o_ref, *p_ref):
    tr = x_ref.shape[0]
    x = x_ref[...]
    ms = jnp.mean(x * x, axis=-1, keepdims=True)
    y = x * lax.rsqrt(ms + EPS) * g_ref[...]
    rows = pl.program_id(0) * tr + lax.broadcasted_iota(I32, (tr, 1), 0)
    is_ctx = rows >= n_lat
    sh = jnp.where(is_ctx, sh_ref[1:2, :], sh_ref[0:1, :])
    sc = jnp.where(is_ctx, sc_ref[1:2, :], sc_ref[0:1, :])
    h = y * (1.0 + sc) + sh
    o_ref[...] = h.astype(o_ref.dtype)
    if packed:
        p_ref[0][...] = _pack_halves(h)


def _rmsmod(x, g, mod, shift_blk, scale_blk, n_lat, m_rows, packed=False):
    d = x.shape[1]
    tr = 256
    out_shape = [jax.ShapeDtypeStruct((m_rows, d), MXU_DTYPE)]
    out_specs = [pl.BlockSpec((tr, d), lambda i: (i, 0))]
    if packed:
        out_shape.append(jax.ShapeDtypeStruct((m_rows, d // 2), U32))
        out_specs.append(pl.BlockSpec((tr, d // 2), lambda i: (i, 0)))
    res = pl.pallas_call(
        functools.partial(_rmsmod_kernel, n_lat, packed),
        grid=(m_rows // tr,),
        in_specs=[pl.BlockSpec((tr, d), lambda i: (i, 0)),
                  pl.BlockSpec((1, d), lambda i: (0, 0)),
                  pl.BlockSpec((2, d), lambda i: (0, shift_blk)),
                  pl.BlockSpec((2, d), lambda i: (0, scale_blk))],
        out_specs=out_specs,
        out_shape=out_shape,
        compiler_params=_params(("arbitrary",)),
        name="rmsmod",
    )(x, g.reshape(1, d), mod, mod)
    return res if packed else res[0]


def _mm_kernel(a_ref, w_ref, o_ref):
    o_ref[...] = _dot(a_ref[...], w_ref[...]).astype(o_ref.dtype)


def _mm_res_kernel(n_lat, a_ref, w_ref, r_ref, g_ref, o_ref):
    tm = a_ref.shape[0]
    acc = _dot(a_ref[...], w_ref[...])
    rows = pl.program_id(0) * tm + lax.broadcasted_iota(I32, (tm, 1), 0)
    g = jnp.where(rows >= n_lat, g_ref[1:2, :], g_ref[0:1, :])
    o_ref[...] = r_ref[...] + g * acc


def _matmul(a, w, out_dtype, m_rows):
    k = a.shape[1]
    n = w.shape[1]
    tm = _pick(m_rows, (1280, 1024, 640, 512, 256))
    tn = 512
    return pl.pallas_call(
        _mm_kernel,
        grid=(m_rows // tm, n // tn),
        in_specs=[pl.BlockSpec((tm, k), lambda i, j: (i, 0)),
                  pl.BlockSpec((k, tn), lambda i, j: (0, j))],
        out_specs=pl.BlockSpec((tm, tn), lambda i, j: (i, j)),
        out_shape=jax.ShapeDtypeStruct((m_rows, n), out_dtype),
        compiler_params=_params(("arbitrary", "arbitrary")),
        name="matmul",
    )(a, w)


def _matmul_residual(a, w, res, mod, gate_blk, n_lat, m_rows):
    k = a.shape[1]
    n = w.shape[1]
    tm = _pick(m_rows, (1280, 1024, 640, 512, 256))
    tn = 512
    nb = n // tn
    return pl.pallas_call(
        functools.partial(_mm_res_kernel, n_lat),
        grid=(m_rows // tm, nb),
        in_specs=[pl.BlockSpec((tm, k), lambda i, j: (i, 0)),
                  pl.BlockSpec((k, tn), lambda i, j: (0, j)),
                  pl.BlockSpec((tm, tn), lambda i, j: (i, j)),
                  pl.BlockSpec((2, tn), lambda i, j: (0, gate_blk * nb + j))],
        out_specs=pl.BlockSpec((tm, tn), lambda i, j: (i, j)),
        out_shape=jax.ShapeDtypeStruct((m_rows, n), F32),
        compiler_params=_params(("arbitrary", "arbitrary")),
        name="matmul_residual",
    )(a, w, res, mod)


def _conv_kernel(n_lat, t_all, b_ref, c_ref, x_ref, cp_ref, xp_ref, cn_ref, xn_ref, w_ref, o_ref):
    tr = b_ref.shape[0]
    u = c_ref[...].astype(F32) * x_ref[...].astype(F32)
    u_prev_row = cp_ref[SUBLANES - 1:SUBLANES, :].astype(F32) * xp_ref[SUBLANES - 1:SUBLANES, :].astype(F32)
    u_next_row = cn_ref[0:1, :].astype(F32) * xn_ref[0:1, :].astype(F32)
    r = lax.broadcasted_iota(I32, (tr, 1), 0)
    tok = pl.program_id(0) * tr + r
    up = jnp.where(r == 0, u_prev_row, pltpu.roll(u, 1, 0))
    un = jnp.where(r == tr - 1, u_next_row, pltpu.roll(u, tr - 1, 0))
    up = jnp.where((tok == 0) | (tok == n_lat), 0.0, up)
    un = jnp.where((tok == n_lat - 1) | (tok == t_all - 1), 0.0, un)
    y = w_ref[0:1, :] * up + w_ref[1:2, :] * u + w_ref[2:3, :] * un
    o_ref[...] = (b_ref[...].astype(F32) * y).astype(o_ref.dtype)


def _short_conv(p, conv_w, n_lat, m_rows):
    t_all = p.shape[0]
    cw = conv_w.shape[1]
    tr = 256
    tc = 512
    nc = cw // tc
    rb = tr // SUBLANES
    last = t_all // SUBLANES - 1
    return pl.pallas_call(
        functools.partial(_conv_kernel, n_lat, t_all),
        grid=(m_rows // tr, nc),
        in_specs=[pl.BlockSpec((tr, tc), lambda i, j: (i, j)),
                  pl.BlockSpec((tr, tc), lambda i, j: (i, nc + j)),
                  pl.BlockSpec((tr, tc), lambda i, j: (i, 2 * nc + j)),
                  pl.BlockSpec((SUBLANES, tc), lambda i, j: (jnp.maximum(i * rb - 1, 0), nc + j)),
                  pl.BlockSpec((SUBLANES, tc), lambda i, j: (jnp.maximum(i * rb - 1, 0), 2 * nc + j)),
                  pl.BlockSpec((SUBLANES, tc), lambda i, j: (jnp.minimum((i + 1) * rb, last), nc + j)),
                  pl.BlockSpec((SUBLANES, tc), lambda i, j: (jnp.minimum((i + 1) * rb, last), 2 * nc + j)),
                  pl.BlockSpec((3, tc), lambda i, j: (0, j))],
        out_specs=pl.BlockSpec((tr, tc), lambda i, j: (i, j)),
        out_shape=jax.ShapeDtypeStruct((m_rows, cw), MXU_DTYPE),
        compiler_params=_params(("arbitrary", "arbitrary")),
        name="short_conv",
    )(p, p, p, p, p, p, p, conv_w)


def _prep_kernel(n_heads, group, half, split, scale, x_ref, g_ref, *rest):
    if half:
        cos_ref, sa_ref, sb_ref = rest[:3]
        outs = rest[3:]
    else:
        outs = rest
    lane = lax.broadcasted_iota(I32, (1, LANES), 1)
    lo = lane < DIFF_QK
    g = g_ref[...]
    for h in range(n_heads):
        x = x_ref[:, h * LANES:(h + 1) * LANES].astype(F32)
        sq = x * x
        if group == LANES:
            ms = jnp.mean(sq, axis=-1, keepdims=True)
        else:
            s_lo = jnp.sum(jnp.where(lo, sq, 0.0), axis=-1, keepdims=True)
            s_hi = jnp.sum(jnp.where(lo, 0.0, sq), axis=-1, keepdims=True)
            ms = jnp.where(lo, s_lo, s_hi) * (1.0 / group)
        y = x * lax.rsqrt(ms + EPS) * g
        if half:
            y = (y * cos_ref[...] + pltpu.roll(y, LANES - half, 1) * sa_ref[...]
                 + pltpu.roll(y, half, 1) * sb_ref[...])
        if scale != 1.0:
            y = y * scale
        if split:
            outs[0][:, h * LANES:(h + 1) * LANES] = jnp.where(lo, y, 0.0).astype(outs[0].dtype)
            outs[1][:, h * LANES:(h + 1) * LANES] = jnp.where(lo, 0.0, y).astype(outs[1].dtype)
        else:
            outs[0][:, h * LANES:(h + 1) * LANES] = y.astype(outs[0].dtype)


def _prep(p, col0, n_heads, gain, group, rope, split=False, scale=1.0):
    t_all = p.shape[0]
    tr = _pick(t_all, (640, 256))
    hpb = _pick(math.gcd(col0 // LANES, n_heads), (8, 4, 2, 1))
    w = hpb * LANES
    cb = col0 // w
    g = jnp.tile(gain.astype(F32), LANES // gain.shape[0]).reshape(1, LANES)
    in_specs = [pl.BlockSpec((tr, w), lambda i, j: (i, cb + j)),
                pl.BlockSpec((1, LANES), lambda i, j: (0, 0))]
    args = [p, g]
    half = 0
    if rope is not None:
        half = rope[3]
        in_specs += [pl.BlockSpec((tr, LANES), lambda i, j: (i, 0))] * 3
        args += list(rope[:3])
    n_out = 2 if split else 1
    res = pl.pallas_call(
        functools.partial(_prep_kernel, hpb, group, half, split, scale),
        grid=(t_all // tr, n_heads // hpb),
        in_specs=in_specs,
        out_specs=[pl.BlockSpec((tr, w), lambda i, j: (i, j))] * n_out,
        out_shape=[jax.ShapeDtypeStruct((t_all, n_heads * LANES), MXU_DTYPE)] * n_out,
        compiler_params=_params(("arbitrary", "arbitrary")),
        name="qk_prep",
    )(*args)
    return res if split else res[0]


def _rope_tables(n_lat, t_all, seg):
    half = seg // 2
    n_rows = n_lat // GRID_W
    lane = jnp.arange(LANES)
    freqs = ROPE_BASE ** (-jnp.arange(0, seg, 2, dtype=F32) / seg)
    f_lane = freqs[lane % half]
    use_col = ((lane // seg) % 2 == 1)[None, None, :]
    ang_r = jnp.arange(n_rows, dtype=F32)[:, None] * f_lane[None, :]
    ang_c = jnp.arange(GRID_W, dtype=F32)[:, None] * f_lane[None, :]

    def expand(fn):
        full = jnp.where(use_col, fn(ang_c)[None, :, :], fn(ang_r)[:, None, :])
        return full.reshape(n_lat, LANES)

    cos = expand(jnp.cos)
    sin = expand(jnp.sin)
    first = (lane % seg) < half
    sa = jnp.where(first[None, :], -sin, 0.0)
    sb = jnp.where(first[None, :], 0.0, sin)
    pad = t_all - n_lat
    cos = jnp.concatenate([cos, jnp.ones((pad, LANES), F32)], axis=0)
    sa = jnp.concatenate([sa, jnp.zeros((pad, LANES), F32)], axis=0)
    sb = jnp.concatenate([sb, jnp.zeros((pad, LANES), F32)], axis=0)
    return cos, sa, sb, half


DIFF_TQ = 512
DIFF_TK = 512
DIFF_GROUP = 4


def _diff_kernel(n_pairs, tkc, lam_init, q1_ref, q2_ref, k_ref, vx_ref, dl_ref, gs_ref, o_ref, s_scr):
    tq = q1_ref.shape[0]
    t_all = k_ref.shape[0]
    n_lat = 2 * n_pairs * tkc
    q1 = q1_ref[...]
    q2 = q2_ref[...]

    def scores(kc):
        return _nt_dot(q1, kc), _nt_dot(q2, kc)

    def fold(s, vc, m, a):
        mn = jnp.maximum(m, jnp.max(s, axis=-1, keepdims=True))
        alpha = jnp.exp2(m - mn)
        p = jnp.exp2(s - mn).astype(vc.dtype)
        return mn, alpha * a + _dot(p, vc)

    def lat_scores_to(buf, c):
        off = pl.multiple_of(c * tkc, tkc)
        s1, s2 = scores(k_ref[pl.ds(off, tkc), :])
        s_scr[buf, 0] = s1
        s_scr[buf, 1] = s2

    def lat_fold_from(buf, c, carry):
        m1, a1, m2, a2 = carry
        off = pl.multiple_of(c * tkc, tkc)
        vc = vx_ref[pl.ds(off, tkc), :]
        m1, a1 = fold(s_scr[buf, 0], vc, m1, a1)
        m2, a2 = fold(s_scr[buf, 1], vc, m2, a2)
        return m1, a1, m2, a2

    minit = jnp.full((tq, 1), -jnp.inf, F32)
    ainit = jnp.zeros((tq, 2 * HEAD_DIM), F32)
    carry = (minit, ainit, minit, ainit)
    n_chunks = 2 * n_pairs
    group = _pick(n_chunks, (DIFF_GROUP, 2)) if n_pairs > 0 else 0
    if n_pairs > 0:
        lat_scores_to(0, 0)

        def run_group(c0, carry, last):
            for u in range(group):
                if not (last and u == group - 1):
                    lat_scores_to((u + 1) % 2, c0 + u + 1)
                    carry = lat_fold_from(u % 2, c0 + u, carry)
            return carry

        carry = lax.fori_loop(0, n_chunks // group - 1,
                              lambda j, cr: run_group(j * group, cr, False), carry)
        carry = run_group(n_chunks - group, carry, True)
    s1c, s2c = scores(k_ref[n_lat:t_all, :])
    if n_pairs > 0:
        carry = lat_fold_from(1, n_chunks - 1, carry)
    m1, a1, m2, a2 = carry
    vcx = vx_ref[n_lat:t_all, :]
    m1, a1 = fold(s1c, vcx, m1, a1)
    m2, a2 = fold(s2c, vcx, m2, a2)

    dl = dl_ref[...]
    lam = (jnp.exp(jnp.sum(dl[0:1, :] * dl[1:2, :], axis=-1, keepdims=True))
           - jnp.exp(jnp.sum(dl[2:3, :] * dl[3:4, :], axis=-1, keepdims=True)) + lam_init)
    o = (a1[:, :HEAD_DIM] / a1[:, HEAD_DIM:HEAD_DIM + 1]
         - lam * (a2[:, :HEAD_DIM] / a2[:, HEAD_DIM:HEAD_DIM + 1]))
    ms = jnp.mean(o * o, axis=-1, keepdims=True)
    o = o * lax.rsqrt(ms + EPS) * gs_ref[...] * (1.0 - lam_init)
    o_ref[...] = o.astype(o_ref.dtype)


def _diff_attention(q1z, q2z, kd, vx, diff_lambda, subln, lam_init, n_lat, latent):
    t_all = kd.shape[0]
    n_ctx = t_all - n_lat
    if latent:
        tq, tkc, rows, blk0 = DIFF_TQ, DIFF_TK, n_lat, 0
        assert n_lat % (2 * tkc) == 0 and n_lat % tq == 0
        n_pairs = n_lat // (2 * tkc)
    else:
        tq, tkc, rows, blk0, n_pairs = n_ctx, DIFF_TK, n_ctx, n_lat // n_ctx, 0
        assert n_lat % n_ctx == 0
    if n_pairs == 0:
        k_spec = pl.BlockSpec((n_ctx, LANES), lambda h, i: (n_lat // n_ctx, h))
        v_spec = pl.BlockSpec((n_ctx, 2 * LANES), lambda h, i: (n_lat // n_ctx, h))
    else:
        k_spec = pl.BlockSpec((t_all, LANES), lambda h, i: (0, h))
        v_spec = pl.BlockSpec((t_all, 2 * LANES), lambda h, i: (0, h))
    return pl.pallas_call(
        functools.partial(_diff_kernel, n_pairs, tkc, lam_init),
        grid=(N_HEADS, rows // tq),
        in_specs=[pl.BlockSpec((tq, LANES), lambda h, i: (blk0 + i, h)),
                  pl.BlockSpec((tq, LANES), lambda h, i: (blk0 + i, h)),
                  k_spec, v_spec,
                  pl.BlockSpec((4, DIFF_QK), lambda h, i: (0, 0)),
                  pl.BlockSpec((1, HEAD_DIM), lambda h, i: (0, 0))],
        out_specs=pl.BlockSpec((tq, LANES), lambda h, i: (i, h)),
        out_shape=jax.ShapeDtypeStruct((rows, N_HEADS * HEAD_DIM), MXU_DTYPE),
        scratch_shapes=[pltpu.VMEM((2, 2, tq, tkc), F32)],
        compiler_params=_params(("arbitrary", "arbitrary")),
        name="diff_attention" if latent else "diff_attention_ctx",
    )(q1z, q2z, kd, vx, diff_lambda.astype(F32), subln.astype(F32).reshape(1, HEAD_DIM))


def _swa_kernel(n_lat, t_all, q_ref, k_ref, v_ref, sink_ref, o_ref):
    w = SWA_BLOCK
    g = q_ref.shape[1] // HEAD_DIM
    b = pl.program_id(1)
    reach = jnp.where(b * w < n_lat, SWA_WINDOW, -1)
    scale = HEAD_DIM ** -0.5
    start = jnp.clip((b - 1) * w, 0, n_lat - 3 * w)
    start = pl.multiple_of(start, w)
    kw = k_ref[pl.ds(start, 3 * w), :]
    vw = v_ref[pl.ds(start, 3 * w), :]
    kc = k_ref[n_lat:t_all, :]
    vc = v_ref[n_lat:t_all, :]
    q = jnp.concatenate([q_ref[:, j * HEAD_DIM:(j + 1) * HEAD_DIM] for j in range(g)], axis=0)
    s_w = _nt_dot(q, kw) * scale
    s_c = _nt_dot(q, kc) * scale
    qpos = b * w + lax.broadcasted_iota(I32, (w, 3 * w), 0)
    kpos = start + lax.broadcasted_iota(I32, (w, 3 * w), 1)
    mask = jnp.abs(qpos - kpos) <= reach
    mask = jnp.concatenate([mask] * g, axis=0)
    s_w = jnp.where(mask, s_w, NEG)
    sink = jnp.concatenate([jnp.broadcast_to(sink_ref[0, :, j:j + 1], (w, 1)) for j in range(g)], axis=0)
    m = jnp.maximum(jnp.maximum(jnp.max(s_w, axis=-1, keepdims=True), jnp.max(s_c, axis=-1, keepdims=True)), sink)
    p_w = jnp.exp(s_w - m)
    p_c = jnp.exp(s_c - m)
    l = jnp.sum(p_w, axis=-1, keepdims=True) + jnp.sum(p_c, axis=-1, keepdims=True) + jnp.exp(sink - m)
    inv = 1.0 / l
    o = _dot((p_w * inv).astype(vw.dtype), vw) + _dot((p_c * inv).astype(vc.dtype), vc)
    for j in range(g):
        o_ref[:, j * HEAD_DIM:(j + 1) * HEAD_DIM] = o[j * w:(j + 1) * w, :].astype(o_ref.dtype)


def _swa_attention(qs, ks, p, v_col0, sink, n_lat, nq):
    t_all = ks.shape[0]
    g = N_HEADS // SWA_KV_HEADS
    vb = v_col0 // LANES
    sink3 = sink.astype(F32).reshape(SWA_KV_HEADS, 1, g)
    return pl.pallas_call(
        functools.partial(_swa_kernel, n_lat, t_all),
        grid=(SWA_KV_HEADS, nq // SWA_BLOCK),
        in_specs=[pl.BlockSpec((SWA_BLOCK, g * HEAD_DIM), lambda kv, b: (b, kv)),
                  pl.BlockSpec((t_all, LANES), lambda kv, b: (0, kv)),
                  pl.BlockSpec((t_all, LANES), lambda kv, b: (0, vb + kv)),
                  pl.BlockSpec((1, 1, g), lambda kv, b: (kv, 0, 0))],
        out_specs=pl.BlockSpec((SWA_BLOCK, g * HEAD_DIM), lambda kv, b: (b, kv)),
        out_shape=jax.ShapeDtypeStruct((nq, N_HEADS * HEAD_DIM), MXU_DTYPE),
        compiler_params=_params(("arbitrary", "arbitrary")),
        name="swa_attention",
    )(qs, ks, p, sink3)


NA_ROWS_PER_STEP = 4


def _na_kernel(n_lat, t_all, q_ref, k_ref, v_ref, bias_ref, o_ref):
    n_rows = n_lat // GRID_W
    span = NA_ROWS * GRID_W
    scale = HEAD_DIM ** -0.5
    kc = k_ref[n_lat:t_all, :]
    vc = v_ref[n_lat:t_all, :]
    for j in range(NA_ROWS_PER_STEP):
        r = pl.program_id(1) * NA_ROWS_PER_STEP + j
        is_lat = r < n_rows
        r0 = jnp.clip(r - NA_ROWS // 2, 0, n_rows - NA_ROWS)
        delta = jnp.where(is_lat, r0 - r + (NA_ROWS - 1), NA_ROWS)
        off = pl.multiple_of(r0 * GRID_W, GRID_W)
        kn = k_ref[pl.ds(off, span), :]
        vn = v_ref[pl.ds(off, span), :]
        q = q_ref[j * GRID_W:(j + 1) * GRID_W, :]
        s_n = _nt_dot(q, kn) * scale + bias_ref[0, delta]
        s_c = _nt_dot(q, kc) * scale
        m = jnp.maximum(jnp.max(s_n, axis=-1, keepdims=True), jnp.max(s_c, axis=-1, keepdims=True))
        p_n = jnp.exp(s_n - m)
        p_c = jnp.exp(s_c - m)
        inv = 1.0 / (jnp.sum(p_n, axis=-1, keepdims=True) + jnp.sum(p_c, axis=-1, keepdims=True))
        o = _dot((p_n * inv).astype(vn.dtype), vn) + _dot((p_c * inv).astype(vc.dtype), vc)
        o_ref[j * GRID_W:(j + 1) * GRID_W, :] = o.astype(o_ref.dtype)


def _na_bias_table(rpb):
    h = rpb.shape[0]
    c = jnp.arange(GRID_W)
    c0 = jnp.clip(c - NA_COLS // 2, 0, GRID_W - NA_COLS)
    kc = jnp.arange(GRID_W)
    inwin = (kc[None, :] >= c0[:, None]) & (kc[None, :] < c0[:, None] + NA_COLS)
    col_off = jnp.clip(kc[None, :] - c[:, None] + NA_COLS - 1, 0, 2 * NA_COLS - 2)
    toe = jnp.where(inwin[None, None], rpb.astype(F32)[:, :, col_off], NEG)
    tab = jnp.stack([toe[:, dl:dl + NA_ROWS] for dl in range(NA_ROWS)], axis=1)
    tab = tab.transpose(0, 1, 3, 2, 4).reshape(h, NA_ROWS, GRID_W, NA_ROWS * GRID_W)
    masked = jnp.full((h, 1, GRID_W, NA_ROWS * GRID_W), NEG, F32)
    return jnp.concatenate([tab, masked], axis=1)


def _na_attention(qn, kn, p, v_col0, bias, n_lat, nq):
    t_all = kn.shape[0]
    vb = v_col0 // LANES
    tq = NA_ROWS_PER_STEP * GRID_W
    return pl.pallas_call(
        functools.partial(_na_kernel, n_lat, t_all),
        grid=(N_HEADS, nq // tq),
        in_specs=[pl.BlockSpec((tq, LANES), lambda h, i: (i, h)),
                  pl.BlockSpec((t_all, LANES), lambda h, i: (0, h)),
                  pl.BlockSpec((t_all, LANES), lambda h, i: (0, vb + h)),
                  pl.BlockSpec((1, NA_ROWS + 1, GRID_W, NA_ROWS * GRID_W), lambda h, i: (h, 0, 0, 0))],
        out_specs=pl.BlockSpec((tq, LANES), lambda h, i: (i, h)),
        out_shape=jax.ShapeDtypeStruct((nq, N_HEADS * HEAD_DIM), MXU_DTYPE),
        compiler_params=_params(("arbitrary", "arbitrary")),
        name="na_attention",
    )(qn, kn, p, bias)


def _merge_kernel(h_ref, y0_ref, y1_ref, y2_ref, y3_ref, wg_ref, bg_ref, wb_ref, o_ref):
    h = h_ref[...]
    acc = None
    for i, y_ref in enumerate((y0_ref, y1_ref, y2_ref, y3_ref)):
        gate = _sigmoid(_dot(h, wg_ref[i]) + bg_ref[i])
        term = gate * _dot(y_ref[...], wb_ref[i])
        acc = term if acc is None else acc + term
    o_ref[...] = acc.astype(o_ref.dtype)


def _merge(h, ys, wg, bg, wb, m_rows):
    d = h.shape[1]
    bw = ys[0].shape[1]
    tm = _pick(m_rows, (1280, 1024, 640, 512, 256))
    tn = 256
    once = pl.Buffered(1)
    return pl.pallas_call(
        _merge_kernel,
        grid=(m_rows // tm, d // tn),
        in_specs=[pl.BlockSpec((tm, d), lambda i, j: (i, 0), pipeline_mode=once)]
        + [pl.BlockSpec((tm, bw), lambda i, j: (i, 0), pipeline_mode=once)] * 4
        + [pl.BlockSpec((4, d, tn), lambda i, j: (0, 0, j)),
           pl.BlockSpec((4, 1, tn), lambda i, j: (0, 0, j)),
           pl.BlockSpec((4, bw, tn), lambda i, j: (0, 0, j))],
        out_specs=pl.BlockSpec((tm, tn), lambda i, j: (i, j)),
        out_shape=jax.ShapeDtypeStruct((m_rows, d), MXU_DTYPE),
        compiler_params=_params(("arbitrary", "arbitrary"), vmem_mb=56),
        name="merge",
    )(h, *ys, wg, bg.astype(F32).reshape(4, 1, d), wb)


def _router_kernel(h_ref, w_ref, b_ref, sel_ref, idx_ref, wk_ref):
    tr = h_ref.shape[0]
    per = N_EXPERTS // N_GROUPS
    scores = _sigmoid(_nt_dot(w_ref[...], h_ref[...]))
    choice = scores + b_ref[...]
    sub = lax.broadcasted_iota(I32, (per, tr), 0)
    neg_inf = -jnp.inf
    slabs = [choice[g * per:(g + 1) * per, :] for g in range(N_GROUPS)]
    grp = []
    for c in slabs:
        m1 = jnp.max(c, axis=0, keepdims=True)
        i1 = jnp.min(jnp.where(c == m1, sub, per), axis=0, keepdims=True)
        m2 = jnp.max(jnp.where(sub == i1, neg_inf, c), axis=0, keepdims=True)
        grp.append(m1 + m2)
    gsel = [jnp.zeros((1, tr), jnp.bool_) for _ in range(N_GROUPS)]
    for _ in range(TOPK_GROUPS):
        gm = functools.reduce(jnp.maximum, grp)
        found = jnp.zeros((1, tr), jnp.bool_)
        for g in range(N_GROUPS):
            hit = (grp[g] == gm) & jnp.logical_not(found)
            found = found | hit
            gsel[g] = gsel[g] | hit
            grp[g] = jnp.where(hit, neg_inf, grp[g])
    vals = [jnp.where(gsel[g], slabs[g], neg_inf) for g in range(N_GROUPS)]
    eidx = [sub + g * per for g in range(N_GROUPS)]
    sel = [jnp.zeros((per, tr), jnp.bool_) for _ in range(N_GROUPS)]
    row8 = lax.broadcasted_iota(I32, (TOP_K, tr), 0)
    idx_out = jnp.zeros((TOP_K, tr), I32)
    w_out = jnp.zeros((TOP_K, tr), F32)
    wsum = jnp.zeros((1, tr), F32)
    for k in range(TOP_K):
        vm = jnp.max(functools.reduce(jnp.maximum, vals), axis=0, keepdims=True)
        cand = [jnp.where(vals[g] == vm, eidx[g], N_EXPERTS) for g in range(N_GROUPS)]
        ik = jnp.min(functools.reduce(jnp.minimum, cand), axis=0, keepdims=True)
        wk = jnp.zeros((1, tr), F32)
        for g in range(N_GROUPS):
            hit = eidx[g] == ik
            sel[g] = sel[g] | hit
            vals[g] = jnp.where(hit, neg_inf, vals[g])
            sc = scores[g * per:(g + 1) * per, :]
            wk = wk + jnp.sum(jnp.where(hit, sc, 0.0), axis=0, keepdims=True)
        idx_out = jnp.where(row8 == k, ik, idx_out)
        w_out = jnp.where(row8 == k, wk, w_out)
        wsum = wsum + wk
    for g in range(N_GROUPS):
        sel_ref[g * per:(g + 1) * per, :] = sel[g].astype(I32)
    idx_ref[...] = idx_out
    wk_ref[...] = w_out / wsum * ROUTED_SCALE


def _router(h2, w_rt, b_r, m_rows):
    d = h2.shape[1]
    tr = 256
    return pl.pallas_call(
        _router_kernel,
        grid=(m_rows // tr,),
        in_specs=[pl.BlockSpec((tr, d), lambda i: (i, 0)),
                  pl.BlockSpec((N_EXPERTS, d), lambda i: (0, 0)),
                  pl.BlockSpec((N_EXPERTS, 1), lambda i: (0, 0))],
        out_specs=[pl.BlockSpec((N_EXPERTS, tr), lambda i: (0, i)),
                   pl.BlockSpec((TOP_K, tr), lambda i: (0, i)),
                   pl.BlockSpec((TOP_K, tr), lambda i: (0, i))],
        out_shape=[jax.ShapeDtypeStruct((N_EXPERTS, m_rows), I32),
                   jax.ShapeDtypeStruct((TOP_K, m_rows), I32),
                   jax.ShapeDtypeStruct((TOP_K, m_rows), F32)],
        compiler_params=_params(("arbitrary",)),
        name="router",
    )(h2, w_rt, b_r.astype(F32).reshape(N_EXPERTS, 1))


def _row_gather(tok_ref, n_rows, src_hbm, dst_buf, sem, start):
    def body(j, carry):
        for u in range(SUBLANES):
            r = j * SUBLANES + u
            t = tok_ref[0, 0, r] if start else 0
            cp = pltpu.make_async_copy(src_hbm.at[pl.ds(t, 1), :], dst_buf.at[pl.ds(r, 1), :], sem)
            if start:
                cp.start()
            else:
                cp.wait()
        return carry
    lax.fori_loop(0, n_rows // SUBLANES, body, 0)


def _expert_kernel(nused_ref, blke_ref, tok_ref, tokn_ref, hp_ref, wg_ref, wu_ref, wd_ref, sw_ref, y_ref,
                   xbuf, sem, wgu_c, wd_c):
    i = pl.program_id(0)
    nu = nused_ref[0]
    tm = xbuf.shape[1]
    half = xbuf.shape[2]
    slot = i % 2

    @pl.when(i == 0)
    def _():
        _row_gather(tok_ref, tm, hp_ref, xbuf.at[0], sem.at[0], True)

    @pl.when(i + 1 < nu)
    def _():
        _row_gather(tokn_ref, tm, hp_ref, xbuf.at[1 - slot], sem.at[1 - slot], True)

    @pl.when((i < nu) & ((i == 0) | (blke_ref[i] != blke_ref[jnp.maximum(i - 1, 0)])))
    def _():
        wgu_c[:, :EXPERT_DIM] = wg_ref[0, 0].astype(wgu_c.dtype)
        wgu_c[:, EXPERT_DIM:] = wu_ref[0, 0].astype(wgu_c.dtype)
        wd_c[...] = wd_ref[0, 0].astype(wd_c.dtype)

    @pl.when(i < nu)
    def _():
        _row_gather(tok_ref, tm, hp_ref, xbuf.at[slot], sem.at[slot], False)
        lo, hi = _unpack_halves(xbuf[slot])
        gu = (_dot(lo.astype(MXU_DTYPE), wgu_c[:half, :])
              + _dot(hi.astype(MXU_DTYPE), wgu_c[half:, :]))
        g = gu[:, :EXPERT_DIM]
        u = gu[:, EXPERT_DIM:]
        a = (g * _sigmoid(g) * u).astype(MXU_DTYPE)
        y = _dot(a, wd_c[...]) * sw_ref[...]
        y_ref[...] = _pack_halves(y)

    @pl.when(i >= nu)
    def _():
        y_ref[...] = jnp.zeros_like(y_ref)


def _experts(hp, wg, wu, wd, layer, n_used, blk_e, slot_tok, slot_w, n_blk):
    d2 = hp.shape[1]
    d = 2 * d2
    tm = MOE_TILE
    tok3 = slot_tok.reshape(n_blk, 1, tm)
    last = n_blk - 1
    grid_spec = pltpu.PrefetchScalarGridSpec(
        num_scalar_prefetch=2,
        grid=(n_blk,),
        in_specs=[pl.BlockSpec((1, 1, tm), lambda i, nu, be: (i, 0, 0), memory_space=pltpu.SMEM),
                  pl.BlockSpec((1, 1, tm), lambda i, nu, be: (jnp.minimum(i + 1, last), 0, 0),
                               memory_space=pltpu.SMEM),
                  pl.BlockSpec(memory_space=pl.ANY),
                  pl.BlockSpec((1, 1, d, EXPERT_DIM), lambda i, nu, be: (layer, be[i], 0, 0)),
                  pl.BlockSpec((1, 1, d, EXPERT_DIM), lambda i, nu, be: (layer, be[i], 0, 0)),
                  pl.BlockSpec((1, 1, EXPERT_DIM, d), lambda i, nu, be: (layer, be[i], 0, 0)),
                  pl.BlockSpec((tm, 1), lambda i, nu, be: (i, 0))],
        out_specs=pl.BlockSpec((tm, d2), lambda i, nu, be: (i, 0)),
        scratch_shapes=[pltpu.VMEM((2, tm, d2), U32), pltpu.SemaphoreType.DMA((2,)),
                        pltpu.VMEM((d, 2 * EXPERT_DIM), MXU_DTYPE), pltpu.VMEM((EXPERT_DIM, d), MXU_DTYPE)],
    )
    return pl.pallas_call(
        _expert_kernel,
        grid_spec=grid_spec,
        out_shape=jax.ShapeDtypeStruct((n_blk * tm, d2), U32),
        compiler_params=_params(("arbitrary",), vmem_mb=56),
        name="routed_experts",
    )(n_used, blk_e, tok3, tok3, hp, wg, wu, wd, slot_w.reshape(n_blk * tm, 1))


def _shared_kernel(h_ref, wgu_ref, wd_ref, o_ref):
    gu = _dot(h_ref[...], wgu_ref[...])
    hd = gu.shape[1] // 2
    g = gu[:, :hd]
    u = gu[:, hd:]
    a = (g * _sigmoid(g) * u).astype(MXU_DTYPE)
    o_ref[...] = _dot(a, wd_ref[...]).astype(o_ref.dtype)


def _shared_expert(h2, wgu, wd, m_rows):
    d = h2.shape[1]
    tm = 256
    return pl.pallas_call(
        _shared_kernel,
        grid=(m_rows // tm,),
        in_specs=[pl.BlockSpec((tm, d), lambda i: (i, 0)),
                  pl.BlockSpec(wgu.shape, lambda i: (0, 0)),
                  pl.BlockSpec(wd.shape, lambda i: (0, 0))],
        out_specs=pl.BlockSpec((tm, d), lambda i: (i, 0)),
        out_shape=jax.ShapeDtypeStruct((m_rows, d), MXU_DTYPE),
        compiler_params=_params(("arbitrary",)),
        name="shared_expert",
    )(h2, wgu, wd)


def _combine_kernel(n_lat, sl_ref, sln_ref, y_hbm, x_ref, sh_ref, g_ref, o_ref, buf, sem):
    i = pl.program_id(0)
    nsteps = pl.num_programs(0)
    tc = x_ref.shape[0]
    nrow = buf.shape[1]
    slot = i % 2

    @pl.when(i == 0)
    def _():
        _row_gather(sl_ref, nrow, y_hbm, buf.at[0], sem.at[0], True)

    @pl.when(i + 1 < nsteps)
    def _():
        _row_gather(sln_ref, nrow, y_hbm, buf.at[1 - slot], sem.at[1 - slot], True)

    _row_gather(sl_ref, nrow, y_hbm, buf.at[slot], sem.at[slot], False)
    lo_acc = None
    hi_acc = None
    for k in range(TOP_K):
        lo, hi = _unpack_halves(buf[slot, k * tc:(k + 1) * tc, :])
        lo_acc = lo if lo_acc is None else lo_acc + lo
        hi_acc = hi if hi_acc is None else hi_acc + hi
    routed = jnp.concatenate([lo_acc, hi_acc], axis=1)
    rows = i * tc + lax.broadcasted_iota(I32, (tc, 1), 0)
    g = jnp.where(rows >= n_lat, g_ref[1:2, :], g_ref[0:1, :])
    o_ref[...] = x_ref[...] + g * (routed + sh_ref[...].astype(F32))


def _combine(y_slots, slots, x1, sh, mod, gate_blk, n_lat, m_rows):
    d = x1.shape[1]
    tc = COMBINE_TILE
    nsteps = m_rows // tc
    last = nsteps - 1
    return pl.pallas_call(
        functools.partial(_combine_kernel, n_lat),
        grid=(nsteps,),
        in_specs=[pl.BlockSpec((1, 1, TOP_K * tc), lambda i: (i, 0, 0), memory_space=pltpu.SMEM),
                  pl.BlockSpec((1, 1, TOP_K * tc), lambda i: (jnp.minimum(i + 1, last), 0, 0),
                               memory_space=pltpu.SMEM),
                  pl.BlockSpec(memory_space=pl.ANY),
                  pl.BlockSpec((tc, d), lambda i: (i, 0)),
                  pl.BlockSpec((tc, d), lambda i: (i, 0)),
                  pl.BlockSpec((2, d), lambda i: (0, gate_blk))],
        out_specs=pl.BlockSpec((tc, d), lambda i: (i, 0)),
        out_shape=jax.ShapeDtypeStruct((m_rows, d), F32),
        scratch_shapes=[pltpu.VMEM((2, TOP_K * tc, d // 2), U32), pltpu.SemaphoreType.DMA((2,))],
        compiler_params=_params(("arbitrary",)),
        name="moe_combine",
    )(slots, slots, y_slots, x1, sh, mod)


def _dispatch_tables(sel, idx, wk, m_rows):
    tm = MOE_TILE
    tk = m_rows * TOP_K
    n_blk = (tk + tm - 1) // tm + N_EXPERTS
    cs = jnp.cumsum(sel, axis=1)
    counts = cs[:, -1]
    padded = ((counts + tm - 1) // tm) * tm
    pend = jnp.cumsum(padded)
    pstart = pend - padded
    dest_et = pstart[:, None] + cs - 1
    dest = jnp.take_along_axis(dest_et, idx, axis=0)
    blk_start = jnp.arange(n_blk, dtype=I32) * tm
    blk_e = jnp.minimum(jnp.sum((pend[None, :] <= blk_start[:, None]).astype(I32), axis=1), N_EXPERTS - 1)
    n_fill = n_blk * tm - tk - N_EXPERTS * tm
    f = jnp.arange(tk, dtype=I32)
    real_key = idx.T.reshape(-1) * (2 * tk) + f
    e_d = jnp.arange(N_EXPERTS, dtype=I32)[:, None]
    j_d = jnp.arange(tm, dtype=I32)[None, :]
    big = N_EXPERTS * 2 * tk
    pad_key = jnp.where(j_d < (padded - counts)[:, None], e_d * (2 * tk) + tk + j_d, big + e_d * tm + j_d)
    keys = jnp.concatenate([real_key, pad_key.reshape(-1), jnp.full((n_fill,), big + N_EXPERTS * tm, I32)])
    n_pad = keys.shape[0] - tk
    toks = jnp.concatenate([f // TOP_K, jnp.zeros((n_pad,), I32)])
    ws = jnp.concatenate([wk.T.reshape(-1), jnp.zeros((n_pad,), F32)])
    _, slot_tok, slot_w = lax.sort((keys, toks, ws), num_keys=1)
    n_used = (pend[-1] // tm).astype(I32).reshape(1)
    tc = COMBINE_TILE
    slots = dest.reshape(TOP_K, m_rows // tc, tc).transpose(1, 0, 2).reshape(m_rows // tc, 1, TOP_K * tc)
    return n_used, blk_e, slot_tok, slot_w, slots.astype(I32), n_blk


def kernel(x, c, ctx, c_ctx, w_mod, b_mod, g_norm1, g_norm2, w_in, conv_w, diff_qnorm, diff_knorm, diff_lambda, diff_subln, swa_qnorm, swa_knorm, swa_sink, na_qnorm, na_knorm, na_rpb, w_gate, b_gate, w_branch, w_out, w_router, b_router, w_exp_gate, w_exp_up, w_exp_down, w_sh_gate, w_sh_up, w_sh_down):
    bsz, n, d = x.shape
    assert bsz == 1
    n_ctx = ctx.shape[1]
    t_all = n + n_ctx
    depth = w_in.shape[0]
    bw = d // 4
    cdt = MXU_DTYPE

    xa = jnp.concatenate([x[0], ctx[0]], axis=0)
    cc = jnp.stack([c[0], c_ctx], axis=0).astype(F32)
    rope_d = _rope_tables(n, t_all, DIFF_QK // 2)
    rope_s = _rope_tables(n, t_all, HEAD_DIM // 2)

    o_conv = 0
    o_dq, o_dk, o_dv = 3 * bw, 4 * bw, 5 * bw
    o_sq = 6 * bw
    o_sk = o_sq + bw
    o_sv = o_sk + SWA_KV_HEADS * HEAD_DIM
    o_nq = o_sv + SWA_KV_HEADS * HEAD_DIM
    o_nk, o_nv = o_nq + bw, o_nq + 2 * bw

    for l in range(depth):
        update_ctx = l < depth - 1
        nq = t_all if update_ctx else n
        lam_init = 0.8 - 0.6 * math.exp(-0.3 * l)
        mod = _modvec(cc, w_mod, b_mod, l)

        h = _rmsmod(xa, g_norm1[l], mod, 0, 1, n, t_all)
        p = _matmul(h, w_in[l].astype(cdt), cdt, t_all)

        a = _short_conv(p, conv_w[l].astype(F32), n, nq)
        q1z, q2z = _prep(p, o_dq, N_HEADS, diff_qnorm[l], DIFF_QK, rope_d, split=True,
                         scale=DIFF_QK ** -0.5 * math.log2(math.e))
        kd = _prep(p, o_dk, N_HEADS, diff_knorm[l], DIFF_QK, rope_d)
        qs = _prep(p, o_sq, N_HEADS, swa_qnorm[l], HEAD_DIM, rope_s)
        ks = _prep(p, o_sk, SWA_KV_HEADS, swa_knorm[l], HEAD_DIM, rope_s)
        qn = _prep(p, o_nq, N_HEADS, na_qnorm[l], HEAD_DIM, None)
        kn = _prep(p, o_nk, N_HEADS, na_knorm[l], HEAD_DIM, None)

        v_d = p[:, o_dv:o_dv + bw].reshape(t_all, N_HEADS, HEAD_DIM)
        vx = jnp.concatenate([v_d, jnp.ones_like(v_d)], axis=-1).reshape(t_all, 2 * bw)
        yb = _diff_attention(q1z, q2z, kd, vx, diff_lambda[l], diff_subln[l], lam_init, n, True)
        if update_ctx:
            yb_c = _diff_attention(q1z, q2z, kd, vx, diff_lambda[l], diff_subln[l], lam_init, n, False)
            yb = jnp.concatenate([yb, yb_c], axis=0)
        ys = _swa_attention(qs, ks, p, o_sv, swa_sink[l], n, nq)
        yn = _na_attention(qn, kn, p, o_nv, _na_bias_table(na_rpb[l]), n, nq)

        acc = _merge(h, (a, yb, ys, yn), w_gate[l].astype(cdt), b_gate[l], w_branch[l].astype(cdt), nq)
        x1 = _matmul_residual(acc, w_out[l].astype(cdt), xa, mod, 2, n, nq)

        h2, h2p = _rmsmod(x1, g_norm2[l], mod, 3, 4, n, nq, packed=True)
        sel, idx, wk = _router(h2, w_router[l].T.astype(cdt), b_router[l], nq)
        n_used, blk_e, slot_tok, slot_w, slots, n_blk = _dispatch_tables(sel, idx, wk, nq)
        y_slots = _experts(h2p, w_exp_gate, w_exp_up, w_exp_down, l, n_used, blk_e, slot_tok, slot_w, n_blk)
        wsh = jnp.concatenate([w_sh_gate[l], w_sh_up[l]], axis=-1).astype(cdt)
        sh = _shared_expert(h2, wsh, w_sh_down[l].astype(cdt), nq)
        xa = _combine(y_slots, slots, x1, sh, mod, 5, n, nq)

    return xa[:n].reshape(bsz, n, d)
```

```python
import functools
import math

import jax
import jax.numpy as jnp
from jax import lax
from jax.experimental import pallas as pl
from jax.experimental.pallas import tpu as pltpu

GRID_W = 64
HEAD_DIM = 128
DIFF_QK = 64
SWA_WINDOW = 128
SWA_BLOCK = 128
SWA_KV_HEADS = 2
N_HEADS = 8
NA_ROWS = 8
NA_COLS = 16
ROPE_BASE = 10000.0
N_EXPERTS = 64
N_GROUPS = 8
TOPK_GROUPS = 4
TOP_K = 8
EXPERT_DIM = 256
ROUTED_SCALE = 2.5
EPS = 1e-6
NEG = -1e30

LANES = 128
SUBLANES = 8
V7X_VMEM_BYTES = 64 * 1024 * 1024

MXU_DTYPE = jnp.bfloat16
MOE_TILE = 256
COMBINE_TILE = 64

F32 = jnp.float32
I32 = jnp.int32
U32 = jnp.uint32


def _params(sem, vmem_mb=48):
    return pltpu.CompilerParams(dimension_semantics=sem, vmem_limit_bytes=vmem_mb * 1024 * 1024)


def _pick(m, cands):
    for c in cands:
        if m % c == 0:
            return c
    raise ValueError(f"no tile for {m} in {cands}")


def _nt_dot(a, b):
    return lax.dot_general(a, b, (((1,), (1,)), ((), ())), preferred_element_type=F32)


def _dot(a, b):
    return jnp.dot(a, b, preferred_element_type=F32)


def _sigmoid(x):
    return 1.0 / (1.0 + jnp.exp(-x))


def _modvec_kernel(s_ref, w_ref, b_ref, o_ref):
    _, d, tn = w_ref.shape
    rows = []
    for r in range(2):
        c = s_ref[r]
        s = c * _sigmoid(c)
        parts = []
        for j in range(tn // LANES):
            prod = w_ref[0, :, j * LANES:(j + 1) * LANES] * s
            part = jnp.sum(prod.reshape(d // SUBLANES, SUBLANES, LANES), axis=0)
            parts.append(jnp.sum(part, axis=0, keepdims=True))
        rows.append(jnp.concatenate(parts, axis=1))
    is_first = lax.broadcasted_iota(I32, (2, tn), 0) == 0
    o_ref[...] = jnp.where(is_first, rows[0], rows[1]) + b_ref[0]


def _modvec(cc, w, b, layer):
    depth, d, n = w.shape
    tn = 512
    s_rep = jnp.broadcast_to(cc[:, :, None], (2, d, LANES))
    return pl.pallas_call(
        _modvec_kernel,
        grid=(n // tn,),
        in_specs=[pl.BlockSpec((2, d, LANES), lambda j: (0, 0, 0)),
                  pl.BlockSpec((1, d, tn), lambda j: (layer, 0, j)),
                  pl.BlockSpec((1, 1, tn), lambda j: (layer, 0, j))],
        out_specs=pl.BlockSpec((2, tn), lambda j: (0, j)),
        out_shape=jax.ShapeDtypeStruct((2, n), F32),
        compiler_params=_params(("arbitrary",)),
        name="modvec",
    )(s_rep, w, b.reshape(depth, 1, n))


def _pack_halves(v):
    c = v.shape[1] // 2
    bits = lax.bitcast_convert_type(v.astype(jnp.bfloat16).astype(F32), U32)
    return (bits[:, c:] & jnp.uint32(0xFFFF0000)) | (bits[:, :c] >> 16)


def _unpack_halves(w):
    lo = lax.bitcast_convert_type(w << 16, F32)
    hi = lax.bitcast_convert_type(w & jnp.uint32(0xFFFF0000), F32)
    return lo, hi


def _rmsmod_kernel(n_lat, packed, x_ref, g_ref, sh_ref, sc_ref, o_ref, *p_ref):
    tr = x_ref.shape[0]
    x = x_ref[...]
    ms = jnp.mean(x * x, axis=-1, keepdims=True)
    y = x * lax.rsqrt(ms + EPS) * g_ref[...]
    rows = pl.program_id(0) * tr + lax.broadcasted_iota(I32, (tr, 1), 0)
    is_ctx = rows >= n_lat
    sh = jnp.where(is_ctx, sh_ref[1:2, :], sh_ref[0:1, :])
    sc = jnp.where(is_ctx, sc_ref[1:2, :], sc_ref[0:1, :])
    h = y * (1.0 + sc) + sh
    o_ref[...] = h.astype(o_ref.dtype)
    if packed:
        p_ref[0][...] = _pack_halves(h)


def _rmsmod(x, g, mod, shift_blk, scale_blk, n_lat, m_rows, packed=False):
    d = x.shape[1]
    tr = 256
    out_shape = [jax.ShapeDtypeStruct((m_rows, d), MXU_DTYPE)]
    out_specs = [pl.BlockSpec((tr, d), lambda i: (i, 0))]
    if packed:
        out_shape.append(jax.ShapeDtypeStruct((m_rows, d // 2), U32))
        out_specs.append(pl.BlockSpec((tr, d // 2), lambda i: (i, 0)))
    res = pl.pallas_call(
        functools.partial(_rmsmod_kernel, n_lat, packed),
        grid=(m_rows // tr,),
        in_specs=[pl.BlockSpec((tr, d), lambda i: (i, 0)),
                  pl.BlockSpec((1, d), lambda i: (0, 0)),
                  pl.BlockSpec((2, d), lambda i: (0, shift_blk)),
                  pl.BlockSpec((2, d), lambda i: (0, scale_blk))],
        out_specs=out_specs,
        out_shape=out_shape,
        compiler_params=_params(("arbitrary",)),
        name="rmsmod",
    )(x, g.reshape(1, d), mod, mod)
    return res if packed else res[0]


def _mm_kernel(a_ref, w_ref, o_ref):
    o_ref[...] = _dot(a_ref[...], w_ref[...]).astype(o_ref.dtype)


def _mm_res_kernel(n_lat, a_ref, w_ref, r_ref, g_ref, o_ref):
    tm = a_ref.shape[0]
    acc = _dot(a_ref[...], w_ref[...])
    rows = pl.program_id(0) * tm + lax.broadcasted_iota(I32, (tm, 1), 0)
    g = jnp.where(rows >= n_lat, g_ref[1:2, :], g_ref[0:1, :])
    o_ref[...] = r_ref[...] + g * acc


def _matmul(a, w, out_dtype, m_rows):
    k = a.shape[1]
    n = w.shape[1]
    tm = _pick(m_rows, (1280, 1024, 640, 512, 256))
    tn = 512
    return pl.pallas_call(
        _mm_kernel,
        grid=(m_rows // tm, n // tn),
        in_specs=[pl.BlockSpec((tm, k), lambda i, j: (i, 0)),
                  pl.BlockSpec((k, tn), lambda i, j: (0, j))],
        out_specs=pl.BlockSpec((tm, tn), lambda i, j: (i, j)),
        out_shape=jax.ShapeDtypeStruct((m_rows, n), out_dtype),
        compiler_params=_params(("arbitrary", "arbitrary")),
        name="matmul",
    )(a, w)


def _matmul_residual(a, w, res, mod, gate_blk, n_lat, m_rows):
    k = a.shape[1]
    n = w.shape[1]
    tm = _pick(m_rows, (1280, 1024, 640, 512, 256))
    tn = 512
    nb = n // tn
    return pl.pallas_call(
        functools.partial(_mm_res_kernel, n_lat),
        grid=(m_rows // tm, nb),
        in_specs=[pl.BlockSpec((tm, k), lambda i, j: (i, 0)),
                  pl.BlockSpec((k, tn), lambda i, j: (0, j)),
                  pl.BlockSpec((tm, tn), lambda i, j: (i, j)),
                  pl.BlockSpec((2, tn), lambda i, j: (0, gate_blk * nb + j))],
        out_specs=pl.BlockSpec((tm, tn), lambda i, j: (i, j)),
        out_shape=jax.ShapeDtypeStruct((m_rows, n), F32),
        compiler_params=_params(("arbitrary", "arbitrary")),
        name="matmul_residual",
    )(a, w, res, mod)


def _conv_kernel(n_lat, t_all, b_ref, c_ref, x_ref, cp_ref, xp_ref, cn_ref, xn_ref, w_ref, o_ref):
    tr = b_ref.shape[0]
    u = c_ref[...].astype(F32) * x_ref[...].astype(F32)
    u_prev_row = cp_ref[SUBLANES - 1:SUBLANES, :].astype(F32) * xp_ref[SUBLANES - 1:SUBLANES, :].astype(F32)
    u_next_row = cn_ref[0:1, :].astype(F32) * xn_ref[0:1, :].astype(F32)
    r = lax.broadcasted_iota(I32, (tr, 1), 0)
    tok = pl.program_id(0) * tr + r
    up = jnp.where(r == 0, u_prev_row, pltpu.roll(u, 1, 0))
    un = jnp.where(r == tr - 1, u_next_row, pltpu.roll(u, tr - 1, 0))
    up = jnp.where((tok == 0) | (tok == n_lat), 0.0, up)
    un = jnp.where((tok == n_lat - 1) | (tok == t_all - 1), 0.0, un)
    y = w_ref[0:1, :] * up + w_ref[1:2, :] * u + w_ref[2:3, :] * un
    o_ref[...] = (b_ref[...].astype(F32) * y).astype(o_ref.dtype)


def _short_conv(p, conv_w, n_lat, m_rows):
    t_all = p.shape[0]
    cw = conv_w.shape[1]
    tr = 256
    tc = 512
    nc = cw // tc
    rb = tr // SUBLANES
    last = t_all // SUBLANES - 1
    return pl.pallas_call(
        functools.partial(_conv_kernel, n_lat, t_all),
        grid=(m_rows // tr, nc),
        in_specs=[pl.BlockSpec((tr, tc), lambda i, j: (i, j)),
                  pl.BlockSpec((tr, tc), lambda i, j: (i, nc + j)),
                  pl.BlockSpec((tr, tc), lambda i, j: (i, 2 * nc + j)),
                  pl.BlockSpec((SUBLANES, tc), lambda i, j: (jnp.maximum(i * rb - 1, 0), nc + j)),
                  pl.BlockSpec((SUBLANES, tc), lambda i, j: (jnp.maximum(i * rb - 1, 0), 2 * nc + j)),
                  pl.BlockSpec((SUBLANES, tc), lambda i, j: (jnp.minimum((i + 1) * rb, last), nc + j)),
                  pl.BlockSpec((SUBLANES, tc), lambda i, j: (jnp.minimum((i + 1) * rb, last), 2 * nc + j)),
                  pl.BlockSpec((3, tc), lambda i, j: (0, j))],
        out_specs=pl.BlockSpec((tr, tc), lambda i, j: (i, j)),
        out_shape=jax.ShapeDtypeStruct((m_rows, cw), MXU_DTYPE),
        compiler_params=_params(("arbitrary", "arbitrary")),
        name="short_conv",
    )(p, p, p, p, p, p, p, conv_w)


def _prep_kernel(n_heads, group, half, split, scale, x_ref, g_ref, *rest):
    if half:
        cos_ref, sa_ref, sb_ref = rest[:3]
        outs = rest[3:]
    else:
        outs = rest
    lane = lax.broadcasted_iota(I32, (1, LANES), 1)
    lo = lane < DIFF_QK
    g = g_ref[...]
    for h in range(n_heads):
        x = x_ref[:, h * LANES:(h + 1) * LANES].astype(F32)
        sq = x * x
        if group == LANES:
            ms = jnp.mean(sq, axis=-1, keepdims=True)
        else:
            s_lo = jnp.sum(jnp.where(lo, sq, 0.0), axis=-1, keepdims=True)
            s_hi = jnp.sum(jnp.where(lo, 0.0, sq), axis=-1, keepdims=True)
            ms = jnp.where(lo, s_lo, s_hi) * (1.0 / group)
        y = x * lax.rsqrt(ms + EPS) * g
        if half:
            y = (y * cos_ref[...] + pltpu.roll(y, LANES - half, 1) * sa_ref[...]
                 + pltpu.roll(y, half, 1) * sb_ref[...])
        if scale != 1.0:
            y = y * scale
        if split:
            outs[0][:, h * LANES:(h + 1) * LANES] = jnp.where(lo, y, 0.0).astype(outs[0].dtype)
            outs[1][:, h * LANES:(h + 1) * LANES] = jnp.where(lo, 0.0, y).astype(outs[1].dtype)
        else:
            outs[0][:, h * LANES:(h + 1) * LANES] = y.astype(outs[0].dtype)


def _prep(p, col0, n_heads, gain, group, rope, split=False, scale=1.0):
    t_all = p.shape[0]
    tr = _pick(t_all, (640, 256))
    hpb = _pick(math.gcd(col0 // LANES, n_heads), (8, 4, 2, 1))
    w = hpb * LANES
    cb = col0 // w
    g = jnp.tile(gain.astype(F32), LANES // gain.shape[0]).reshape(1, LANES)
    in_specs = [pl.BlockSpec((tr, w), lambda i, j: (i, cb + j)),
                pl.BlockSpec((1, LANES), lambda i, j: (0, 0))]
    args = [p, g]
    half = 0
    if rope is not None:
        half = rope[3]
        in_specs += [pl.BlockSpec((tr, LANES), lambda i, j: (i, 0))] * 3
        args += list(rope[:3])
    n_out = 2 if split else 1
    res = pl.pallas_call(
        functools.partial(_prep_kernel, hpb, group, half, split, scale),
        grid=(t_all // tr, n_heads // hpb),
        in_specs=in_specs,
        out_specs=[pl.BlockSpec((tr, w), lambda i, j: (i, j))] * n_out,
        out_shape=[jax.ShapeDtypeStruct((t_all, n_heads * LANES), MXU_DTYPE)] * n_out,
        compiler_params=_params(("arbitrary", "arbitrary")),
        name="qk_prep",
    )(*args)
    return res if split else res[0]


def _rope_tables(n_lat, t_all, seg):
    half = seg // 2
    n_rows = n_lat // GRID_W
    lane = jnp.arange(LANES)
    freqs = ROPE_BASE ** (-jnp.arange(0, seg, 2, dtype=F32) / seg)
    f_lane = freqs[lane % half]
    use_col = ((lane // seg) % 2 == 1)[None, None, :]
    ang_r = jnp.arange(n_rows, dtype=F32)[:, None] * f_lane[None, :]
    ang_c = jnp.arange(GRID_W, dtype=F32)[:, None] * f_lane[None, :]

    def expand(fn):
        full = jnp.where(use_col, fn(ang_c)[None, :, :], fn(ang_r)[:, None, :])
        return full.reshape(n_lat, LANES)

    cos = expand(jnp.cos)
    sin = expand(jnp.sin)
    first = (lane % seg) < half
    sa = jnp.where(first[None, :], -sin, 0.0)
    sb = jnp.where(first[None, :], 0.0, sin)
    pad = t_all - n_lat
    cos = jnp.concatenate([cos, jnp.ones((pad, LANES), F32)], axis=0)
    sa = jnp.concatenate([sa, jnp.zeros((pad, LANES), F32)], axis=0)
    sb = jnp.concatenate([sb, jnp.zeros((pad, LANES), F32)], axis=0)
    return cos, sa, sb, half


DIFF_TQ = 512
DIFF_TK = 512
DIFF_GROUP = 8


def _diff_kernel(n_pairs, tkc, lam_init, q1_ref, q2_ref, k_ref, vx_ref, dl_ref, gs_ref, o_ref, s_scr):
    tq = q1_ref.shape[0]
    t_all = k_ref.shape[0]
    n_lat = 2 * n_pairs * tkc
    q1 = q1_ref[...]
    q2 = q2_ref[...]

    def scores(kc):
        return _nt_dot(q1, kc), _nt_dot(q2, kc)

    def fold(s, vc, m, a):
        mn = jnp.maximum(m, jnp.max(s, axis=-1, keepdims=True))
        alpha = jnp.exp2(m - mn)
        p = jnp.exp2(s - mn).astype(vc.dtype)
        return mn, alpha * a + _dot(p, vc)

    def lat_scores_to(buf, c):
        off = pl.multiple_of(c * tkc, tkc)
        s1, s2 = scores(k_ref[pl.ds(off, tkc), :])
        s_scr[buf, 0] = s1
        s_scr[buf, 1] = s2

    def lat_fold_from(buf, c, carry):
        m1, a1, m2, a2 = carry
        off = pl.multiple_of(c * tkc, tkc)
        vc = vx_ref[pl.ds(off, tkc), :]
        m1, a1 = fold(s_scr[buf, 0], vc, m1, a1)
        m2, a2 = fold(s_scr[buf, 1], vc, m2, a2)
        return m1, a1, m2, a2

    minit = jnp.full((tq, 1), -jnp.inf, F32)
    ainit = jnp.zeros((tq, 2 * HEAD_DIM), F32)
    carry = (minit, ainit, minit, ainit)
    n_chunks = 2 * n_pairs
    group = _pick(n_chunks, (DIFF_GROUP, 4, 2)) if n_pairs > 0 else 0
    if n_pairs > 0:
        lat_scores_to(0, 0)

        def run_group(c0, carry, last):
            for u in range(group):
                if not (last and u == group - 1):
                    lat_scores_to((u + 1) % 2, c0 + u + 1)
                    carry = lat_fold_from(u % 2, c0 + u, carry)
            return carry

        carry = lax.fori_loop(0, n_chunks // group - 1,
                              lambda j, cr: run_group(j * group, cr, False), carry)
        carry = run_group(n_chunks - group, carry, True)
    s1c, s2c = scores(k_ref[n_lat:t_all, :])
    if n_pairs > 0:
        carry = lat_fold_from(1, n_chunks - 1, carry)
    m1, a1, m2, a2 = carry
    vcx = vx_ref[n_lat:t_all, :]
    m1, a1 = fold(s1c, vcx, m1, a1)
    m2, a2 = fold(s2c, vcx, m2, a2)

    dl = dl_ref[...]
    lam = (jnp.exp(jnp.sum(dl[0:1, :] * dl[1:2, :], axis=-1, keepdims=True))
           - jnp.exp(jnp.sum(dl[2:3, :] * dl[3:4, :], axis=-1, keepdims=True)) + lam_init)
    o = (a1[:, :HEAD_DIM] / a1[:, HEAD_DIM:HEAD_DIM + 1]
         - lam * (a2[:, :HEAD_DIM] / a2[:, HEAD_DIM:HEAD_DIM + 1]))
    ms = jnp.mean(o * o, axis=-1, keepdims=True)
    o = o * lax.rsqrt(ms + EPS) * gs_ref[...] * (1.0 - lam_init)
    o_ref[...] = o.astype(o_ref.dtype)


def _diff_attention(q1z, q2z, kd, vx, diff_lambda, subln, lam_init, n_lat, latent):
    t_all = kd.shape[0]
    n_ctx = t_all - n_lat
    if latent:
        tq, tkc, rows, blk0 = DIFF_TQ, DIFF_TK, n_lat, 0
        assert n_lat % (2 * tkc) == 0 and n_lat % tq == 0
        n_pairs = n_lat // (2 * tkc)
    else:
        tq, tkc, rows, blk0, n_pairs = n_ctx, DIFF_TK, n_ctx, n_lat // n_ctx, 0
        assert n_lat % n_ctx == 0
    if n_pairs == 0:
        k_spec = pl.BlockSpec((n_ctx, LANES), lambda h, i: (n_lat // n_ctx, h))
        v_spec = pl.BlockSpec((n_ctx, 2 * LANES), lambda h, i: (n_lat // n_ctx, h))
    else:
        k_spec = pl.BlockSpec((t_all, LANES), lambda h, i: (0, h))
        v_spec = pl.BlockSpec((t_all, 2 * LANES), lambda h, i: (0, h))
    return pl.pallas_call(
        functools.partial(_diff_kernel, n_pairs, tkc, lam_init),
        grid=(N_HEADS, rows // tq),
        in_specs=[pl.BlockSpec((tq, LANES), lambda h, i: (blk0 + i, h)),
                  pl.BlockSpec((tq, LANES), lambda h, i: (blk0 + i, h)),
                  k_spec, v_spec,
                  pl.BlockSpec((4, DIFF_QK), lambda h, i: (0, 0)),
                  pl.BlockSpec((1, HEAD_DIM), lambda h, i: (0, 0))],
        out_specs=pl.BlockSpec((tq, LANES), lambda h, i: (i, h)),
        out_shape=jax.ShapeDtypeStruct((rows, N_HEADS * HEAD_DIM), MXU_DTYPE),
        scratch_shapes=[pltpu.VMEM((2, 2, tq, tkc), F32)],
        compiler_params=_params(("arbitrary", "arbitrary")),
        name="diff_attention" if latent else "diff_attention_ctx",
    )(q1z, q2z, kd, vx, diff_lambda.astype(F32), subln.astype(F32).reshape(1, HEAD_DIM))


def _swa_kernel(n_lat, t_all, q_ref, k_ref, v_ref, sink_ref, o_ref):
    w = SWA_BLOCK
    g = q_ref.shape[1] // HEAD_DIM
    b = pl.program_id(1)
    reach = jnp.where(b * w < n_lat, SWA_WINDOW, -1)
    scale = HEAD_DIM ** -0.5
    start = jnp.clip((b - 1) * w, 0, n_lat - 3 * w)
    start = pl.multiple_of(start, w)
    kw = k_ref[pl.ds(start, 3 * w), :]
    vw = v_ref[pl.ds(start, 3 * w), :]
    kc = k_ref[n_lat:t_all, :]
    vc = v_ref[n_lat:t_all, :]
    q = jnp.concatenate([q_ref[:, j * HEAD_DIM:(j + 1) * HEAD_DIM] for j in range(g)], axis=0)
    s_w = _nt_dot(q, kw) * scale
    s_c = _nt_dot(q, kc) * scale
    qpos = b * w + lax.broadcasted_iota(I32, (w, 3 * w), 0)
    kpos = start + lax.broadcasted_iota(I32, (w, 3 * w), 1)
    mask = jnp.abs(qpos - kpos) <= reach
    mask = jnp.concatenate([mask] * g, axis=0)
    s_w = jnp.where(mask, s_w, NEG)
    sink = jnp.concatenate([jnp.broadcast_to(sink_ref[0, :, j:j + 1], (w, 1)) for j in range(g)], axis=0)
    m = jnp.maximum(jnp.maximum(jnp.max(s_w, axis=-1, keepdims=True), jnp.max(s_c, axis=-1, keepdims=True)), sink)
    p_w = jnp.exp(s_w - m)
    p_c = jnp.exp(s_c - m)
    l = jnp.sum(p_w, axis=-1, keepdims=True) + jnp.sum(p_c, axis=-1, keepdims=True) + jnp.exp(sink - m)
    inv = 1.0 / l
    o = _dot((p_w * inv).astype(vw.dtype), vw) + _dot((p_c * inv).astype(vc.dtype), vc)
    for j in range(g):
        o_ref[:, j * HEAD_DIM:(j + 1) * HEAD_DIM] = o[j * w:(j + 1) * w, :].astype(o_ref.dtype)


def _swa_attention(qs, ks, p, v_col0, sink, n_lat, nq):
    t_all = ks.shape[0]
    g = N_HEADS // SWA_KV_HEADS
    vb = v_col0 // LANES
    sink3 = sink.astype(F32).reshape(SWA_KV_HEADS, 1, g)
    return pl.pallas_call(
        functools.partial(_swa_kernel, n_lat, t_all),
        grid=(SWA_KV_HEADS, nq // SWA_BLOCK),
        in_specs=[pl.BlockSpec((SWA_BLOCK, g * HEAD_DIM), lambda kv, b: (b, kv)),
                  pl.BlockSpec((t_all, LANES), lambda kv, b: (0, kv)),
                  pl.BlockSpec((t_all, LANES), lambda kv, b: (0, vb + kv)),
                  pl.BlockSpec((1, 1, g), lambda kv, b: (kv, 0, 0))],
        out_specs=pl.BlockSpec((SWA_BLOCK, g * HEAD_DIM), lambda kv, b: (b, kv)),
        out_shape=jax.ShapeDtypeStruct((nq, N_HEADS * HEAD_DIM), MXU_DTYPE),
        compiler_params=_params(("arbitrary", "arbitrary")),
        name="swa_attention",
    )(qs, ks, p, sink3)


NA_QROWS = 4
NA_WIN_ROWS = NA_QROWS + NA_ROWS
NA_VARIANTS = 4


def _na_kernel(n_lat, t_all, q_ref, k_ref, v_ref, bias_ref, o_ref):
    n_rows = n_lat // GRID_W
    n_grp = n_rows // NA_QROWS
    win = NA_WIN_ROWS * GRID_W
    scale = HEAD_DIM ** -0.5
    g = pl.program_id(1)
    wrow = jnp.clip(NA_QROWS * g - NA_ROWS // 2, 0, n_rows - NA_WIN_ROWS)
    variant = jnp.where(g >= n_grp, 3, jnp.where(g == 0, 0, jnp.where(g == n_grp - 1, 2, 1)))
    off = pl.multiple_of(wrow * GRID_W, GRID_W)
    kn = k_ref[pl.ds(off, win), :]
    vn = v_ref[pl.ds(off, win), :]
    kc = k_ref[n_lat:t_all, :]
    vc = v_ref[n_lat:t_all, :]
    q = q_ref[...]
    s_n = _nt_dot(q, kn) * scale + bias_ref[0, variant]
    s_c = _nt_dot(q, kc) * scale
    m = jnp.maximum(jnp.max(s_n, axis=-1, keepdims=True), jnp.max(s_c, axis=-1, keepdims=True))
    p_n = jnp.exp(s_n - m)
    p_c = jnp.exp(s_c - m)
    inv = 1.0 / (jnp.sum(p_n, axis=-1, keepdims=True) + jnp.sum(p_c, axis=-1, keepdims=True))
    o = _dot((p_n * inv).astype(vn.dtype), vn) + _dot((p_c * inv).astype(vc.dtype), vc)
    o_ref[...] = o.astype(o_ref.dtype)


def _na_bias_table(rpb):
    h = rpb.shape[0]
    c = jnp.arange(GRID_W)
    c0 = jnp.clip(c - NA_COLS // 2, 0, GRID_W - NA_COLS)
    kc = jnp.arange(GRID_W)
    inwin = (kc[None, :] >= c0[:, None]) & (kc[None, :] < c0[:, None] + NA_COLS)
    col_off = jnp.clip(kc[None, :] - c[:, None] + NA_COLS - 1, 0, 2 * NA_COLS - 2)
    toe = jnp.where(inwin[None, None], rpb.astype(F32)[:, :, col_off], NEG)
    neg_blk = jnp.full((h, GRID_W, GRID_W), NEG, F32)
    half = NA_ROWS // 2
    variants = []
    for shift, lo, hi in ((0, 0, NA_ROWS), (-half, None, None), (-NA_ROWS, half, NA_WIN_ROWS)):
        rows = []
        for j in range(NA_QROWS):
            blocks = []
            for a in range(NA_WIN_ROWS):
                dr = a - j + shift
                if lo is None:
                    valid = -half <= dr < half
                else:
                    valid = lo <= a < hi
                blocks.append(toe[:, dr + NA_ROWS - 1] if valid else neg_blk)
            rows.append(jnp.concatenate(blocks, axis=-1))
        variants.append(jnp.concatenate(rows, axis=1))
    variants.append(jnp.full_like(variants[0], NEG))
    return jnp.stack(variants, axis=1)


def _na_attention(qn, kn, p, v_col0, bias, n_lat, nq):
    t_all = kn.shape[0]
    vb = v_col0 // LANES
    tq = NA_QROWS * GRID_W
    assert n_lat // GRID_W >= NA_WIN_ROWS and (n_lat // GRID_W) % NA_QROWS == 0 and (t_all - n_lat) % tq == 0
    return pl.pallas_call(
        functools.partial(_na_kernel, n_lat, t_all),
        grid=(N_HEADS, nq // tq),
        in_specs=[pl.BlockSpec((tq, LANES), lambda h, i: (i, h)),
                  pl.BlockSpec((t_all, LANES), lambda h, i: (0, h)),
                  pl.BlockSpec((t_all, LANES), lambda h, i: (0, vb + h)),
                  pl.BlockSpec((1, NA_VARIANTS, tq, NA_WIN_ROWS * GRID_W), lambda h, i: (h, 0, 0, 0))],
        out_specs=pl.BlockSpec((tq, LANES), lambda h, i: (i, h)),
        out_shape=jax.ShapeDtypeStruct((nq, N_HEADS * HEAD_DIM), MXU_DTYPE),
        compiler_params=_params(("arbitrary", "arbitrary")),
        name="na_attention",
    )(qn, kn, p, bias)


def _merge_kernel(h_ref, y0_ref, y1_ref, y2_ref, y3_ref, wg_ref, bg_ref, wb_ref, o_ref):
    h = h_ref[...]
    acc = None
    for i, y_ref in enumerate((y0_ref, y1_ref, y2_ref, y3_ref)):
        gate = _sigmoid(_dot(h, wg_ref[i]) + bg_ref[i])
        term = gate * _dot(y_ref[...], wb_ref[i])
        acc = term if acc is None else acc + term
    o_ref[...] = acc.astype(o_ref.dtype)


def _merge(h, ys, wg, bg, wb, m_rows):
    d = h.shape[1]
    bw = ys[0].shape[1]
    tm = _pick(m_rows, (1280, 1024, 640, 512, 256))
    tn = 256
    once = pl.Buffered(1)
    return pl.pallas_call(
        _merge_kernel,
        grid=(m_rows // tm, d // tn),
        in_specs=[pl.BlockSpec((tm, d), lambda i, j: (i, 0), pipeline_mode=once)]
        + [pl.BlockSpec((tm, bw), lambda i, j: (i, 0), pipeline_mode=once)] * 4
        + [pl.BlockSpec((4, d, tn), lambda i, j: (0, 0, j)),
           pl.BlockSpec((4, 1, tn), lambda i, j: (0, 0, j)),
           pl.BlockSpec((4, bw, tn), lambda i, j: (0, 0, j))],
        out_specs=pl.BlockSpec((tm, tn), lambda i, j: (i, j)),
        out_shape=jax.ShapeDtypeStruct((m_rows, d), MXU_DTYPE),
        compiler_params=_params(("arbitrary", "arbitrary"), vmem_mb=56),
        name="merge",
    )(h, *ys, wg, bg.astype(F32).reshape(4, 1, d), wb)


def _router_kernel(h_ref, w_ref, b_ref, sel_ref, idx_ref, wk_ref):
    tr = h_ref.shape[0]
    per = N_EXPERTS // N_GROUPS
    scores = _sigmoid(_nt_dot(w_ref[...], h_ref[...]))
    choice = scores + b_ref[...]
    sub = lax.broadcasted_iota(I32, (per, tr), 0)
    neg_inf = -jnp.inf
    slabs = [choice[g * per:(g + 1) * per, :] for g in range(N_GROUPS)]
    grp = []
    for c in slabs:
        m1 = jnp.max(c, axis=0, keepdims=True)
        i1 = jnp.min(jnp.where(c == m1, sub, per), axis=0, keepdims=True)
        m2 = jnp.max(jnp.where(sub == i1, neg_inf, c), axis=0, keepdims=True)
        grp.append(m1 + m2)
    gsel = [jnp.zeros((1, tr), jnp.bool_) for _ in range(N_GROUPS)]
    for _ in range(TOPK_GROUPS):
        gm = functools.reduce(jnp.maximum, grp)
        found = jnp.zeros((1, tr), jnp.bool_)
        for g in range(N_GROUPS):
            hit = (grp[g] == gm) & jnp.logical_not(found)
            found = found | hit
            gsel[g] = gsel[g] | hit
            grp[g] = jnp.where(hit, neg_inf, grp[g])
    vals = [jnp.where(gsel[g], slabs[g], neg_inf) for g in range(N_GROUPS)]
    eidx = [sub + g * per for g in range(N_GROUPS)]
    sel = [jnp.zeros((per, tr), jnp.bool_) for _ in range(N_GROUPS)]
    row8 = lax.broadcasted_iota(I32, (TOP_K, tr), 0)
    idx_out = jnp.zeros((TOP_K, tr), I32)
    w_out = jnp.zeros((TOP_K, tr), F32)
    wsum = jnp.zeros((1, tr), F32)
    for k in range(TOP_K):
        vm = jnp.max(functools.reduce(jnp.maximum, vals), axis=0, keepdims=True)
        cand = [jnp.where(vals[g] == vm, eidx[g], N_EXPERTS) for g in range(N_GROUPS)]
        ik = jnp.min(functools.reduce(jnp.minimum, cand), axis=0, keepdims=True)
        wk = jnp.zeros((1, tr), F32)
        for g in range(N_GROUPS):
            hit = eidx[g] == ik
            sel[g] = sel[g] | hit
            vals[g] = jnp.where(hit, neg_inf, vals[g])
            sc = scores[g * per:(g + 1) * per, :]
            wk = wk + jnp.sum(jnp.where(hit, sc, 0.0), axis=0, keepdims=True)
        idx_out = jnp.where(row8 == k, ik, idx_out)
        w_out = jnp.where(row8 == k, wk, w_out)
        wsum = wsum + wk
    for g in range(N_GROUPS):
        sel_ref[g * per:(g + 1) * per, :] = sel[g].astype(I32)
    idx_ref[...] = idx_out
    wk_ref[...] = w_out / wsum * ROUTED_SCALE


def _router(h2, w_rt, b_r, m_rows):
    d = h2.shape[1]
    tr = 256
    return pl.pallas_call(
        _router_kernel,
        grid=(m_rows // tr,),
        in_specs=[pl.BlockSpec((tr, d), lambda i: (i, 0)),
                  pl.BlockSpec((N_EXPERTS, d), lambda i: (0, 0)),
                  pl.BlockSpec((N_EXPERTS, 1), lambda i: (0, 0))],
        out_specs=[pl.BlockSpec((N_EXPERTS, tr), lambda i: (0, i)),
                   pl.BlockSpec((TOP_K, tr), lambda i: (0, i)),
                   pl.BlockSpec((TOP_K, tr), lambda i: (0, i))],
        out_shape=[jax.ShapeDtypeStruct((N_EXPERTS, m_rows), I32),
                   jax.ShapeDtypeStruct((TOP_K, m_rows), I32),
                   jax.ShapeDtypeStruct((TOP_K, m_rows), F32)],
        compiler_params=_params(("arbitrary",)),
        name="router",
    )(h2, w_rt, b_r.astype(F32).reshape(N_EXPERTS, 1))


def _row_gather(tok_ref, n_rows, src_hbm, dst_buf, sem, start):
    def body(j, carry):
        for u in range(SUBLANES):
            r = j * SUBLANES + u
            t = tok_ref[0, 0, r] if start else 0
            cp = pltpu.make_async_copy(src_hbm.at[pl.ds(t, 1), :], dst_buf.at[pl.ds(r, 1), :], sem)
            if start:
                cp.start()
            else:
                cp.wait()
        return carry
    lax.fori_loop(0, n_rows // SUBLANES, body, 0)


def _expert_kernel(nused_ref, blke_ref, tok_ref, tokn_ref, hp_ref, wg_ref, wu_ref, wd_ref, sw_ref, y_ref,
                   xbuf, sem, wgu_c, wd_c):
    i = pl.program_id(0)
    nu = nused_ref[0]
    tm = xbuf.shape[1]
    half = xbuf.shape[2]
    slot = i % 2

    @pl.when(i == 0)
    def _():
        _row_gather(tok_ref, tm, hp_ref, xbuf.at[0], sem.at[0], True)

    @pl.when(i + 1 < nu)
    def _():
        _row_gather(tokn_ref, tm, hp_ref, xbuf.at[1 - slot], sem.at[1 - slot], True)

    @pl.when((i < nu) & ((i == 0) | (blke_ref[i] != blke_ref[jnp.maximum(i - 1, 0)])))
    def _():
        wgu_c[:, :EXPERT_DIM] = wg_ref[0, 0].astype(wgu_c.dtype)
        wgu_c[:, EXPERT_DIM:] = wu_ref[0, 0].astype(wgu_c.dtype)
        wd_c[...] = wd_ref[0, 0].astype(wd_c.dtype)

    @pl.when(i < nu)
    def _():
        _row_gather(tok_ref, tm, hp_ref, xbuf.at[slot], sem.at[slot], False)
        lo, hi = _unpack_halves(xbuf[slot])
        gu = (_dot(lo.astype(MXU_DTYPE), wgu_c[:half, :])
              + _dot(hi.astype(MXU_DTYPE), wgu_c[half:, :]))
        g = gu[:, :EXPERT_DIM]
        u = gu[:, EXPERT_DIM:]
        a = (g * _sigmoid(g) * u).astype(MXU_DTYPE)
        y = _dot(a, wd_c[...]) * sw_ref[...]
        y_ref[...] = _pack_halves(y)

    @pl.when(i >= nu)
    def _():
        y_ref[...] = jnp.zeros_like(y_ref)


def _experts(hp, wg, wu, wd, layer, n_used, blk_e, slot_tok, slot_w, n_blk):
    d2 = hp.shape[1]
    d = 2 * d2
    tm = MOE_TILE
    tok3 = slot_tok.reshape(n_blk, 1, tm)
    last = n_blk - 1
    grid_spec = pltpu.PrefetchScalarGridSpec(
        num_scalar_prefetch=2,
        grid=(n_blk,),
        in_specs=[pl.BlockSpec((1, 1, tm), lambda i, nu, be: (i, 0, 0), memory_space=pltpu.SMEM),
                  pl.BlockSpec((1, 1, tm), lambda i, nu, be: (jnp.minimum(i + 1, last), 0, 0),
                               memory_space=pltpu.SMEM),
                  pl.BlockSpec(memory_space=pl.ANY),
                  pl.BlockSpec((1, 1, d, EXPERT_DIM), lambda i, nu, be: (layer, be[i], 0, 0)),
                  pl.BlockSpec((1, 1, d, EXPERT_DIM), lambda i, nu, be: (layer, be[i], 0, 0)),
                  pl.BlockSpec((1, 1, EXPERT_DIM, d), lambda i, nu, be: (layer, be[i], 0, 0)),
                  pl.BlockSpec((tm, 1), lambda i, nu, be: (i, 0))],
        out_specs=pl.BlockSpec((tm, d2), lambda i, nu, be: (i, 0)),
        scratch_shapes=[pltpu.VMEM((2, tm, d2), U32), pltpu.SemaphoreType.DMA((2,)),
                        pltpu.VMEM((d, 2 * EXPERT_DIM), MXU_DTYPE), pltpu.VMEM((EXPERT_DIM, d), MXU_DTYPE)],
    )
    return pl.pallas_call(
        _expert_kernel,
        grid_spec=grid_spec,
        out_shape=jax.ShapeDtypeStruct((n_blk * tm, d2), U32),
        compiler_params=_params(("arbitrary",), vmem_mb=56),
        name="routed_experts",
    )(n_used, blk_e, tok3, tok3, hp, wg, wu, wd, slot_w.reshape(n_blk * tm, 1))


def _shared_kernel(h_ref, wgu_ref, wd_ref, o_ref):
    gu = _dot(h_ref[...], wgu_ref[...])
    hd = gu.shape[1] // 2
    g = gu[:, :hd]
    u = gu[:, hd:]
    a = (g * _sigmoid(g) * u).astype(MXU_DTYPE)
    o_ref[...] = _dot(a, wd_ref[...]).astype(o_ref.dtype)


def _shared_expert(h2, wgu, wd, m_rows):
    d = h2.shape[1]
    tm = 256
    return pl.pallas_call(
        _shared_kernel,
        grid=(m_rows // tm,),
        in_specs=[pl.BlockSpec((tm, d), lambda i: (i, 0)),
                  pl.BlockSpec(wgu.shape, lambda i: (0, 0)),
                  pl.BlockSpec(wd.shape, lambda i: (0, 0))],
        out_specs=pl.BlockSpec((tm, d), lambda i: (i, 0)),
        out_shape=jax.ShapeDtypeStruct((m_rows, d), MXU_DTYPE),
        compiler_params=_params(("arbitrary",)),
        name="shared_expert",
    )(h2, wgu, wd)


def _combine_kernel(n_lat, sl_ref, sln_ref, y_hbm, x_ref, sh_ref, g_ref, o_ref, buf, sem):
    i = pl.program_id(0)
    nsteps = pl.num_programs(0)
    tc = x_ref.shape[0]
    nrow = buf.shape[1]
    slot = i % 2

    @pl.when(i == 0)
    def _():
        _row_gather(sl_ref, nrow, y_hbm, buf.at[0], sem.at[0], True)

    @pl.when(i + 1 < nsteps)
    def _():
        _row_gather(sln_ref, nrow, y_hbm, buf.at[1 - slot], sem.at[1 - slot], True)

    _row_gather(sl_ref, nrow, y_hbm, buf.at[slot], sem.at[slot], False)
    lo_acc = None
    hi_acc = None
    for k in range(TOP_K):
        lo, hi = _unpack_halves(buf[slot, k * tc:(k + 1) * tc, :])
        lo_acc = lo if lo_acc is None else lo_acc + lo
        hi_acc = hi if hi_acc is None else hi_acc + hi
    routed = jnp.concatenate([lo_acc, hi_acc], axis=1)
    rows = i * tc + lax.broadcasted_iota(I32, (tc, 1), 0)
    g = jnp.where(rows >= n_lat, g_ref[1:2, :], g_ref[0:1, :])
    o_ref[...] = x_ref[...] + g * (routed + sh_ref[...].astype(F32))


def _combine(y_slots, slots, x1, sh, mod, gate_blk, n_lat, m_rows):
    d = x1.shape[1]
    tc = COMBINE_TILE
    nsteps = m_rows // tc
    last = nsteps - 1
    return pl.pallas_call(
        functools.partial(_combine_kernel, n_lat),
        grid=(nsteps,),
        in_specs=[pl.BlockSpec((1, 1, TOP_K * tc), lambda i: (i, 0, 0), memory_space=pltpu.SMEM),
                  pl.BlockSpec((1, 1, TOP_K * tc), lambda i: (jnp.minimum(i + 1, last), 0, 0),
                               memory_space=pltpu.SMEM),
                  pl.BlockSpec(memory_space=pl.ANY),
                  pl.BlockSpec((tc, d), lambda i: (i, 0)),
                  pl.BlockSpec((tc, d), lambda i: (i, 0)),
                  pl.BlockSpec((2, d), lambda i: (0, gate_blk))],
        out_specs=pl.BlockSpec((tc, d), lambda i: (i, 0)),
        out_shape=jax.ShapeDtypeStruct((m_rows, d), F32),
        scratch_shapes=[pltpu.VMEM((2, TOP_K * tc, d // 2), U32), pltpu.SemaphoreType.DMA((2,))],
        compiler_params=_params(("arbitrary",)),
        name="moe_combine",
    )(slots, slots, y_slots, x1, sh, mod)


def _dispatch_tables(sel, idx, wk, m_rows):
    tm = MOE_TILE
    tk = m_rows * TOP_K
    n_blk = (tk + tm - 1) // tm + N_EXPERTS
    cs = jnp.cumsum(sel, axis=1)
    counts = cs[:, -1]
    padded = ((counts + tm - 1) // tm) * tm
    pend = jnp.cumsum(padded)
    pstart = pend - padded
    dest_et = pstart[:, None] + cs - 1
    dest = jnp.take_along_axis(dest_et, idx, axis=0)
    blk_start = jnp.arange(n_blk, dtype=I32) * tm
    blk_e = jnp.minimum(jnp.sum((pend[None, :] <= blk_start[:, None]).astype(I32), axis=1), N_EXPERTS - 1)
    n_fill = n_blk * tm - tk - N_EXPERTS * tm
    f = jnp.arange(tk, dtype=I32)
    real_key = idx.T.reshape(-1) * (2 * tk) + f
    e_d = jnp.arange(N_EXPERTS, dtype=I32)[:, None]
    j_d = jnp.arange(tm, dtype=I32)[None, :]
    big = N_EXPERTS * 2 * tk
    pad_key = jnp.where(j_d < (padded - counts)[:, None], e_d * (2 * tk) + tk + j_d, big + e_d * tm + j_d)
    keys = jnp.concatenate([real_key, pad_key.reshape(-1), jnp.full((n_fill,), big + N_EXPERTS * tm, I32)])
    n_pad = keys.shape[0] - tk
    toks = jnp.concatenate([f // TOP_K, jnp.zeros((n_pad,), I32)])
    ws = jnp.concatenate([wk.T.reshape(-1), jnp.zeros((n_pad,), F32)])
    _, slot_tok, slot_w = lax.sort((keys, toks, ws), num_keys=1)
    n_used = (pend[-1] // tm).astype(I32).reshape(1)
    tc = COMBINE_TILE
    slots = dest.reshape(TOP_K, m_rows // tc, tc).transpose(1, 0, 2).reshape(m_rows // tc, 1, TOP_K * tc)
    return n_used, blk_e, slot_tok, slot_w, slots.astype(I32), n_blk


def kernel(x, c, ctx, c_ctx, w_mod, b_mod, g_norm1, g_norm2, w_in, conv_w, diff_qnorm, diff_knorm, diff_lambda, diff_subln, swa_qnorm, swa_knorm, swa_sink, na_qnorm, na_knorm, na_rpb, w_gate, b_gate, w_branch, w_out, w_router, b_router, w_exp_gate, w_exp_up, w_exp_down, w_sh_gate, w_sh_up, w_sh_down):
    bsz, n, d = x.shape
    assert bsz == 1
    n_ctx = ctx.shape[1]
    t_all = n + n_ctx
    depth = w_in.shape[0]
    bw = d // 4
    cdt = MXU_DTYPE

    xa = jnp.concatenate([x[0], ctx[0]], axis=0)
    cc = jnp.stack([c[0], c_ctx], axis=0).astype(F32)
    rope_d = _rope_tables(n, t_all, DIFF_QK // 2)
    rope_s = _rope_tables(n, t_all, HEAD_DIM // 2)

    o_conv = 0
    o_dq, o_dk, o_dv = 3 * bw, 4 * bw, 5 * bw
    o_sq = 6 * bw
    o_sk = o_sq + bw
    o_sv = o_sk + SWA_KV_HEADS * HEAD_DIM
    o_nq = o_sv + SWA_KV_HEADS * HEAD_DIM
    o_nk, o_nv = o_nq + bw, o_nq + 2 * bw

    for l in range(depth):
        update_ctx = l < depth - 1
        nq = t_all if update_ctx else n
        lam_init = 0.8 - 0.6 * math.exp(-0.3 * l)
        mod = _modvec(cc, w_mod, b_mod, l)

        h = _rmsmod(xa, g_norm1[l], mod, 0, 1, n, t_all)
        p = _matmul(h, w_in[l].astype(cdt), cdt, t_all)

        a = _short_conv(p, conv_w[l].astype(F32), n, nq)
        q1z, q2z = _prep(p, o_dq, N_HEADS, diff_qnorm[l], DIFF_QK, rope_d, split=True,
                         scale=DIFF_QK ** -0.5 * math.log2(math.e))
        kd = _prep(p, o_dk, N_HEADS, diff_knorm[l], DIFF_QK, rope_d)
        qs = _prep(p, o_sq, N_HEADS, swa_qnorm[l], HEAD_DIM, rope_s)
        ks = _prep(p, o_sk, SWA_KV_HEADS, swa_knorm[l], HEAD_DIM, rope_s)
        qn = _prep(p, o_nq, N_HEADS, na_qnorm[l], HEAD_DIM, None)
        kn = _prep(p, o_nk, N_HEADS, na_knorm[l], HEAD_DIM, None)

        v_d = p[:, o_dv:o_dv + bw].reshape(t_all, N_HEADS, HEAD_DIM)
        vx = jnp.concatenate([v_d, jnp.ones_like(v_d)], axis=-1).reshape(t_all, 2 * bw)
        yb = _diff_attention(q1z, q2z, kd, vx, diff_lambda[l], diff_subln[l], lam_init, n, True)
        if update_ctx:
            yb_c = _diff_attention(q1z, q2z, kd, vx, diff_lambda[l], diff_subln[l], lam_init, n, False)
            yb = jnp.concatenate([yb, yb_c], axis=0)
        ys = _swa_attention(qs, ks, p, o_sv, swa_sink[l], n, nq)
        yn = _na_attention(qn, kn, p, o_nv, _na_bias_table(na_rpb[l]), n, nq)

        acc = _merge(h, (a, yb, ys, yn), w_gate[l].astype(cdt), b_gate[l], w_branch[l].astype(cdt), nq)
        x1 = _matmul_residual(acc, w_out[l].astype(cdt), xa, mod, 2, n, nq)

        h2, h2p = _rmsmod(x1, g_norm2[l], mod, 3, 4, n, nq, packed=True)
        sel, idx, wk = _router(h2, w_router[l].T.astype(cdt), b_router[l], nq)
        n_used, blk_e, slot_tok, slot_w, slots, n_blk = _dispatch_tables(sel, idx, wk, nq)
        y_slots = _experts(h2p, w_exp_gate, w_exp_up, w_exp_down, l, n_used, blk_e, slot_tok, slot_w, n_blk)
        wsh = jnp.concatenate([w_sh_gate[l], w_sh_up[l]], axis=-1).astype(cdt)
        sh = _shared_expert(h2, wsh, w_sh_down[l].astype(cdt), nq)
        xa = _combine(y_slots, slots, x1, sh, mod, 5, n, nq)

    return xa[:n].reshape(bsz, n, d)
```

```python
import functools
import math

import jax
import jax.numpy as jnp
from jax import lax
from jax.experimental import pallas as pl
from jax.experimental.pallas import tpu as pltpu

GRID_W = 64
HEAD_DIM = 128
DIFF_QK = 64
SWA_WINDOW = 128
SWA_BLOCK = 128
SWA_KV_HEADS = 2
N_HEADS = 8
NA_ROWS = 8
NA_COLS = 16
ROPE_BASE = 10000.0
N_EXPERTS = 64
N_GROUPS = 8
TOPK_GROUPS = 4
TOP_K = 8
EXPERT_DIM = 256
ROUTED_SCALE = 2.5
EPS = 1e-6
NEG = -1e30

LANES = 128
SUBLANES = 8
V7X_VMEM_BYTES = 64 * 1024 * 1024

MXU_DTYPE = jnp.bfloat16
MOE_TILE = 256
COMBINE_TILE = 64

F32 = jnp.float32
I32 = jnp.int32
U32 = jnp.uint32


def _params(sem, vmem_mb=48):
    return pltpu.CompilerParams(dimension_semantics=sem, vmem_limit_bytes=vmem_mb * 1024 * 1024)


def _pick(m, cands):
    for c in cands:
        if m % c == 0:
            return c
    raise ValueError(f"no tile for {m} in {cands}")


def _nt_dot(a, b):
    return lax.dot_general(a, b, (((1,), (1,)), ((), ())), preferred_element_type=F32)


def _dot(a, b):
    return jnp.dot(a, b, preferred_element_type=F32)


def _sigmoid(x):
    return 1.0 / (1.0 + jnp.exp(-x))


def _modvec_kernel(s_ref, w_ref, b_ref, o_ref):
    _, d, tn = w_ref.shape
    rows = []
    for r in range(2):
        c = s_ref[r]
        s = c * _sigmoid(c)
        parts = []
        for j in range(tn // LANES):
            prod = w_ref[0, :, j * LANES:(j + 1) * LANES] * s
            part = jnp.sum(prod.reshape(d // SUBLANES, SUBLANES, LANES), axis=0)
            parts.append(jnp.sum(part, axis=0, keepdims=True))
        rows.append(jnp.concatenate(parts, axis=1))
    is_first = lax.broadcasted_iota(I32, (2, tn), 0) == 0
    o_ref[...] = jnp.where(is_first, rows[0], rows[1]) + b_ref[0]


def _modvec(cc, w, b, layer):
    depth, d, n = w.shape
    tn = 512
    s_rep = jnp.broadcast_to(cc[:, :, None], (2, d, LANES))
    return pl.pallas_call(
        _modvec_kernel,
        grid=(n // tn,),
        in_specs=[pl.BlockSpec((2, d, LANES), lambda j: (0, 0, 0)),
                  pl.BlockSpec((1, d, tn), lambda j: (layer, 0, j)),
                  pl.BlockSpec((1, 1, tn), lambda j: (layer, 0, j))],
        out_specs=pl.BlockSpec((2, tn), lambda j: (0, j)),
        out_shape=jax.ShapeDtypeStruct((2, n), F32),
        compiler_params=_params(("arbitrary",)),
        name="modvec",
    )(s_rep, w, b.reshape(depth, 1, n))


def _pack_halves(v):
    c = v.shape[1] // 2
    bits = lax.bitcast_convert_type(v.astype(jnp.bfloat16).astype(F32), U32)
    return (bits[:, c:] & jnp.uint32(0xFFFF0000)) | (bits[:, :c] >> 16)


def _unpack_halves(w):
    lo = lax.bitcast_convert_type(w << 16, F32)
    hi = lax.bitcast_convert_type(w & jnp.uint32(0xFFFF0000), F32)
    return lo, hi


def _rmsmod_kernel(n_lat, packed, x_ref, g_ref, sh_ref, sc_ref, o_ref, *p_ref):
    tr = x_ref.shape[0]
    x = x_ref[...]
    ms = jnp.mean(x * x, axis=-1, keepdims=True)
    y = x * lax.rsqrt(ms + EPS) * g_ref[...]
    rows = pl.program_id(0) * tr + lax.broadcasted_iota(I32, (tr, 1), 0)
    is_ctx = rows >= n_lat
    sh = jnp.where(is_ctx, sh_ref[1:2, :], sh_ref[0:1, :])
    sc = jnp.where(is_ctx, sc_ref[1:2, :], sc_ref[0:1, :])
    h = y * (1.0 + sc) + sh
    o_ref[...] = h.astype(o_ref.dtype)
    if packed:
        p_ref[0][...] = _pack_halves(h)


def _rmsmod(x, g, mod, shift_blk, scale_blk, n_lat, m_rows, packed=False):
    d = x.shape[1]
    tr = 256
    out_shape = [jax.ShapeDtypeStruct((m_rows, d), MXU_DTYPE)]
    out_specs = [pl.BlockSpec((tr, d), lambda i: (i, 0))]
    if packed:
        out_shape.append(jax.ShapeDtypeStruct((m_rows, d // 2), U32))
        out_specs.append(pl.BlockSpec((tr, d // 2), lambda i: (i, 0)))
    res = pl.pallas_call(
        functools.partial(_rmsmod_kernel, n_lat, packed),
        grid=(m_rows // tr,),
        in_specs=[pl.BlockSpec((tr, d), lambda i: (i, 0)),
                  pl.BlockSpec((1, d), lambda i: (0, 0)),
                  pl.BlockSpec((2, d), lambda i: (0, shift_blk)),
                  pl.BlockSpec((2, d), lambda i: (0, scale_blk))],
        out_specs=out_specs,
        out_shape=out_shape,
        compiler_params=_params(("arbitrary",)),
        name="rmsmod",
    )(x, g.reshape(1, d), mod, mod)
    return res if packed else res[0]


def _mm_kernel(a_ref, w_ref, o_ref):
    o_ref[...] = _dot(a_ref[...], w_ref[...]).astype(o_ref.dtype)


def _mm_res_kernel(n_lat, a_ref, w_ref, r_ref, g_ref, o_ref):
    tm = a_ref.shape[0]
    acc = _dot(a_ref[...], w_ref[...])
    rows = pl.program_id(0) * tm + lax.broadcasted_iota(I32, (tm, 1), 0)
    g = jnp.where(rows >= n_lat, g_ref[1:2, :], g_ref[0:1, :])
    o_ref[...] = r_ref[...] + g * acc


def _matmul(a, w, out_dtype, m_rows):
    k = a.shape[1]
    n = w.shape[1]
    tm = _pick(m_rows, (1280, 1024, 640, 512, 256))
    tn = 512
    return pl.pallas_call(
        _mm_kernel,
        grid=(m_rows // tm, n // tn),
        in_specs=[pl.BlockSpec((tm, k), lambda i, j: (i, 0)),
                  pl.BlockSpec((k, tn), lambda i, j: (0, j))],
        out_specs=pl.BlockSpec((tm, tn), lambda i, j: (i, j)),
        out_shape=jax.ShapeDtypeStruct((m_rows, n), out_dtype),
        compiler_params=_params(("arbitrary", "arbitrary")),
        name="matmul",
    )(a, w)


def _matmul_residual(a, w, res, mod, gate_blk, n_lat, m_rows):
    k = a.shape[1]
    n = w.shape[1]
    tm = _pick(m_rows, (1280, 1024, 640, 512, 256))
    tn = 512
    nb = n // tn
    return pl.pallas_call(
        functools.partial(_mm_res_kernel, n_lat),
        grid=(m_rows // tm, nb),
        in_specs=[pl.BlockSpec((tm, k), lambda i, j: (i, 0)),
                  pl.BlockSpec((k, tn), lambda i, j: (0, j)),
                  pl.BlockSpec((tm, tn), lambda i, j: (i, j)),
                  pl.BlockSpec((2, tn), lambda i, j: (0, gate_blk * nb + j))],
        out_specs=pl.BlockSpec((tm, tn), lambda i, j: (i, j)),
        out_shape=jax.ShapeDtypeStruct((m_rows, n), F32),
        compiler_params=_params(("arbitrary", "arbitrary")),
        name="matmul_residual",
    )(a, w, res, mod)


def _conv_kernel(n_lat, t_all, b_ref, c_ref, x_ref, cp_ref, xp_ref, cn_ref, xn_ref, w_ref, o_ref):
    tr = b_ref.shape[0]
    u = c_ref[...].astype(F32) * x_ref[...].astype(F32)
    u_prev_row = cp_ref[SUBLANES - 1:SUBLANES, :].astype(F32) * xp_ref[SUBLANES - 1:SUBLANES, :].astype(F32)
    u_next_row = cn_ref[0:1, :].astype(F32) * xn_ref[0:1, :].astype(F32)
    r = lax.broadcasted_iota(I32, (tr, 1), 0)
    tok = pl.program_id(0) * tr + r
    up = jnp.where(r == 0, u_prev_row, pltpu.roll(u, 1, 0))
    un = jnp.where(r == tr - 1, u_next_row, pltpu.roll(u, tr - 1, 0))
    up = jnp.where((tok == 0) | (tok == n_lat), 0.0, up)
    un = jnp.where((tok == n_lat - 1) | (tok == t_all - 1), 0.0, un)
    y = w_ref[0:1, :] * up + w_ref[1:2, :] * u + w_ref[2:3, :] * un
    o_ref[...] = (b_ref[...].astype(F32) * y).astype(o_ref.dtype)


def _short_conv(p, conv_w, n_lat, m_rows):
    t_all = p.shape[0]
    cw = conv_w.shape[1]
    tr = 256
    tc = 512
    nc = cw // tc
    rb = tr // SUBLANES
    last = t_all // SUBLANES - 1
    return pl.pallas_call(
        functools.partial(_conv_kernel, n_lat, t_all),
        grid=(m_rows // tr, nc),
        in_specs=[pl.BlockSpec((tr, tc), lambda i, j: (i, j)),
                  pl.BlockSpec((tr, tc), lambda i, j: (i, nc + j)),
                  pl.BlockSpec((tr, tc), lambda i, j: (i, 2 * nc + j)),
                  pl.BlockSpec((SUBLANES, tc), lambda i, j: (jnp.maximum(i * rb - 1, 0), nc + j)),
                  pl.BlockSpec((SUBLANES, tc), lambda i, j: (jnp.maximum(i * rb - 1, 0), 2 * nc + j)),
                  pl.BlockSpec((SUBLANES, tc), lambda i, j: (jnp.minimum((i + 1) * rb, last), nc + j)),
                  pl.BlockSpec((SUBLANES, tc), lambda i, j: (jnp.minimum((i + 1) * rb, last), 2 * nc + j)),
                  pl.BlockSpec((3, tc), lambda i, j: (0, j))],
        out_specs=pl.BlockSpec((tr, tc), lambda i, j: (i, j)),
        out_shape=jax.ShapeDtypeStruct((m_rows, cw), MXU_DTYPE),
        compiler_params=_params(("arbitrary", "arbitrary")),
        name="short_conv",
    )(p, p, p, p, p, p, p, conv_w)


def _prep_kernel(n_heads, group, half, split, scale, x_ref, g_ref, *rest):
    if half:
        cos_ref, sa_ref, sb_ref = rest[:3]
        outs = rest[3:]
    else:
        outs = rest
    lane = lax.broadcasted_iota(I32, (1, LANES), 1)
    lo = lane < DIFF_QK
    g = g_ref[...]
    for h in range(n_heads):
        x = x_ref[:, h * LANES:(h + 1) * LANES].astype(F32)
        sq = x * x
        if group == LANES:
            ms = jnp.mean(sq, axis=-1, keepdims=True)
        else:
            s_lo = jnp.sum(jnp.where(lo, sq, 0.0), axis=-1, keepdims=True)
            s_hi = jnp.sum(jnp.where(lo, 0.0, sq), axis=-1, keepdims=True)
            ms = jnp.where(lo, s_lo, s_hi) * (1.0 / group)
        y = x * lax.rsqrt(ms + EPS) * g
        if half:
            y = (y * cos_ref[...] + pltpu.roll(y, LANES - half, 1) * sa_ref[...]
                 + pltpu.roll(y, half, 1) * sb_ref[...])
        if scale != 1.0:
            y = y * scale
        if split:
            outs[0][:, h * LANES:(h + 1) * LANES] = jnp.where(lo, y, 0.0).astype(outs[0].dtype)
            outs[1][:, h * LANES:(h + 1) * LANES] = jnp.where(lo, 0.0, y).astype(outs[1].dtype)
        else:
            outs[0][:, h * LANES:(h + 1) * LANES] = y.astype(outs[0].dtype)


def _prep(p, col0, n_heads, gain, group, rope, split=False, scale=1.0):
    t_all = p.shape[0]
    tr = _pick(t_all, (640, 256))
    hpb = _pick(math.gcd(col0 // LANES, n_heads), (8, 4, 2, 1))
    w = hpb * LANES
    cb = col0 // w
    g = jnp.tile(gain.astype(F32), LANES // gain.shape[0]).reshape(1, LANES)
    in_specs = [pl.BlockSpec((tr, w), lambda i, j: (i, cb + j)),
                pl.BlockSpec((1, LANES), lambda i, j: (0, 0))]
    args = [p, g]
    half = 0
    if rope is not None:
        half = rope[3]
        in_specs += [pl.BlockSpec((tr, LANES), lambda i, j: (i, 0))] * 3
        args += list(rope[:3])
    n_out = 2 if split else 1
    res = pl.pallas_call(
        functools.partial(_prep_kernel, hpb, group, half, split, scale),
        grid=(t_all // tr, n_heads // hpb),
        in_specs=in_specs,
        out_specs=[pl.BlockSpec((tr, w), lambda i, j: (i, j))] * n_out,
        out_shape=[jax.ShapeDtypeStruct((t_all, n_heads * LANES), MXU_DTYPE)] * n_out,
        compiler_params=_params(("arbitrary", "arbitrary")),
        name="qk_prep",
    )(*args)
    return res if split else res[0]


def _rope_tables(n_lat, t_all, seg):
    half = seg // 2
    n_rows = n_lat // GRID_W
    lane = jnp.arange(LANES)
    freqs = ROPE_BASE ** (-jnp.arange(0, seg, 2, dtype=F32) / seg)
    f_lane = freqs[lane % half]
    use_col = ((lane // seg) % 2 == 1)[None, None, :]
    ang_r = jnp.arange(n_rows, dtype=F32)[:, None] * f_lane[None, :]
    ang_c = jnp.arange(GRID_W, dtype=F32)[:, None] * f_lane[None, :]

    def expand(fn):
        full = jnp.where(use_col, fn(ang_c)[None, :, :], fn(ang_r)[:, None, :])
        return full.reshape(n_lat, LANES)

    cos = expand(jnp.cos)
    sin = expand(jnp.sin)
    first = (lane % seg) < half
    sa = jnp.where(first[None, :], -sin, 0.0)
    sb = jnp.where(first[None, :], 0.0, sin)
    pad = t_all - n_lat
    cos = jnp.concatenate([cos, jnp.ones((pad, LANES), F32)], axis=0)
    sa = jnp.concatenate([sa, jnp.zeros((pad, LANES), F32)], axis=0)
    sb = jnp.concatenate([sb, jnp.zeros((pad, LANES), F32)], axis=0)
    return cos, sa, sb, half


DIFF_TQ = 512
DIFF_TK = 1024
DIFF_GROUP = 4


def _diff_kernel(n_pairs, tkc, lam_init, q1_ref, q2_ref, k_ref, vx_ref, dl_ref, gs_ref, o_ref, s_scr, mx_scr):
    tq = q1_ref.shape[0]
    t_all = k_ref.shape[0]
    n_lat = 2 * n_pairs * tkc
    q1 = q1_ref[...]
    q2 = q2_ref[...]

    def scores(kc):
        return _nt_dot(q1, kc), _nt_dot(q2, kc)

    def lane_max(s):
        return functools.reduce(jnp.maximum, [s[:, i * LANES:(i + 1) * LANES] for i in range(s.shape[1] // LANES)])

    def fold(s, smax, vc, m, a):
        mn = jnp.maximum(m, jnp.max(smax, axis=-1, keepdims=True))
        alpha = jnp.exp2(m - mn)
        p = jnp.exp2(s - mn).astype(vc.dtype)
        return mn, alpha * a + _dot(p, vc)

    def lat_scores_to(buf, c):
        off = pl.multiple_of(c * tkc, tkc)
        s1, s2 = scores(k_ref[pl.ds(off, tkc), :])
        s_scr[buf, 0] = s1
        s_scr[buf, 1] = s2
        mx_scr[buf, 0] = lane_max(s1)
        mx_scr[buf, 1] = lane_max(s2)

    def lat_fold_from(buf, c, carry):
        m1, a1, m2, a2 = carry
        off = pl.multiple_of(c * tkc, tkc)
        vc = vx_ref[pl.ds(off, tkc), :]
        m1, a1 = fold(s_scr[buf, 0], mx_scr[buf, 0], vc, m1, a1)
        m2, a2 = fold(s_scr[buf, 1], mx_scr[buf, 1], vc, m2, a2)
        return m1, a1, m2, a2

    minit = jnp.full((tq, 1), -jnp.inf, F32)
    ainit = jnp.zeros((tq, 2 * HEAD_DIM), F32)
    carry = (minit, ainit, minit, ainit)
    n_chunks = 2 * n_pairs
    group = _pick(n_chunks, (DIFF_GROUP, 4, 2)) if n_pairs > 0 else 0
    if n_pairs > 0:
        lat_scores_to(0, 0)

        def run_group(c0, carry, last):
            for u in range(group):
                if not (last and u == group - 1):
                    lat_scores_to((u + 1) % 2, c0 + u + 1)
                    carry = lat_fold_from(u % 2, c0 + u, carry)
            return carry

        carry = lax.fori_loop(0, n_chunks // group - 1,
                              lambda j, cr: run_group(j * group, cr, False), carry)
        carry = run_group(n_chunks - group, carry, True)
    s1c, s2c = scores(k_ref[n_lat:t_all, :])
    if n_pairs > 0:
        carry = lat_fold_from(1, n_chunks - 1, carry)
    m1, a1, m2, a2 = carry
    vcx = vx_ref[n_lat:t_all, :]
    m1, a1 = fold(s1c, lane_max(s1c), vcx, m1, a1)
    m2, a2 = fold(s2c, lane_max(s2c), vcx, m2, a2)

    dl = dl_ref[...]
    lam = (jnp.exp(jnp.sum(dl[0:1, :] * dl[1:2, :], axis=-1, keepdims=True))
           - jnp.exp(jnp.sum(dl[2:3, :] * dl[3:4, :], axis=-1, keepdims=True)) + lam_init)
    o = (a1[:, :HEAD_DIM] / a1[:, HEAD_DIM:HEAD_DIM + 1]
         - lam * (a2[:, :HEAD_DIM] / a2[:, HEAD_DIM:HEAD_DIM + 1]))
    ms = jnp.mean(o * o, axis=-1, keepdims=True)
    o = o * lax.rsqrt(ms + EPS) * gs_ref[...] * (1.0 - lam_init)
    o_ref[...] = o.astype(o_ref.dtype)


def _diff_attention(q1z, q2z, kd, vx, diff_lambda, subln, lam_init, n_lat, latent):
    t_all = kd.shape[0]
    n_ctx = t_all - n_lat
    if latent:
        tq, tkc, rows, blk0 = DIFF_TQ, DIFF_TK, n_lat, 0
        assert n_lat % (2 * tkc) == 0 and n_lat % tq == 0
        n_pairs = n_lat // (2 * tkc)
    else:
        tq, tkc, rows, blk0, n_pairs = n_ctx, DIFF_TK, n_ctx, n_lat // n_ctx, 0
        assert n_lat % n_ctx == 0
    if n_pairs == 0:
        k_spec = pl.BlockSpec((n_ctx, LANES), lambda h, i: (n_lat // n_ctx, h))
        v_spec = pl.BlockSpec((n_ctx, 2 * LANES), lambda h, i: (n_lat // n_ctx, h))
    else:
        k_spec = pl.BlockSpec((t_all, LANES), lambda h, i: (0, h))
        v_spec = pl.BlockSpec((t_all, 2 * LANES), lambda h, i: (0, h))
    return pl.pallas_call(
        functools.partial(_diff_kernel, n_pairs, tkc, lam_init),
        grid=(N_HEADS, rows // tq),
        in_specs=[pl.BlockSpec((tq, LANES), lambda h, i: (blk0 + i, h)),
                  pl.BlockSpec((tq, LANES), lambda h, i: (blk0 + i, h)),
                  k_spec, v_spec,
                  pl.BlockSpec((4, DIFF_QK), lambda h, i: (0, 0)),
                  pl.BlockSpec((1, HEAD_DIM), lambda h, i: (0, 0))],
        out_specs=pl.BlockSpec((tq, LANES), lambda h, i: (i, h)),
        out_shape=jax.ShapeDtypeStruct((rows, N_HEADS * HEAD_DIM), MXU_DTYPE),
        scratch_shapes=[pltpu.VMEM((2, 2, tq, tkc), F32), pltpu.VMEM((2, 2, tq, LANES), F32)],
        compiler_params=_params(("arbitrary", "arbitrary")),
        name="diff_attention" if latent else "diff_attention_ctx",
    )(q1z, q2z, kd, vx, diff_lambda.astype(F32), subln.astype(F32).reshape(1, HEAD_DIM))


def _swa_kernel(n_lat, t_all, q_ref, k_ref, v_ref, sink_ref, o_ref):
    w = SWA_BLOCK
    g = q_ref.shape[1] // HEAD_DIM
    b = pl.program_id(1)
    reach = jnp.where(b * w < n_lat, SWA_WINDOW, -1)
    scale = HEAD_DIM ** -0.5
    start = jnp.clip((b - 1) * w, 0, n_lat - 3 * w)
    start = pl.multiple_of(start, w)
    kw = k_ref[pl.ds(start, 3 * w), :]
    vw = v_ref[pl.ds(start, 3 * w), :]
    kc = k_ref[n_lat:t_all, :]
    vc = v_ref[n_lat:t_all, :]
    q = jnp.concatenate([q_ref[:, j * HEAD_DIM:(j + 1) * HEAD_DIM] for j in range(g)], axis=0)
    s_w = _nt_dot(q, kw) * scale
    s_c = _nt_dot(q, kc) * scale
    qpos = b * w + lax.broadcasted_iota(I32, (w, 3 * w), 0)
    kpos = start + lax.broadcasted_iota(I32, (w, 3 * w), 1)
    mask = jnp.abs(qpos - kpos) <= reach
    mask = jnp.concatenate([mask] * g, axis=0)
    s_w = jnp.where(mask, s_w, NEG)
    sink = jnp.concatenate([jnp.broadcast_to(sink_ref[0, :, j:j + 1], (w, 1)) for j in range(g)], axis=0)
    m = jnp.maximum(jnp.maximum(jnp.max(s_w, axis=-1, keepdims=True), jnp.max(s_c, axis=-1, keepdims=True)), sink)
    p_w = jnp.exp(s_w - m)
    p_c = jnp.exp(s_c - m)
    l = jnp.sum(p_w, axis=-1, keepdims=True) + jnp.sum(p_c, axis=-1, keepdims=True) + jnp.exp(sink - m)
    inv = 1.0 / l
    o = _dot((p_w * inv).astype(vw.dtype), vw) + _dot((p_c * inv).astype(vc.dtype), vc)
    for j in range(g):
        o_ref[:, j * HEAD_DIM:(j + 1) * HEAD_DIM] = o[j * w:(j + 1) * w, :].astype(o_ref.dtype)


def _swa_attention(qs, ks, p, v_col0, sink, n_lat, nq):
    t_all = ks.shape[0]
    g = N_HEADS // SWA_KV_HEADS
    vb = v_col0 // LANES
    sink3 = sink.astype(F32).reshape(SWA_KV_HEADS, 1, g)
    return pl.pallas_call(
        functools.partial(_swa_kernel, n_lat, t_all),
        grid=(SWA_KV_HEADS, nq // SWA_BLOCK),
        in_specs=[pl.BlockSpec((SWA_BLOCK, g * HEAD_DIM), lambda kv, b: (b, kv)),
                  pl.BlockSpec((t_all, LANES), lambda kv, b: (0, kv)),
                  pl.BlockSpec((t_all, LANES), lambda kv, b: (0, vb + kv)),
                  pl.BlockSpec((1, 1, g), lambda kv, b: (kv, 0, 0))],
        out_specs=pl.BlockSpec((SWA_BLOCK, g * HEAD_DIM), lambda kv, b: (b, kv)),
        out_shape=jax.ShapeDtypeStruct((nq, N_HEADS * HEAD_DIM), MXU_DTYPE),
        compiler_params=_params(("arbitrary", "arbitrary")),
        name="swa_attention",
    )(qs, ks, p, sink3)


NA_QROWS = 4
NA_WIN_ROWS = NA_QROWS + NA_ROWS
NA_VARIANTS = 4


def _na_kernel(n_lat, t_all, q_ref, k_ref, v_ref, bias_ref, o_ref):
    n_rows = n_lat // GRID_W
    n_grp = n_rows // NA_QROWS
    win = NA_WIN_ROWS * GRID_W
    scale = HEAD_DIM ** -0.5
    g = pl.program_id(1)
    wrow = jnp.clip(NA_QROWS * g - NA_ROWS // 2, 0, n_rows - NA_WIN_ROWS)
    variant = jnp.where(g >= n_grp, 3, jnp.where(g == 0, 0, jnp.where(g == n_grp - 1, 2, 1)))
    off = pl.multiple_of(wrow * GRID_W, GRID_W)
    kn = k_ref[pl.ds(off, win), :]
    vn = v_ref[pl.ds(off, win), :]
    kc = k_ref[n_lat:t_all, :]
    vc = v_ref[n_lat:t_all, :]
    q = q_ref[...]
    s_n = _nt_dot(q, kn) * scale + bias_ref[0, variant]
    s_c = _nt_dot(q, kc) * scale
    m = jnp.maximum(jnp.max(s_n, axis=-1, keepdims=True), jnp.max(s_c, axis=-1, keepdims=True))
    p_n = jnp.exp(s_n - m)
    p_c = jnp.exp(s_c - m)
    inv = 1.0 / (jnp.sum(p_n, axis=-1, keepdims=True) + jnp.sum(p_c, axis=-1, keepdims=True))
    o = _dot((p_n * inv).astype(vn.dtype), vn) + _dot((p_c * inv).astype(vc.dtype), vc)
    o_ref[...] = o.astype(o_ref.dtype)


def _na_bias_table(rpb):
    h = rpb.shape[0]
    c = jnp.arange(GRID_W)
    c0 = jnp.clip(c - NA_COLS // 2, 0, GRID_W - NA_COLS)
    kc = jnp.arange(GRID_W)
    inwin = (kc[None, :] >= c0[:, None]) & (kc[None, :] < c0[:, None] + NA_COLS)
    col_off = jnp.clip(kc[None, :] - c[:, None] + NA_COLS - 1, 0, 2 * NA_COLS - 2)
    toe = jnp.where(inwin[None, None], rpb.astype(F32)[:, :, col_off], NEG)
    neg_blk = jnp.full((h, GRID_W, GRID_W), NEG, F32)
    half = NA_ROWS // 2
    variants = []
    for shift, lo, hi in ((0, 0, NA_ROWS), (-half, None, None), (-NA_ROWS, half, NA_WIN_ROWS)):
        rows = []
        for j in range(NA_QROWS):
            blocks = []
            for a in range(NA_WIN_ROWS):
                dr = a - j + shift
                if lo is None:
                    valid = -half <= dr < half
                else:
                    valid = lo <= a < hi
                blocks.append(toe[:, dr + NA_ROWS - 1] if valid else neg_blk)
            rows.append(jnp.concatenate(blocks, axis=-1))
        variants.append(jnp.concatenate(rows, axis=1))
    variants.append(jnp.full_like(variants[0], NEG))
    return jnp.stack(variants, axis=1)


def _na_attention(qn, kn, p, v_col0, bias, n_lat, nq):
    t_all = kn.shape[0]
    vb = v_col0 // LANES
    tq = NA_QROWS * GRID_W
    assert n_lat // GRID_W >= NA_WIN_ROWS and (n_lat // GRID_W) % NA_QROWS == 0 and (t_all - n_lat) % tq == 0
    return pl.pallas_call(
        functools.partial(_na_kernel, n_lat, t_all),
        grid=(N_HEADS, nq // tq),
        in_specs=[pl.BlockSpec((tq, LANES), lambda h, i: (i, h)),
                  pl.BlockSpec((t_all, LANES), lambda h, i: (0, h)),
                  pl.BlockSpec((t_all, LANES), lambda h, i: (0, vb + h)),
                  pl.BlockSpec((1, NA_VARIANTS, tq, NA_WIN_ROWS * GRID_W), lambda h, i: (h, 0, 0, 0))],
        out_specs=pl.BlockSpec((tq, LANES), lambda h, i: (i, h)),
        out_shape=jax.ShapeDtypeStruct((nq, N_HEADS * HEAD_DIM), MXU_DTYPE),
        compiler_params=_params(("arbitrary", "arbitrary")),
        name="na_attention",
    )(qn, kn, p, bias)


def _merge_kernel(h_ref, y0_ref, y1_ref, y2_ref, y3_ref, wg_ref, bg_ref, wb_ref, o_ref):
    h = h_ref[...]
    acc = None
    for i, y_ref in enumerate((y0_ref, y1_ref, y2_ref, y3_ref)):
        gate = _sigmoid(_dot(h, wg_ref[i]) + bg_ref[i])
        term = gate * _dot(y_ref[...], wb_ref[i])
        acc = term if acc is None else acc + term
    o_ref[...] = acc.astype(o_ref.dtype)


def _merge(h, ys, wg, bg, wb, m_rows):
    d = h.shape[1]
    bw = ys[0].shape[1]
    tm = _pick(m_rows, (1280, 1024, 640, 512, 256))
    tn = 256
    once = pl.Buffered(1)
    return pl.pallas_call(
        _merge_kernel,
        grid=(m_rows // tm, d // tn),
        in_specs=[pl.BlockSpec((tm, d), lambda i, j: (i, 0), pipeline_mode=once)]
        + [pl.BlockSpec((tm, bw), lambda i, j: (i, 0), pipeline_mode=once)] * 4
        + [pl.BlockSpec((4, d, tn), lambda i, j: (0, 0, j)),
           pl.BlockSpec((4, 1, tn), lambda i, j: (0, 0, j)),
           pl.BlockSpec((4, bw, tn), lambda i, j: (0, 0, j))],
        out_specs=pl.BlockSpec((tm, tn), lambda i, j: (i, j)),
        out_shape=jax.ShapeDtypeStruct((m_rows, d), MXU_DTYPE),
        compiler_params=_params(("arbitrary", "arbitrary"), vmem_mb=56),
        name="merge",
    )(h, *ys, wg, bg.astype(F32).reshape(4, 1, d), wb)


def _router_kernel(h_ref, w_ref, b_ref, sel_ref, idx_ref, wk_ref):
    tr = h_ref.shape[0]
    per = N_EXPERTS // N_GROUPS
    scores = _sigmoid(_nt_dot(w_ref[...], h_ref[...]))
    choice = scores + b_ref[...]
    sub = lax.broadcasted_iota(I32, (per, tr), 0)
    neg_inf = -jnp.inf
    slabs = [choice[g * per:(g + 1) * per, :] for g in range(N_GROUPS)]
    grp = []
    for c in slabs:
        m1 = jnp.max(c, axis=0, keepdims=True)
        i1 = jnp.min(jnp.where(c == m1, sub, per), axis=0, keepdims=True)
        m2 = jnp.max(jnp.where(sub == i1, neg_inf, c), axis=0, keepdims=True)
        grp.append(m1 + m2)
    gsel = [jnp.zeros((1, tr), jnp.bool_) for _ in range(N_GROUPS)]
    for _ in range(TOPK_GROUPS):
        gm = functools.reduce(jnp.maximum, grp)
        found = jnp.zeros((1, tr), jnp.bool_)
        for g in range(N_GROUPS):
            hit = (grp[g] == gm) & jnp.logical_not(found)
            found = found | hit
            gsel[g] = gsel[g] | hit
            grp[g] = jnp.where(hit, neg_inf, grp[g])
    vals = [jnp.where(gsel[g], slabs[g], neg_inf) for g in range(N_GROUPS)]
    eidx = [sub + g * per for g in range(N_GROUPS)]
    sel = [jnp.zeros((per, tr), jnp.bool_) for _ in range(N_GROUPS)]
    row8 = lax.broadcasted_iota(I32, (TOP_K, tr), 0)
    idx_out = jnp.zeros((TOP_K, tr), I32)
    w_out = jnp.zeros((TOP_K, tr), F32)
    wsum = jnp.zeros((1, tr), F32)
    for k in range(TOP_K):
        vm = jnp.max(functools.reduce(jnp.maximum, vals), axis=0, keepdims=True)
        cand = [jnp.where(vals[g] == vm, eidx[g], N_EXPERTS) for g in range(N_GROUPS)]
        ik = jnp.min(functools.reduce(jnp.minimum, cand), axis=0, keepdims=True)
        wk = jnp.zeros((1, tr), F32)
        for g in range(N_GROUPS):
            hit = eidx[g] == ik
            sel[g] = sel[g] | hit
            vals[g] = jnp.where(hit, neg_inf, vals[g])
            sc = scores[g * per:(g + 1) * per, :]
            wk = wk + jnp.sum(jnp.where(hit, sc, 0.0), axis=0, keepdims=True)
        idx_out = jnp.where(row8 == k, ik, idx_out)
        w_out = jnp.where(row8 == k, wk, w_out)
        wsum = wsum + wk
    for g in range(N_GROUPS):
        sel_ref[g * per:(g + 1) * per, :] = sel[g].astype(I32)
    idx_ref[...] = idx_out
    wk_ref[...] = w_out / wsum * ROUTED_SCALE


def _router(h2, w_rt, b_r, m_rows):
    d = h2.shape[1]
    tr = 256
    return pl.pallas_call(
        _router_kernel,
        grid=(m_rows // tr,),
        in_specs=[pl.BlockSpec((tr, d), lambda i: (i, 0)),
                  pl.BlockSpec((N_EXPERTS, d), lambda i: (0, 0)),
                  pl.BlockSpec((N_EXPERTS, 1), lambda i: (0, 0))],
        out_specs=[pl.BlockSpec((N_EXPERTS, tr), lambda i: (0, i)),
                   pl.BlockSpec((TOP_K, tr), lambda i: (0, i)),
                   pl.BlockSpec((TOP_K, tr), lambda i: (0, i))],
        out_shape=[jax.ShapeDtypeStruct((N_EXPERTS, m_rows), I32),
                   jax.ShapeDtypeStruct((TOP_K, m_rows), I32),
                   jax.ShapeDtypeStruct((TOP_K, m_rows), F32)],
        compiler_params=_params(("arbitrary",)),
        name="router",
    )(h2, w_rt, b_r.astype(F32).reshape(N_EXPERTS, 1))


def _row_gather(tok_ref, n_rows, src_hbm, dst_buf, sem, start):
    def body(j, carry):
        for u in range(SUBLANES):
            r = j * SUBLANES + u
            t = tok_ref[0, 0, r] if start else 0
            cp = pltpu.make_async_copy(src_hbm.at[pl.ds(t, 1), :], dst_buf.at[pl.ds(r, 1), :], sem)
            if start:
                cp.start()
            else:
                cp.wait()
        return carry
    lax.fori_loop(0, n_rows // SUBLANES, body, 0)


def _expert_kernel(nused_ref, blke_ref, tok_ref, tokn_ref, hp_ref, wg_ref, wu_ref, wd_ref, sw_ref, y_ref,
                   xbuf, sem, wgu_c, wd_c):
    i = pl.program_id(0)
    nu = nused_ref[0]
    tm = xbuf.shape[1]
    half = xbuf.shape[2]
    slot = i % 2

    @pl.when(i == 0)
    def _():
        _row_gather(tok_ref, tm, hp_ref, xbuf.at[0], sem.at[0], True)

    @pl.when(i + 1 < nu)
    def _():
        _row_gather(tokn_ref, tm, hp_ref, xbuf.at[1 - slot], sem.at[1 - slot], True)

    @pl.when((i < nu) & ((i == 0) | (blke_ref[i] != blke_ref[jnp.maximum(i - 1, 0)])))
    def _():
        wgu_c[:, :EXPERT_DIM] = wg_ref[0, 0].astype(wgu_c.dtype)
        wgu_c[:, EXPERT_DIM:] = wu_ref[0, 0].astype(wgu_c.dtype)
        wd_c[...] = wd_ref[0, 0].astype(wd_c.dtype)

    @pl.when(i < nu)
    def _():
        _row_gather(tok_ref, tm, hp_ref, xbuf.at[slot], sem.at[slot], False)
        lo, hi = _unpack_halves(xbuf[slot])
        gu = (_dot(lo.astype(MXU_DTYPE), wgu_c[:half, :])
              + _dot(hi.astype(MXU_DTYPE), wgu_c[half:, :]))
        g = gu[:, :EXPERT_DIM]
        u = gu[:, EXPERT_DIM:]
        a = (g * _sigmoid(g) * u).astype(MXU_DTYPE)
        y = _dot(a, wd_c[...]) * sw_ref[...]
        y_ref[...] = _pack_halves(y)

    @pl.when(i >= nu)
    def _():
        y_ref[...] = jnp.zeros_like(y_ref)


def _experts(hp, wg, wu, wd, layer, n_used, blk_e, slot_tok, slot_w, n_blk):
    d2 = hp.shape[1]
    d = 2 * d2
    tm = MOE_TILE
    tok3 = slot_tok.reshape(n_blk, 1, tm)
    last = n_blk - 1
    grid_spec = pltpu.PrefetchScalarGridSpec(
        num_scalar_prefetch=2,
        grid=(n_blk,),
        in_specs=[pl.BlockSpec((1, 1, tm), lambda i, nu, be: (i, 0, 0), memory_space=pltpu.SMEM),
                  pl.BlockSpec((1, 1, tm), lambda i, nu, be: (jnp.minimum(i + 1, last), 0, 0),
                               memory_space=pltpu.SMEM),
                  pl.BlockSpec(memory_space=pl.ANY),
                  pl.BlockSpec((1, 1, d, EXPERT_DIM), lambda i, nu, be: (layer, be[i], 0, 0)),
                  pl.BlockSpec((1, 1, d, EXPERT_DIM), lambda i, nu, be: (layer, be[i], 0, 0)),
                  pl.BlockSpec((1, 1, EXPERT_DIM, d), lambda i, nu, be: (layer, be[i], 0, 0)),
                  pl.BlockSpec((tm, 1), lambda i, nu, be: (i, 0))],
        out_specs=pl.BlockSpec((tm, d2), lambda i, nu, be: (i, 0)),
        scratch_shapes=[pltpu.VMEM((2, tm, d2), U32), pltpu.SemaphoreType.DMA((2,)),
                        pltpu.VMEM((d, 2 * EXPERT_DIM), MXU_DTYPE), pltpu.VMEM((EXPERT_DIM, d), MXU_DTYPE)],
    )
    return pl.pallas_call(
        _expert_kernel,
        grid_spec=grid_spec,
        out_shape=jax.ShapeDtypeStruct((n_blk * tm, d2), U32),
        compiler_params=_params(("arbitrary",), vmem_mb=56),
        name="routed_experts",
    )(n_used, blk_e, tok3, tok3, hp, wg, wu, wd, slot_w.reshape(n_blk * tm, 1))


def _shared_kernel(h_ref, wgu_ref, wd_ref, o_ref):
    gu = _dot(h_ref[...], wgu_ref[...])
    hd = gu.shape[1] // 2
    g = gu[:, :hd]
    u = gu[:, hd:]
    a = (g * _sigmoid(g) * u).astype(MXU_DTYPE)
    o_ref[...] = _dot(a, wd_ref[...]).astype(o_ref.dtype)


def _shared_expert(h2, wgu, wd, m_rows):
    d = h2.shape[1]
    tm = 256
    return pl.pallas_call(
        _shared_kernel,
        grid=(m_rows // tm,),
        in_specs=[pl.BlockSpec((tm, d), lambda i: (i, 0)),
                  pl.BlockSpec(wgu.shape, lambda i: (0, 0)),
                  pl.BlockSpec(wd.shape, lambda i: (0, 0))],
        out_specs=pl.BlockSpec((tm, d), lambda i: (i, 0)),
        out_shape=jax.ShapeDtypeStruct((m_rows, d), MXU_DTYPE),
        compiler_params=_params(("arbitrary",)),
        name="shared_expert",
    )(h2, wgu, wd)


def _combine_kernel(n_lat, sl_ref, sln_ref, y_hbm, x_ref, sh_ref, g_ref, o_ref, buf, sem):
    i = pl.program_id(0)
    nsteps = pl.num_programs(0)
    tc = x_ref.shape[0]
    nrow = buf.shape[1]
    slot = i % 2

    @pl.when(i == 0)
    def _():
        _row_gather(sl_ref, nrow, y_hbm, buf.at[0], sem.at[0], True)

    @pl.when(i + 1 < nsteps)
    def _():
        _row_gather(sln_ref, nrow, y_hbm, buf.at[1 - slot], sem.at[1 - slot], True)

    _row_gather(sl_ref, nrow, y_hbm, buf.at[slot], sem.at[slot], False)
    lo_acc = None
    hi_acc = None
    for k in range(TOP_K):
        lo, hi = _unpack_halves(buf[slot, k * tc:(k + 1) * tc, :])
        lo_acc = lo if lo_acc is None else lo_acc + lo
        hi_acc = hi if hi_acc is None else hi_acc + hi
    routed = jnp.concatenate([lo_acc, hi_acc], axis=1)
    rows = i * tc + lax.broadcasted_iota(I32, (tc, 1), 0)
    g = jnp.where(rows >= n_lat, g_ref[1:2, :], g_ref[0:1, :])
    o_ref[...] = x_ref[...] + g * (routed + sh_ref[...].astype(F32))


def _combine(y_slots, slots, x1, sh, mod, gate_blk, n_lat, m_rows):
    d = x1.shape[1]
    tc = COMBINE_TILE
    nsteps = m_rows // tc
    last = nsteps - 1
    return pl.pallas_call(
        functools.partial(_combine_kernel, n_lat),
        grid=(nsteps,),
        in_specs=[pl.BlockSpec((1, 1, TOP_K * tc), lambda i: (i, 0, 0), memory_space=pltpu.SMEM),
                  pl.BlockSpec((1, 1, TOP_K * tc), lambda i: (jnp.minimum(i + 1, last), 0, 0),
                               memory_space=pltpu.SMEM),
                  pl.BlockSpec(memory_space=pl.ANY),
                  pl.BlockSpec((tc, d), lambda i: (i, 0)),
                  pl.BlockSpec((tc, d), lambda i: (i, 0)),
                  pl.BlockSpec((2, d), lambda i: (0, gate_blk))],
        out_specs=pl.BlockSpec((tc, d), lambda i: (i, 0)),
        out_shape=jax.ShapeDtypeStruct((m_rows, d), F32),
        scratch_shapes=[pltpu.VMEM((2, TOP_K * tc, d // 2), U32), pltpu.SemaphoreType.DMA((2,))],
        compiler_params=_params(("arbitrary",)),
        name="moe_combine",
    )(slots, slots, y_slots, x1, sh, mod)


def _dispatch_tables(sel, idx, wk, m_rows):
    tm = MOE_TILE
    tk = m_rows * TOP_K
    n_blk = (tk + tm - 1) // tm + N_EXPERTS
    cs = jnp.cumsum(sel, axis=1)
    counts = cs[:, -1]
    padded = ((counts + tm - 1) // tm) * tm
    pend = jnp.cumsum(padded)
    pstart = pend - padded
    dest_et = pstart[:, None] + cs - 1
    dest = jnp.take_along_axis(dest_et, idx, axis=0)
    blk_start = jnp.arange(n_blk, dtype=I32) * tm
    blk_e = jnp.minimum(jnp.sum((pend[None, :] <= blk_start[:, None]).astype(I32), axis=1), N_EXPERTS - 1)
    n_fill = n_blk * tm - tk - N_EXPERTS * tm
    f = jnp.arange(tk, dtype=I32)
    real_key = idx.T.reshape(-1) * (2 * tk) + f
    e_d = jnp.arange(N_EXPERTS, dtype=I32)[:, None]
    j_d = jnp.arange(tm, dtype=I32)[None, :]
    big = N_EXPERTS * 2 * tk
    pad_key = jnp.where(j_d < (padded - counts)[:, None], e_d * (2 * tk) + tk + j_d, big + e_d * tm + j_d)
    keys = jnp.concatenate([real_key, pad_key.reshape(-1), jnp.full((n_fill,), big + N_EXPERTS * tm, I32)])
    n_pad = keys.shape[0] - tk
    toks = jnp.concatenate([f // TOP_K, jnp.zeros((n_pad,), I32)])
    ws = jnp.concatenate([wk.T.reshape(-1), jnp.zeros((n_pad,), F32)])
    _, slot_tok, slot_w = lax.sort((keys, toks, ws), num_keys=1)
    n_used = (pend[-1] // tm).astype(I32).reshape(1)
    tc = COMBINE_TILE
    slots = dest.reshape(TOP_K, m_rows // tc, tc).transpose(1, 0, 2).reshape(m_rows // tc, 1, TOP_K * tc)
    return n_used, blk_e, slot_tok, slot_w, slots.astype(I32), n_blk


def kernel(x, c, ctx, c_ctx, w_mod, b_mod, g_norm1, g_norm2, w_in, conv_w, diff_qnorm, diff_knorm, diff_lambda, diff_subln, swa_qnorm, swa_knorm, swa_sink, na_qnorm, na_knorm, na_rpb, w_gate, b_gate, w_branch, w_out, w_router, b_router, w_exp_gate, w_exp_up, w_exp_down, w_sh_gate, w_sh_up, w_sh_down):
    bsz, n, d = x.shape
    assert bsz == 1
    n_ctx = ctx.shape[1]
    t_all = n + n_ctx
    depth = w_in.shape[0]
    bw = d // 4
    cdt = MXU_DTYPE

    xa = jnp.concatenate([x[0], ctx[0]], axis=0)
    cc = jnp.stack([c[0], c_ctx], axis=0).astype(F32)
    rope_d = _rope_tables(n, t_all, DIFF_QK // 2)
    rope_s = _rope_tables(n, t_all, HEAD_DIM // 2)

    o_conv = 0
    o_dq, o_dk, o_dv = 3 * bw, 4 * bw, 5 * bw
    o_sq = 6 * bw
    o_sk = o_sq + bw
    o_sv = o_sk + SWA_KV_HEADS * HEAD_DIM
    o_nq = o_sv + SWA_KV_HEADS * HEAD_DIM
    o_nk, o_nv = o_nq + bw, o_nq + 2 * bw

    for l in range(depth):
        update_ctx = l < depth - 1
        nq = t_all if update_ctx else n
        lam_init = 0.8 - 0.6 * math.exp(-0.3 * l)
        mod = _modvec(cc, w_mod, b_mod, l)

        h = _rmsmod(xa, g_norm1[l], mod, 0, 1, n, t_all)
        p = _matmul(h, w_in[l].astype(cdt), cdt, t_all)

        a = _short_conv(p, conv_w[l].astype(F32), n, nq)
        q1z, q2z = _prep(p, o_dq, N_HEADS, diff_qnorm[l], DIFF_QK, rope_d, split=True,
                         scale=DIFF_QK ** -0.5 * math.log2(math.e))
        kd = _prep(p, o_dk, N_HEADS, diff_knorm[l], DIFF_QK, rope_d)
        qs = _prep(p, o_sq, N_HEADS, swa_qnorm[l], HEAD_DIM, rope_s)
        ks = _prep(p, o_sk, SWA_KV_HEADS, swa_knorm[l], HEAD_DIM, rope_s)
        qn = _prep(p, o_nq, N_HEADS, na_qnorm[l], HEAD_DIM, None)
        kn = _prep(p, o_nk, N_HEADS, na_knorm[l], HEAD_DIM, None)

        v_d = p[:, o_dv:o_dv + bw].reshape(t_all, N_HEADS, HEAD_DIM)
        vx = jnp.concatenate([v_d, jnp.ones_like(v_d)], axis=-1).reshape(t_all, 2 * bw)
        yb = _diff_attention(q1z, q2z, kd, vx, diff_lambda[l], diff_subln[l], lam_init, n, True)
        if update_ctx:
            yb_c = _diff_attention(q1z, q2z, kd, vx, diff_lambda[l], diff_subln[l], lam_init, n, False)
            yb = jnp.concatenate([yb, yb_c], axis=0)
        ys = _swa_attention(qs, ks, p, o_sv, swa_sink[l], n, nq)
        yn = _na_attention(qn, kn, p, o_nv, _na_bias_table(na_rpb[l]), n, nq)

        acc = _merge(h, (a, yb, ys, yn), w_gate[l].astype(cdt), b_gate[l], w_branch[l].astype(cdt), nq)
        x1 = _matmul_residual(acc, w_out[l].astype(cdt), xa, mod, 2, n, nq)

        h2, h2p = _rmsmod(x1, g_norm2[l], mod, 3, 4, n, nq, packed=True)
        sel, idx, wk = _router(h2, w_router[l].T.astype(cdt), b_router[l], nq)
        n_used, blk_e, slot_tok, slot_w, slots, n_blk = _dispatch_tables(sel, idx, wk, nq)
        y_slots = _experts(h2p, w_exp_gate, w_exp_up, w_exp_down, l, n_used, blk_e, slot_tok, slot_w, n_blk)
        wsh = jnp.concatenate([w_sh_gate[l], w_sh_up[l]], axis=-1).astype(cdt)
        sh = _shared_expert(h2, wsh, w_sh_down[l].astype(cdt), nq)
        xa = _combine(y_slots, slots, x1, sh, mod, 5, n, nq)

    return xa[:n].reshape(bsz, n, d)
```

```python
import functools
import math

import jax
import jax.numpy as jnp
from jax import lax
from jax.experimental import pallas as pl
from jax.experimental.pallas import tpu as pltpu

GRID_W = 64
HEAD_DIM = 128
DIFF_QK = 64
SWA_WINDOW = 128
SWA_BLOCK = 128
SWA_KV_HEADS = 2
N_HEADS = 8
NA_ROWS = 8
NA_COLS = 16
ROPE_BASE = 10000.0
N_EXPERTS = 64
N_GROUPS = 8
TOPK_GROUPS = 4
TOP_K = 8
EXPERT_DIM = 256
ROUTED_SCALE = 2.5
EPS = 1e-6
NEG = -1e30

LANES = 128
SUBLANES = 8
V7X_VMEM_BYTES = 64 * 1024 * 1024

MXU_DTYPE = jnp.bfloat16
MOE_TILE = 256
COMBINE_TILE = 64

F32 = jnp.float32
I32 = jnp.int32
U32 = jnp.uint32


def _params(sem, vmem_mb=48):
    return pltpu.CompilerParams(dimension_semantics=sem, vmem_limit_bytes=vmem_mb * 1024 * 1024)


def _pick(m, cands):
    for c in cands:
        if m % c == 0:
            return c
    raise ValueError(f"no tile for {m} in {cands}")


def _nt_dot(a, b):
    return lax.dot_general(a, b, (((1,), (1,)), ((), ())), preferred_element_type=F32)


def _dot(a, b):
    return jnp.dot(a, b, preferred_element_type=F32)


def _sigmoid(x):
    return 1.0 / (1.0 + jnp.exp(-x))


def _modvec_kernel(s_ref, w_ref, b_ref, o_ref):
    _, d, tn = w_ref.shape
    rows = []
    for r in range(2):
        c = s_ref[r]
        s = c * _sigmoid(c)
        parts = []
        for j in range(tn // LANES):
            prod = w_ref[0, :, j * LANES:(j + 1) * LANES] * s
            part = jnp.sum(prod.reshape(d // SUBLANES, SUBLANES, LANES), axis=0)
            parts.append(jnp.sum(part, axis=0, keepdims=True))
        rows.append(jnp.concatenate(parts, axis=1))
    is_first = lax.broadcasted_iota(I32, (2, tn), 0) == 0
    o_ref[...] = jnp.where(is_first, rows[0], rows[1]) + b_ref[0]


def _modvec(cc, w, b, layer):
    depth, d, n = w.shape
    tn = 512
    s_rep = jnp.broadcast_to(cc[:, :, None], (2, d, LANES))
    return pl.pallas_call(
        _modvec_kernel,
        grid=(n // tn,),
        in_specs=[pl.BlockSpec((2, d, LANES), lambda j: (0, 0, 0)),
                  pl.BlockSpec((1, d, tn), lambda j: (layer, 0, j)),
                  pl.BlockSpec((1, 1, tn), lambda j: (layer, 0, j))],
        out_specs=pl.BlockSpec((2, tn), lambda j: (0, j)),
        out_shape=jax.ShapeDtypeStruct((2, n), F32),
        compiler_params=_params(("arbitrary",)),
        name="modvec",
    )(s_rep, w, b.reshape(depth, 1, n))


def _pack_halves(v):
    c = v.shape[1] // 2
    bits = lax.bitcast_convert_type(v.astype(jnp.bfloat16).astype(F32), U32)
    return (bits[:, c:] & jnp.uint32(0xFFFF0000)) | (bits[:, :c] >> 16)


def _unpack_halves(w):
    lo = lax.bitcast_convert_type(w << 16, F32)
    hi = lax.bitcast_convert_type(w & jnp.uint32(0xFFFF0000), F32)
    return lo, hi


def _rmsmod_kernel(n_lat, packed, x_ref, g_ref, sh_ref, sc_ref, o_ref, *p_ref):
    tr = x_ref.shape[0]
    x = x_ref[...]
    ms = jnp.mean(x * x, axis=-1, keepdims=True)
    y = x * lax.rsqrt(ms + EPS) * g_ref[...]
    rows = pl.program_id(0) * tr + lax.broadcasted_iota(I32, (tr, 1), 0)
    is_ctx = rows >= n_lat
    sh = jnp.where(is_ctx, sh_ref[1:2, :], sh_ref[0:1, :])
    sc = jnp.where(is_ctx, sc_ref[1:2, :], sc_ref[0:1, :])
    h = y * (1.0 + sc) + sh
    o_ref[...] = h.astype(o_ref.dtype)
    if packed:
        p_ref[0][...] = _pack_halves(h)


def _rmsmod(x, g, mod, shift_blk, scale_blk, n_lat, m_rows, packed=False):
    d = x.shape[1]
    tr = 256
    out_shape = [jax.ShapeDtypeStruct((m_rows, d), MXU_DTYPE)]
    out_specs = [pl.BlockSpec((tr, d), lambda i: (i, 0))]
    if packed:
        out_shape.append(jax.ShapeDtypeStruct((m_rows, d // 2), U32))
        out_specs.append(pl.BlockSpec((tr, d // 2), lambda i: (i, 0)))
    res = pl.pallas_call(
        functools.partial(_rmsmod_kernel, n_lat, packed),
        grid=(m_rows // tr,),
        in_specs=[pl.BlockSpec((tr, d), lambda i: (i, 0)),
                  pl.BlockSpec((1, d), lambda i: (0, 0)),
                  pl.BlockSpec((2, d), lambda i: (0, shift_blk)),
                  pl.BlockSpec((2, d), lambda i: (0, scale_blk))],
        out_specs=out_specs,
        out_shape=out_shape,
        compiler_params=_params(("arbitrary",)),
        name="rmsmod",
    )(x, g.reshape(1, d), mod, mod)
    return res if packed else res[0]


def _mm_kernel(a_ref, w_ref, o_ref):
    o_ref[...] = _dot(a_ref[...], w_ref[...]).astype(o_ref.dtype)


def _mm_res_kernel(n_lat, a_ref, w_ref, r_ref, g_ref, o_ref):
    tm = a_ref.shape[0]
    acc = _dot(a_ref[...], w_ref[...])
    rows = pl.program_id(0) * tm + lax.broadcasted_iota(I32, (tm, 1), 0)
    g = jnp.where(rows >= n_lat, g_ref[1:2, :], g_ref[0:1, :])
    o_ref[...] = r_ref[...] + g * acc


def _matmul(a, w, out_dtype, m_rows):
    k = a.shape[1]
    n = w.shape[1]
    tm = _pick(m_rows, (1280, 1024, 640, 512, 256))
    tn = 512
    return pl.pallas_call(
        _mm_kernel,
        grid=(m_rows // tm, n // tn),
        in_specs=[pl.BlockSpec((tm, k), lambda i, j: (i, 0)),
                  pl.BlockSpec((k, tn), lambda i, j: (0, j))],
        out_specs=pl.BlockSpec((tm, tn), lambda i, j: (i, j)),
        out_shape=jax.ShapeDtypeStruct((m_rows, n), out_dtype),
        compiler_params=_params(("arbitrary", "arbitrary")),
        name="matmul",
    )(a, w)


def _matmul_residual(a, w, res, mod, gate_blk, n_lat, m_rows):
    k = a.shape[1]
    n = w.shape[1]
    tm = _pick(m_rows, (1280, 1024, 640, 512, 256))
    tn = 512
    nb = n // tn
    return pl.pallas_call(
        functools.partial(_mm_res_kernel, n_lat),
        grid=(m_rows // tm, nb),
        in_specs=[pl.BlockSpec((tm, k), lambda i, j: (i, 0)),
                  pl.BlockSpec((k, tn), lambda i, j: (0, j)),
                  pl.BlockSpec((tm, tn), lambda i, j: (i, j)),
                  pl.BlockSpec((2, tn), lambda i, j: (0, gate_blk * nb + j))],
        out_specs=pl.BlockSpec((tm, tn), lambda i, j: (i, j)),
        out_shape=jax.ShapeDtypeStruct((m_rows, n), F32),
        compiler_params=_params(("arbitrary", "arbitrary")),
        name="matmul_residual",
    )(a, w, res, mod)


def _conv_kernel(n_lat, t_all, b_ref, c_ref, x_ref, cp_ref, xp_ref, cn_ref, xn_ref, w_ref, o_ref):
    tr = b_ref.shape[0]
    u = c_ref[...].astype(F32) * x_ref[...].astype(F32)
    u_prev_row = cp_ref[SUBLANES - 1:SUBLANES, :].astype(F32) * xp_ref[SUBLANES - 1:SUBLANES, :].astype(F32)
    u_next_row = cn_ref[0:1, :].astype(F32) * xn_ref[0:1, :].astype(F32)
    r = lax.broadcasted_iota(I32, (tr, 1), 0)
    tok = pl.program_id(0) * tr + r
    up = jnp.where(r == 0, u_prev_row, pltpu.roll(u, 1, 0))
    un = jnp.where(r == tr - 1, u_next_row, pltpu.roll(u, tr - 1, 0))
    up = jnp.where((tok == 0) | (tok == n_lat), 0.0, up)
    un = jnp.where((tok == n_lat - 1) | (tok == t_all - 1), 0.0, un)
    y = w_ref[0:1, :] * up + w_ref[1:2, :] * u + w_ref[2:3, :] * un
    o_ref[...] = (b_ref[...].astype(F32) * y).astype(o_ref.dtype)


def _short_conv(p, conv_w, n_lat, m_rows):
    t_all = p.shape[0]
    cw = conv_w.shape[1]
    tr = 256
    tc = 512
    nc = cw // tc
    rb = tr // SUBLANES
    last = t_all // SUBLANES - 1
    return pl.pallas_call(
        functools.partial(_conv_kernel, n_lat, t_all),
        grid=(m_rows // tr, nc),
        in_specs=[pl.BlockSpec((tr, tc), lambda i, j: (i, j)),
                  pl.BlockSpec((tr, tc), lambda i, j: (i, nc + j)),
                  pl.BlockSpec((tr, tc), lambda i, j: (i, 2 * nc + j)),
                  pl.BlockSpec((SUBLANES, tc), lambda i, j: (jnp.maximum(i * rb - 1, 0), nc + j)),
                  pl.BlockSpec((SUBLANES, tc), lambda i, j: (jnp.maximum(i * rb - 1, 0), 2 * nc + j)),
                  pl.BlockSpec((SUBLANES, tc), lambda i, j: (jnp.minimum((i + 1) * rb, last), nc + j)),
                  pl.BlockSpec((SUBLANES, tc), lambda i, j: (jnp.minimum((i + 1) * rb, last), 2 * nc + j)),
                  pl.BlockSpec((3, tc), lambda i, j: (0, j))],
        out_specs=pl.BlockSpec((tr, tc), lambda i, j: (i, j)),
        out_shape=jax.ShapeDtypeStruct((m_rows, cw), MXU_DTYPE),
        compiler_params=_params(("arbitrary", "arbitrary")),
        name="short_conv",
    )(p, p, p, p, p, p, p, conv_w)


def _prep_kernel(n_heads, group, half, split, scale, x_ref, g_ref, *rest):
    if half:
        cos_ref, sa_ref, sb_ref = rest[:3]
        outs = rest[3:]
    else:
        outs = rest
    lane = lax.broadcasted_iota(I32, (1, LANES), 1)
    lo = lane < DIFF_QK
    g = g_ref[...]
    for h in range(n_heads):
        x = x_ref[:, h * LANES:(h + 1) * LANES].astype(F32)
        sq = x * x
        if group == LANES:
            ms = jnp.mean(sq, axis=-1, keepdims=True)
        else:
            s_lo = jnp.sum(jnp.where(lo, sq, 0.0), axis=-1, keepdims=True)
            s_hi = jnp.sum(jnp.where(lo, 0.0, sq), axis=-1, keepdims=True)
            ms = jnp.where(lo, s_lo, s_hi) * (1.0 / group)
        y = x * lax.rsqrt(ms + EPS) * g
        if half:
            y = (y * cos_ref[...] + pltpu.roll(y, LANES - half, 1) * sa_ref[...]
                 + pltpu.roll(y, half, 1) * sb_ref[...])
        if scale != 1.0:
            y = y * scale
        if split:
            outs[0][:, h * LANES:(h + 1) * LANES] = jnp.where(lo, y, 0.0).astype(outs[0].dtype)
            outs[1][:, h * LANES:(h + 1) * LANES] = jnp.where(lo, 0.0, y).astype(outs[1].dtype)
        else:
            outs[0][:, h * LANES:(h + 1) * LANES] = y.astype(outs[0].dtype)


def _prep(p, col0, n_heads, gain, group, rope, split=False, scale=1.0):
    t_all = p.shape[0]
    tr = _pick(t_all, (640, 256))
    hpb = _pick(math.gcd(col0 // LANES, n_heads), (8, 4, 2, 1))
    w = hpb * LANES
    cb = col0 // w
    g = jnp.tile(gain.astype(F32), LANES // gain.shape[0]).reshape(1, LANES)
    in_specs = [pl.BlockSpec((tr, w), lambda i, j: (i, cb + j)),
                pl.BlockSpec((1, LANES), lambda i, j: (0, 0))]
    args = [p, g]
    half = 0
    if rope is not None:
        half = rope[3]
        in_specs += [pl.BlockSpec((tr, LANES), lambda i, j: (i, 0))] * 3
        args += list(rope[:3])
    n_out = 2 if split else 1
    res = pl.pallas_call(
        functools.partial(_prep_kernel, hpb, group, half, split, scale),
        grid=(t_all // tr, n_heads // hpb),
        in_specs=in_specs,
        out_specs=[pl.BlockSpec((tr, w), lambda i, j: (i, j))] * n_out,
        out_shape=[jax.ShapeDtypeStruct((t_all, n_heads * LANES), MXU_DTYPE)] * n_out,
        compiler_params=_params(("arbitrary", "arbitrary")),
        name="qk_prep",
    )(*args)
    return res if split else res[0]


def _rope_tables(n_lat, t_all, seg):
    half = seg // 2
    n_rows = n_lat // GRID_W
    lane = jnp.arange(LANES)
    freqs = ROPE_BASE ** (-jnp.arange(0, seg, 2, dtype=F32) / seg)
    f_lane = freqs[lane % half]
    use_col = ((lane // seg) % 2 == 1)[None, None, :]
    ang_r = jnp.arange(n_rows, dtype=F32)[:, None] * f_lane[None, :]
    ang_c = jnp.arange(GRID_W, dtype=F32)[:, None] * f_lane[None, :]

    def expand(fn):
        full = jnp.where(use_col, fn(ang_c)[None, :, :], fn(ang_r)[:, None, :])
        return full.reshape(n_lat, LANES)

    cos = expand(jnp.cos)
    sin = expand(jnp.sin)
    first = (lane % seg) < half
    sa = jnp.where(first[None, :], -sin, 0.0)
    sb = jnp.where(first[None, :], 0.0, sin)
    pad = t_all - n_lat
    cos = jnp.concatenate([cos, jnp.ones((pad, LANES), F32)], axis=0)
    sa = jnp.concatenate([sa, jnp.zeros((pad, LANES), F32)], axis=0)
    sb = jnp.concatenate([sb, jnp.zeros((pad, LANES), F32)], axis=0)
    return cos, sa, sb, half


DIFF_TQ = 512
DIFF_TK = 1024
DIFF_GROUP = 4


def _diff_kernel(n_pairs, tkc, lam_init, q1_ref, q2_ref, k_ref, vx_ref, dl_ref, gs_ref, o_ref, s_scr, mx_scr):
    tq = q1_ref.shape[0]
    t_all = k_ref.shape[0]
    n_lat = 2 * n_pairs * tkc
    q1 = q1_ref[...]
    q2 = q2_ref[...]

    def scores(kc):
        return _nt_dot(q1, kc), _nt_dot(q2, kc)

    def lane_max(s):
        return functools.reduce(jnp.maximum, [s[:, i * LANES:(i + 1) * LANES] for i in range(s.shape[1] // LANES)])

    def fold(s, smax, vc, m, a):
        mn = jnp.maximum(m, jnp.max(smax, axis=-1, keepdims=True))
        alpha = jnp.exp2(m - mn)
        p = jnp.exp2(s - mn).astype(vc.dtype)
        return mn, alpha * a + _dot(p, vc)

    def lat_scores_to(buf, c):
        off = pl.multiple_of(c * tkc, tkc)
        s1, s2 = scores(k_ref[pl.ds(off, tkc), :])
        s_scr[buf, 0] = s1
        s_scr[buf, 1] = s2
        mx_scr[buf, 0] = lane_max(s1)
        mx_scr[buf, 1] = lane_max(s2)

    def lat_fold_from(buf, c, carry):
        m1, a1, m2, a2 = carry
        off = pl.multiple_of(c * tkc, tkc)
        vc = vx_ref[pl.ds(off, tkc), :]
        m1, a1 = fold(s_scr[buf, 0], mx_scr[buf, 0], vc, m1, a1)
        m2, a2 = fold(s_scr[buf, 1], mx_scr[buf, 1], vc, m2, a2)
        return m1, a1, m2, a2

    minit = jnp.full((tq, 1), -jnp.inf, F32)
    ainit = jnp.zeros((tq, 2 * HEAD_DIM), F32)
    carry = (minit, ainit, minit, ainit)
    n_chunks = 2 * n_pairs
    group = _pick(n_chunks, (DIFF_GROUP, 4, 2)) if n_pairs > 0 else 0
    if n_pairs > 0:
        lat_scores_to(0, 0)

        def run_group(c0, carry, last):
            for u in range(group):
                if not (last and u == group - 1):
                    lat_scores_to((u + 1) % 2, c0 + u + 1)
                    carry = lat_fold_from(u % 2, c0 + u, carry)
            return carry

        carry = lax.fori_loop(0, n_chunks // group - 1,
                              lambda j, cr: run_group(j * group, cr, False), carry)
        carry = run_group(n_chunks - group, carry, True)
    s1c, s2c = scores(k_ref[n_lat:t_all, :])
    if n_pairs > 0:
        carry = lat_fold_from(1, n_chunks - 1, carry)
    m1, a1, m2, a2 = carry
    vcx = vx_ref[n_lat:t_all, :]
    m1, a1 = fold(s1c, lane_max(s1c), vcx, m1, a1)
    m2, a2 = fold(s2c, lane_max(s2c), vcx, m2, a2)

    dl = dl_ref[...]
    lam = (jnp.exp(jnp.sum(dl[0:1, :] * dl[1:2, :], axis=-1, keepdims=True))
           - jnp.exp(jnp.sum(dl[2:3, :] * dl[3:4, :], axis=-1, keepdims=True)) + lam_init)
    o = (a1[:, :HEAD_DIM] / a1[:, HEAD_DIM:HEAD_DIM + 1]
         - lam * (a2[:, :HEAD_DIM] / a2[:, HEAD_DIM:HEAD_DIM + 1]))
    ms = jnp.mean(o * o, axis=-1, keepdims=True)
    o = o * lax.rsqrt(ms + EPS) * gs_ref[...] * (1.0 - lam_init)
    o_ref[...] = o.astype(o_ref.dtype)


def _diff_attention(q1z, q2z, kd, vx, diff_lambda, subln, lam_init, n_lat, latent):
    t_all = kd.shape[0]
    n_ctx = t_all - n_lat
    if latent:
        tq, tkc, rows, blk0 = DIFF_TQ, DIFF_TK, n_lat, 0
        assert n_lat % (2 * tkc) == 0 and n_lat % tq == 0
        n_pairs = n_lat // (2 * tkc)
    else:
        tq, tkc, rows, blk0, n_pairs = n_ctx, DIFF_TK, n_ctx, n_lat // n_ctx, 0
        assert n_lat % n_ctx == 0
    if n_pairs == 0:
        k_spec = pl.BlockSpec((n_ctx, LANES), lambda h, i: (n_lat // n_ctx, h))
        v_spec = pl.BlockSpec((n_ctx, 2 * LANES), lambda h, i: (n_lat // n_ctx, h))
    else:
        k_spec = pl.BlockSpec((t_all, LANES), lambda h, i: (0, h))
        v_spec = pl.BlockSpec((t_all, 2 * LANES), lambda h, i: (0, h))
    return pl.pallas_call(
        functools.partial(_diff_kernel, n_pairs, tkc, lam_init),
        grid=(N_HEADS, rows // tq),
        in_specs=[pl.BlockSpec((tq, LANES), lambda h, i: (blk0 + i, h)),
                  pl.BlockSpec((tq, LANES), lambda h, i: (blk0 + i, h)),
                  k_spec, v_spec,
                  pl.BlockSpec((4, DIFF_QK), lambda h, i: (0, 0)),
                  pl.BlockSpec((1, HEAD_DIM), lambda h, i: (0, 0))],
        out_specs=pl.BlockSpec((tq, LANES), lambda h, i: (i, h)),
        out_shape=jax.ShapeDtypeStruct((rows, N_HEADS * HEAD_DIM), MXU_DTYPE),
        scratch_shapes=[pltpu.VMEM((2, 2, tq, tkc), F32), pltpu.VMEM((2, 2, tq, LANES), F32)],
        compiler_params=_params(("arbitrary", "arbitrary")),
        name="diff_attention" if latent else "diff_attention_ctx",
    )(q1z, q2z, kd, vx, diff_lambda.astype(F32), subln.astype(F32).reshape(1, HEAD_DIM))


SWA_BLOCKS_PER_STEP = 2


def _swa_kernel(n_lat, t_all, q_ref, k_ref, v_ref, sink_ref, o_ref):
    w = SWA_BLOCK
    g = q_ref.shape[1] // HEAD_DIM
    scale = HEAD_DIM ** -0.5
    kc = k_ref[n_lat:t_all, :]
    vc = v_ref[n_lat:t_all, :]
    sink = jnp.concatenate([jnp.broadcast_to(sink_ref[0, :, j:j + 1], (w, 1)) for j in range(g)], axis=0)
    for sb in range(SWA_BLOCKS_PER_STEP):
        b = pl.program_id(1) * SWA_BLOCKS_PER_STEP + sb
        reach = jnp.where(b * w < n_lat, SWA_WINDOW, -1)
        start = jnp.clip((b - 1) * w, 0, n_lat - 3 * w)
        start = pl.multiple_of(start, w)
        kw = k_ref[pl.ds(start, 3 * w), :]
        vw = v_ref[pl.ds(start, 3 * w), :]
        rows = slice(sb * w, (sb + 1) * w)
        q = jnp.concatenate([q_ref[rows, j * HEAD_DIM:(j + 1) * HEAD_DIM] for j in range(g)], axis=0)
        s_w = _nt_dot(q, kw) * scale
        s_c = _nt_dot(q, kc) * scale
        qpos = b * w + lax.broadcasted_iota(I32, (w, 3 * w), 0)
        kpos = start + lax.broadcasted_iota(I32, (w, 3 * w), 1)
        mask = jnp.abs(qpos - kpos) <= reach
        mask = jnp.concatenate([mask] * g, axis=0)
        s_w = jnp.where(mask, s_w, NEG)
        m = jnp.maximum(jnp.maximum(jnp.max(s_w, axis=-1, keepdims=True), jnp.max(s_c, axis=-1, keepdims=True)),
                        sink)
        p_w = jnp.exp(s_w - m)
        p_c = jnp.exp(s_c - m)
        l = jnp.sum(p_w, axis=-1, keepdims=True) + jnp.sum(p_c, axis=-1, keepdims=True) + jnp.exp(sink - m)
        inv = 1.0 / l
        o = _dot((p_w * inv).astype(vw.dtype), vw) + _dot((p_c * inv).astype(vc.dtype), vc)
        for j in range(g):
            o_ref[rows, j * HEAD_DIM:(j + 1) * HEAD_DIM] = o[j * w:(j + 1) * w, :].astype(o_ref.dtype)


def _swa_attention(qs, ks, p, v_col0, sink, n_lat, nq):
    t_all = ks.shape[0]
    g = N_HEADS // SWA_KV_HEADS
    vb = v_col0 // LANES
    sink3 = sink.astype(F32).reshape(SWA_KV_HEADS, 1, g)
    tq = SWA_BLOCKS_PER_STEP * SWA_BLOCK
    assert nq % tq == 0
    return pl.pallas_call(
        functools.partial(_swa_kernel, n_lat, t_all),
        grid=(SWA_KV_HEADS, nq // tq),
        in_specs=[pl.BlockSpec((tq, g * HEAD_DIM), lambda kv, b: (b, kv)),
                  pl.BlockSpec((t_all, LANES), lambda kv, b: (0, kv)),
                  pl.BlockSpec((t_all, LANES), lambda kv, b: (0, vb + kv)),
                  pl.BlockSpec((1, 1, g), lambda kv, b: (kv, 0, 0))],
        out_specs=pl.BlockSpec((tq, g * HEAD_DIM), lambda kv, b: (b, kv)),
        out_shape=jax.ShapeDtypeStruct((nq, N_HEADS * HEAD_DIM), MXU_DTYPE),
        compiler_params=_params(("arbitrary", "arbitrary")),
        name="swa_attention",
    )(qs, ks, p, sink3)


NA_QROWS = 4
NA_WIN_ROWS = NA_QROWS + NA_ROWS
NA_VARIANTS = 4


def _na_kernel(n_lat, t_all, q_ref, k_ref, v_ref, bias_ref, o_ref):
    n_rows = n_lat // GRID_W
    n_grp = n_rows // NA_QROWS
    win = NA_WIN_ROWS * GRID_W
    scale = HEAD_DIM ** -0.5
    g = pl.program_id(1)
    wrow = jnp.clip(NA_QROWS * g - NA_ROWS // 2, 0, n_rows - NA_WIN_ROWS)
    variant = jnp.where(g >= n_grp, 3, jnp.where(g == 0, 0, jnp.where(g == n_grp - 1, 2, 1)))
    off = pl.multiple_of(wrow * GRID_W, GRID_W)
    kn = k_ref[pl.ds(off, win), :]
    vn = v_ref[pl.ds(off, win), :]
    kc = k_ref[n_lat:t_all, :]
    vc = v_ref[n_lat:t_all, :]
    q = q_ref[...]
    s_n = _nt_dot(q, kn) * scale + bias_ref[0, variant]
    s_c = _nt_dot(q, kc) * scale
    m = jnp.maximum(jnp.max(s_n, axis=-1, keepdims=True), jnp.max(s_c, axis=-1, keepdims=True))
    p_n = jnp.exp(s_n - m)
    p_c = jnp.exp(s_c - m)
    inv = 1.0 / (jnp.sum(p_n, axis=-1, keepdims=True) + jnp.sum(p_c, axis=-1, keepdims=True))
    o = _dot((p_n * inv).astype(vn.dtype), vn) + _dot((p_c * inv).astype(vc.dtype), vc)
    o_ref[...] = o.astype(o_ref.dtype)


def _na_bias_table(rpb):
    h = rpb.shape[0]
    c = jnp.arange(GRID_W)
    c0 = jnp.clip(c - NA_COLS // 2, 0, GRID_W - NA_COLS)
    kc = jnp.arange(GRID_W)
    inwin = (kc[None, :] >= c0[:, None]) & (kc[None, :] < c0[:, None] + NA_COLS)
    col_off = jnp.clip(kc[None, :] - c[:, None] + NA_COLS - 1, 0, 2 * NA_COLS - 2)
    toe = jnp.where(inwin[None, None], rpb.astype(F32)[:, :, col_off], NEG)
    neg_blk = jnp.full((h, GRID_W, GRID_W), NEG, F32)
    half = NA_ROWS // 2
    variants = []
    for shift, lo, hi in ((0, 0, NA_ROWS), (-half, None, None), (-NA_ROWS, half, NA_WIN_ROWS)):
        rows = []
        for j in range(NA_QROWS):
            blocks = []
            for a in range(NA_WIN_ROWS):
                dr = a - j + shift
                if lo is None:
                    valid = -half <= dr < half
                else:
                    valid = lo <= a < hi
                blocks.append(toe[:, dr + NA_ROWS - 1] if valid else neg_blk)
            rows.append(jnp.concatenate(blocks, axis=-1))
        variants.append(jnp.concatenate(rows, axis=1))
    variants.append(jnp.full_like(variants[0], NEG))
    return jnp.stack(variants, axis=1)


def _na_attention(qn, kn, p, v_col0, bias, n_lat, nq):
    t_all = kn.shape[0]
    vb = v_col0 // LANES
    tq = NA_QROWS * GRID_W
    assert n_lat // GRID_W >= NA_WIN_ROWS and (n_lat // GRID_W) % NA_QROWS == 0 and (t_all - n_lat) % tq == 0
    return pl.pallas_call(
        functools.partial(_na_kernel, n_lat, t_all),
        grid=(N_HEADS, nq // tq),
        in_specs=[pl.BlockSpec((tq, LANES), lambda h, i: (i, h)),
                  pl.BlockSpec((t_all, LANES), lambda h, i: (0, h)),
                  pl.BlockSpec((t_all, LANES), lambda h, i: (0, vb + h)),
                  pl.BlockSpec((1, NA_VARIANTS, tq, NA_WIN_ROWS * GRID_W), lambda h, i: (h, 0, 0, 0))],
        out_specs=pl.BlockSpec((tq, LANES), lambda h, i: (i, h)),
        out_shape=jax.ShapeDtypeStruct((nq, N_HEADS * HEAD_DIM), MXU_DTYPE),
        compiler_params=_params(("arbitrary", "arbitrary")),
        name="na_attention",
    )(qn, kn, p, bias)


def _merge_kernel(h_ref, y0_ref, y1_ref, y2_ref, y3_ref, wg_ref, bg_ref, wb_ref, o_ref):
    h = h_ref[...]
    acc = None
    for i, y_ref in enumerate((y0_ref, y1_ref, y2_ref, y3_ref)):
        gate = _sigmoid(_dot(h, wg_ref[i]) + bg_ref[i])
        term = gate * _dot(y_ref[...], wb_ref[i])
        acc = term if acc is None else acc + term
    o_ref[...] = acc.astype(o_ref.dtype)


def _merge(h, ys, wg, bg, wb, m_rows):
    d = h.shape[1]
    bw = ys[0].shape[1]
    tm = _pick(m_rows, (1280, 1024, 640, 512, 256))
    tn = 256
    once = pl.Buffered(1)
    return pl.pallas_call(
        _merge_kernel,
        grid=(m_rows // tm, d // tn),
        in_specs=[pl.BlockSpec((tm, d), lambda i, j: (i, 0), pipeline_mode=once)]
        + [pl.BlockSpec((tm, bw), lambda i, j: (i, 0), pipeline_mode=once)] * 4
        + [pl.BlockSpec((4, d, tn), lambda i, j: (0, 0, j)),
           pl.BlockSpec((4, 1, tn), lambda i, j: (0, 0, j)),
           pl.BlockSpec((4, bw, tn), lambda i, j: (0, 0, j))],
        out_specs=pl.BlockSpec((tm, tn), lambda i, j: (i, j)),
        out_shape=jax.ShapeDtypeStruct((m_rows, d), MXU_DTYPE),
        compiler_params=_params(("arbitrary", "arbitrary"), vmem_mb=56),
        name="merge",
    )(h, *ys, wg, bg.astype(F32).reshape(4, 1, d), wb)


def _router_kernel(h_ref, w_ref, b_ref, sel_ref, idx_ref, wk_ref):
    tr = h_ref.shape[0]
    per = N_EXPERTS // N_GROUPS
    scores = _sigmoid(_nt_dot(w_ref[...], h_ref[...]))
    choice = scores + b_ref[...]
    sub = lax.broadcasted_iota(I32, (per, tr), 0)
    neg_inf = -jnp.inf
    slabs = [choice[g * per:(g + 1) * per, :] for g in range(N_GROUPS)]
    grp = []
    for c in slabs:
        m1 = jnp.max(c, axis=0, keepdims=True)
        i1 = jnp.min(jnp.where(c == m1, sub, per), axis=0, keepdims=True)
        m2 = jnp.max(jnp.where(sub == i1, neg_inf, c), axis=0, keepdims=True)
        grp.append(m1 + m2)
    gsel = [jnp.zeros((1, tr), jnp.bool_) for _ in range(N_GROUPS)]
    for _ in range(TOPK_GROUPS):
        gm = functools.reduce(jnp.maximum, grp)
        found = jnp.zeros((1, tr), jnp.bool_)
        for g in range(N_GROUPS):
            hit = (grp[g] == gm) & jnp.logical_not(found)
            found = found | hit
            gsel[g] = gsel[g] | hit
            grp[g] = jnp.where(hit, neg_inf, grp[g])
    vals = [jnp.where(gsel[g], slabs[g], neg_inf) for g in range(N_GROUPS)]
    eidx = [sub + g * per for g in range(N_GROUPS)]
    sel = [jnp.zeros((per, tr), jnp.bool_) for _ in range(N_GROUPS)]
    row8 = lax.broadcasted_iota(I32, (TOP_K, tr), 0)
    idx_out = jnp.zeros((TOP_K, tr), I32)
    w_out = jnp.zeros((TOP_K, tr), F32)
    wsum = jnp.zeros((1, tr), F32)
    for k in range(TOP_K):
        vm = jnp.max(functools.reduce(jnp.maximum, vals), axis=0, keepdims=True)
        cand = [jnp.where(vals[g] == vm, eidx[g], N_EXPERTS) for g in range(N_GROUPS)]
        ik = jnp.min(functools.reduce(jnp.minimum, cand), axis=0, keepdims=True)
        wk = jnp.zeros((1, tr), F32)
        for g in range(N_GROUPS):
            hit = eidx[g] == ik
            sel[g] = sel[g] | hit
            vals[g] = jnp.where(hit, neg_inf, vals[g])
            sc = scores[g * per:(g + 1) * per, :]
            wk = wk + jnp.sum(jnp.where(hit, sc, 0.0), axis=0, keepdims=True)
        idx_out = jnp.where(row8 == k, ik, idx_out)
        w_out = jnp.where(row8 == k, wk, w_out)
        wsum = wsum + wk
    for g in range(N_GROUPS):
        sel_ref[g * per:(g + 1) * per, :] = sel[g].astype(I32)
    idx_ref[...] = idx_out
    wk_ref[...] = w_out / wsum * ROUTED_SCALE


def _router(h2, w_rt, b_r, m_rows):
    d = h2.shape[1]
    tr = 256
    return pl.pallas_call(
        _router_kernel,
        grid=(m_rows // tr,),
        in_specs=[pl.BlockSpec((tr, d), lambda i: (i, 0)),
                  pl.BlockSpec((N_EXPERTS, d), lambda i: (0, 0)),
                  pl.BlockSpec((N_EXPERTS, 1), lambda i: (0, 0))],
        out_specs=[pl.BlockSpec((N_EXPERTS, tr), lambda i: (0, i)),
                   pl.BlockSpec((TOP_K, tr), lambda i: (0, i)),
                   pl.BlockSpec((TOP_K, tr), lambda i: (0, i))],
        out_shape=[jax.ShapeDtypeStruct((N_EXPERTS, m_rows), I32),
                   jax.ShapeDtypeStruct((TOP_K, m_rows), I32),
                   jax.ShapeDtypeStruct((TOP_K, m_rows), F32)],
        compiler_params=_params(("arbitrary",)),
        name="router",
    )(h2, w_rt, b_r.astype(F32).reshape(N_EXPERTS, 1))


def _row_gather(tok_ref, n_rows, src_hbm, dst_buf, sem, start):
    def body(j, carry):
        for u in range(SUBLANES):
            r = j * SUBLANES + u
            t = tok_ref[0, 0, r] if start else 0
            cp = pltpu.make_async_copy(src_hbm.at[pl.ds(t, 1), :], dst_buf.at[pl.ds(r, 1), :], sem)
            if start:
                cp.start()
            else:
                cp.wait()
        return carry
    lax.fori_loop(0, n_rows // SUBLANES, body, 0)


def _expert_kernel(nused_ref, blke_ref, tok_ref, tokn_ref, hp_ref, wg_ref, wu_ref, wd_ref, sw_ref, y_ref,
                   xbuf, sem, wgu_c, wd_c):
    i = pl.program_id(0)
    nu = nused_ref[0]
    tm = xbuf.shape[1]
    half = xbuf.shape[2]
    slot = i % 2

    @pl.when(i == 0)
    def _():
        _row_gather(tok_ref, tm, hp_ref, xbuf.at[0], sem.at[0], True)

    @pl.when(i + 1 < nu)
    def _():
        _row_gather(tokn_ref, tm, hp_ref, xbuf.at[1 - slot], sem.at[1 - slot], True)

    @pl.when((i < nu) & ((i == 0) | (blke_ref[i] != blke_ref[jnp.maximum(i - 1, 0)])))
    def _():
        wgu_c[:, :EXPERT_DIM] = wg_ref[0, 0].astype(wgu_c.dtype)
        wgu_c[:, EXPERT_DIM:] = wu_ref[0, 0].astype(wgu_c.dtype)
        wd_c[...] = wd_ref[0, 0].astype(wd_c.dtype)

    @pl.when(i < nu)
    def _():
        _row_gather(tok_ref, tm, hp_ref, xbuf.at[slot], sem.at[slot], False)
        lo, hi = _unpack_halves(xbuf[slot])
        gu = (_dot(lo.astype(MXU_DTYPE), wgu_c[:half, :])
              + _dot(hi.astype(MXU_DTYPE), wgu_c[half:, :]))
        g = gu[:, :EXPERT_DIM]
        u = gu[:, EXPERT_DIM:]
        a = (g * _sigmoid(g) * u).astype(MXU_DTYPE)
        y = _dot(a, wd_c[...]) * sw_ref[...]
        y_ref[...] = _pack_halves(y)

    @pl.when(i >= nu)
    def _():
        y_ref[...] = jnp.zeros_like(y_ref)


def _experts(hp, wg, wu, wd, layer, n_used, blk_e, slot_tok, slot_w, n_blk):
    d2 = hp.shape[1]
    d = 2 * d2
    tm = MOE_TILE
    tok3 = slot_tok.reshape(n_blk, 1, tm)
    last = n_blk - 1
    grid_spec = pltpu.PrefetchScalarGridSpec(
        num_scalar_prefetch=2,
        grid=(n_blk,),
        in_specs=[pl.BlockSpec((1, 1, tm), lambda i, nu, be: (i, 0, 0), memory_space=pltpu.SMEM),
                  pl.BlockSpec((1, 1, tm), lambda i, nu, be: (jnp.minimum(i + 1, last), 0, 0),
                               memory_space=pltpu.SMEM),
                  pl.BlockSpec(memory_space=pl.ANY),
                  pl.BlockSpec((1, 1, d, EXPERT_DIM), lambda i, nu, be: (layer, be[i], 0, 0)),
                  pl.BlockSpec((1, 1, d, EXPERT_DIM), lambda i, nu, be: (layer, be[i], 0, 0)),
                  pl.BlockSpec((1, 1, EXPERT_DIM, d), lambda i, nu, be: (layer, be[i], 0, 0)),
                  pl.BlockSpec((tm, 1), lambda i, nu, be: (i, 0))],
        out_specs=pl.BlockSpec((tm, d2), lambda i, nu, be: (i, 0)),
        scratch_shapes=[pltpu.VMEM((2, tm, d2), U32), pltpu.SemaphoreType.DMA((2,)),
                        pltpu.VMEM((d, 2 * EXPERT_DIM), MXU_DTYPE), pltpu.VMEM((EXPERT_DIM, d), MXU_DTYPE)],
    )
    return pl.pallas_call(
        _expert_kernel,
        grid_spec=grid_spec,
        out_shape=jax.ShapeDtypeStruct((n_blk * tm, d2), U32),
        compiler_params=_params(("arbitrary",), vmem_mb=56),
        name="routed_experts",
    )(n_used, blk_e, tok3, tok3, hp, wg, wu, wd, slot_w.reshape(n_blk * tm, 1))


def _shared_kernel(h_ref, wgu_ref, wd_ref, o_ref):
    gu = _dot(h_ref[...], wgu_ref[...])
    hd = gu.shape[1] // 2
    g = gu[:, :hd]
    u = gu[:, hd:]
    a = (g * _sigmoid(g) * u).astype(MXU_DTYPE)
    o_ref[...] = _dot(a, wd_ref[...]).astype(o_ref.dtype)


def _shared_expert(h2, wgu, wd, m_rows):
    d = h2.shape[1]
    tm = 256
    return pl.pallas_call(
        _shared_kernel,
        grid=(m_rows // tm,),
        in_specs=[pl.BlockSpec((tm, d), lambda i: (i, 0)),
                  pl.BlockSpec(wgu.shape, lambda i: (0, 0)),
                  pl.BlockSpec(wd.shape, lambda i: (0, 0))],
        out_specs=pl.BlockSpec((tm, d), lambda i: (i, 0)),
        out_shape=jax.ShapeDtypeStruct((m_rows, d), MXU_DTYPE),
        compiler_params=_params(("arbitrary",)),
        name="shared_expert",
    )(h2, wgu, wd)


def _combine_kernel(n_lat, sl_ref, sln_ref, y_hbm, x_ref, sh_ref, g_ref, o_ref, buf, sem):
    i = pl.program_id(0)
    nsteps = pl.num_programs(0)
    tc = x_ref.shape[0]
    nrow = buf.shape[1]
    slot = i % 2

    @pl.when(i == 0)
    def _():
        _row_gather(sl_ref, nrow, y_hbm, buf.at[0], sem.at[0], True)

    @pl.when(i + 1 < nsteps)
    def _():
        _row_gather(sln_ref, nrow, y_hbm, buf.at[1 - slot], sem.at[1 - slot], True)

    _row_gather(sl_ref, nrow, y_hbm, buf.at[slot], sem.at[slot], False)
    lo_acc = None
    hi_acc = None
    for k in range(TOP_K):
        lo, hi = _unpack_halves(buf[slot, k * tc:(k + 1) * tc, :])
        lo_acc = lo if lo_acc is None else lo_acc + lo
        hi_acc = hi if hi_acc is None else hi_acc + hi
    routed = jnp.concatenate([lo_acc, hi_acc], axis=1)
    rows = i * tc + lax.broadcasted_iota(I32, (tc, 1), 0)
    g = jnp.where(rows >= n_lat, g_ref[1:2, :], g_ref[0:1, :])
    o_ref[...] = x_ref[...] + g * (routed + sh_ref[...].astype(F32))


def _combine(y_slots, slots, x1, sh, mod, gate_blk, n_lat, m_rows):
    d = x1.shape[1]
    tc = COMBINE_TILE
    nsteps = m_rows // tc
    last = nsteps - 1
    return pl.pallas_call(
        functools.partial(_combine_kernel, n_lat),
        grid=(nsteps,),
        in_specs=[pl.BlockSpec((1, 1, TOP_K * tc), lambda i: (i, 0, 0), memory_space=pltpu.SMEM),
                  pl.BlockSpec((1, 1, TOP_K * tc), lambda i: (jnp.minimum(i + 1, last), 0, 0),
                               memory_space=pltpu.SMEM),
                  pl.BlockSpec(memory_space=pl.ANY),
                  pl.BlockSpec((tc, d), lambda i: (i, 0)),
                  pl.BlockSpec((tc, d), lambda i: (i, 0)),
                  pl.BlockSpec((2, d), lambda i: (0, gate_blk))],
        out_specs=pl.BlockSpec((tc, d), lambda i: (i, 0)),
        out_shape=jax.ShapeDtypeStruct((m_rows, d), F32),
        scratch_shapes=[pltpu.VMEM((2, TOP_K * tc, d // 2), U32), pltpu.SemaphoreType.DMA((2,))],
        compiler_params=_params(("arbitrary",)),
        name="moe_combine",
    )(slots, slots, y_slots, x1, sh, mod)


def _dispatch_tables(sel, idx, wk, m_rows):
    tm = MOE_TILE
    tk = m_rows * TOP_K
    n_blk = (tk + tm - 1) // tm + N_EXPERTS
    cs = jnp.cumsum(sel, axis=1)
    counts = cs[:, -1]
    padded = ((counts + tm - 1) // tm) * tm
    pend = jnp.cumsum(padded)
    pstart = pend - padded
    dest_et = pstart[:, None] + cs - 1
    dest = jnp.take_along_axis(dest_et, idx, axis=0)
    blk_start = jnp.arange(n_blk, dtype=I32) * tm
    blk_e = jnp.minimum(jnp.sum((pend[None, :] <= blk_start[:, None]).astype(I32), axis=1), N_EXPERTS - 1)
    n_fill = n_blk * tm - tk - N_EXPERTS * tm
    f = jnp.arange(tk, dtype=I32)
    real_key = idx.T.reshape(-1) * (2 * tk) + f
    e_d = jnp.arange(N_EXPERTS, dtype=I32)[:, None]
    j_d = jnp.arange(tm, dtype=I32)[None, :]
    big = N_EXPERTS * 2 * tk
    pad_key = jnp.where(j_d < (padded - counts)[:, None], e_d * (2 * tk) + tk + j_d, big + e_d * tm + j_d)
    keys = jnp.concatenate([real_key, pad_key.reshape(-1), jnp.full((n_fill,), big + N_EXPERTS * tm, I32)])
    n_pad = keys.shape[0] - tk
    toks = jnp.concatenate([f // TOP_K, jnp.zeros((n_pad,), I32)])
    ws = jnp.concatenate([wk.T.reshape(-1), jnp.zeros((n_pad,), F32)])
    _, slot_tok, slot_w = lax.sort((keys, toks, ws), num_keys=1)
    n_used = (pend[-1] // tm).astype(I32).reshape(1)
    tc = COMBINE_TILE
    slots = dest.reshape(TOP_K, m_rows // tc, tc).transpose(1, 0, 2).reshape(m_rows // tc, 1, TOP_K * tc)
    return n_used, blk_e, slot_tok, slot_w, slots.astype(I32), n_blk


def kernel(x, c, ctx, c_ctx, w_mod, b_mod, g_norm1, g_norm2, w_in, conv_w, diff_qnorm, diff_knorm, diff_lambda, diff_subln, swa_qnorm, swa_knorm, swa_sink, na_qnorm, na_knorm, na_rpb, w_gate, b_gate, w_branch, w_out, w_router, b_router, w_exp_gate, w_exp_up, w_exp_down, w_sh_gate, w_sh_up, w_sh_down):
    bsz, n, d = x.shape
    assert bsz == 1
    n_ctx = ctx.shape[1]
    t_all = n + n_ctx
    depth = w_in.shape[0]
    bw = d // 4
    cdt = MXU_DTYPE

    xa = jnp.concatenate([x[0], ctx[0]], axis=0)
    cc = jnp.stack([c[0], c_ctx], axis=0).astype(F32)
    rope_d = _rope_tables(n, t_all, DIFF_QK // 2)
    rope_s = _rope_tables(n, t_all, HEAD_DIM // 2)

    o_conv = 0
    o_dq, o_dk, o_dv = 3 * bw, 4 * bw, 5 * bw
    o_sq = 6 * bw
    o_sk = o_sq + bw
    o_sv = o_sk + SWA_KV_HEADS * HEAD_DIM
    o_nq = o_sv + SWA_KV_HEADS * HEAD_DIM
    o_nk, o_nv = o_nq + bw, o_nq + 2 * bw

    for l in range(depth):
        update_ctx = l < depth - 1
        nq = t_all if update_ctx else n
        lam_init = 0.8 - 0.6 * math.exp(-0.3 * l)
        mod = _modvec(cc, w_mod, b_mod, l)

        h = _rmsmod(xa, g_norm1[l], mod, 0, 1, n, t_all)
        p = _matmul(h, w_in[l].astype(cdt), cdt, t_all)

        a = _short_conv(p, conv_w[l].astype(F32), n, nq)
        q1z, q2z = _prep(p, o_dq, N_HEADS, diff_qnorm[l], DIFF_QK, rope_d, split=True,
                         scale=DIFF_QK ** -0.5 * math.log2(math.e))
        kd = _prep(p, o_dk, N_HEADS, diff_knorm[l], DIFF_QK, rope_d)
        qs = _prep(p, o_sq, N_HEADS, swa_qnorm[l], HEAD_DIM, rope_s)
        ks = _prep(p, o_sk, SWA_KV_HEADS, swa_knorm[l], HEAD_DIM, rope_s)
        qn = _prep(p, o_nq, N_HEADS, na_qnorm[l], HEAD_DIM, None)
        kn = _prep(p, o_nk, N_HEADS, na_knorm[l], HEAD_DIM, None)

        v_d = p[:, o_dv:o_dv + bw].reshape(t_all, N_HEADS, HEAD_DIM)
        vx = jnp.concatenate([v_d, jnp.ones_like(v_d)], axis=-1).reshape(t_all, 2 * bw)
        yb = _diff_attention(q1z, q2z, kd, vx, diff_lambda[l], diff_subln[l], lam_init, n, True)
        if update_ctx:
            yb_c = _diff_attention(q1z, q2z, kd, vx, diff_lambda[l], diff_subln[l], lam_init, n, False)
            yb = jnp.concatenate([yb, yb_c], axis=0)
        ys = _swa_attention(qs, ks, p, o_sv, swa_sink[l], n, nq)
        yn = _na_attention(qn, kn, p, o_nv, _na_bias_table(na_rpb[l]), n, nq)

        acc = _merge(h, (a, yb, ys, yn), w_gate[l].astype(cdt), b_gate[l], w_branch[l].astype(cdt), nq)
        x1 = _matmul_residual(acc, w_out[l].astype(cdt), xa, mod, 2, n, nq)

        h2, h2p = _rmsmod(x1, g_norm2[l], mod, 3, 4, n, nq, packed=True)
        sel, idx, wk = _router(h2, w_router[l].T.astype(cdt), b_router[l], nq)
        n_used, blk_e, slot_tok, slot_w, slots, n_blk = _dispatch_tables(sel, idx, wk, nq)
        y_slots = _experts(h2p, w_exp_gate, w_exp_up, w_exp_down, l, n_used, blk_e, slot_tok, slot_w, n_blk)
        wsh = jnp.concatenate([w_sh_gate[l], w_sh_up[l]], axis=-1).astype(cdt)
        sh = _shared_expert(h2, wsh, w_sh_down[l].astype(cdt), nq)
        xa = _combine(y_slots, slots, x1, sh, mod, 5, n, nq)

    return xa[:n].reshape(bsz, n, d)
```
